```python
import math
import jax, jax.numpy as jnp
from jax import lax
import numpy as np

D_MODEL = 2048
BATCH = 2
SEQ = 16384
DEPTH = 2

N_BRANCH = 4
BRANCH_W = D_MODEL // N_BRANCH
A_HEAD_DIM = 64
A_HEADS = BRANCH_W // A_HEAD_DIM
DILATED_PATTERNS = ((128, 1), (512, 4), (2048, 16))
BAND = 128
DIFF_QK_DIM = 64
DIFF_V_DIM = 2 * DIFF_QK_DIM
B_HEADS = BRANCH_W // DIFF_V_DIM
Q_BLOCK = 128
DN_HEAD_K = 128
DN_HEAD_V = 128
C_HEADS = BRANCH_W // DN_HEAD_V
CONV_K = 4
CHUNK = 64
HG_EXPAND = 128
HG_HEAD_V = 128
D_HEADS = BRANCH_W // HG_HEAD_V
NUM_BUCKETS = 32
MAX_DISTANCE = 2048
RMS_EPS = 1e-6

IN_SPLITS = (
    A_HEADS * A_HEAD_DIM, A_HEADS * A_HEAD_DIM, A_HEADS * A_HEAD_DIM, BRANCH_W,
    B_HEADS * 2 * DIFF_QK_DIM, B_HEADS * 2 * DIFF_QK_DIM, B_HEADS * DIFF_V_DIM, BRANCH_W,
    C_HEADS * (2 * DN_HEAD_K + DN_HEAD_V), BRANCH_W, C_HEADS, C_HEADS,
    D_HEADS * HG_EXPAND, D_HEADS * HG_EXPAND, D_HEADS * HG_HEAD_V, BRANCH_W,
    N_BRANCH * D_MODEL,
)
N_IN_COLS = sum(IN_SPLITS)

kernel_name = "hybrid_gated_four_mixer_block"


def rms_norm(x, gain):
    x32 = x.astype(jnp.float32)
    y = x32 * lax.rsqrt(jnp.mean(x32 * x32, axis=-1, keepdims=True) + RMS_EPS)
    return y * gain.astype(jnp.float32)


def l2norm(x):
    return x * lax.rsqrt(jnp.sum(x * x, axis=-1, keepdims=True) + 1e-6)


def t5_bucket(dist):
    n = jnp.maximum(dist, 0)
    max_exact = NUM_BUCKETS // 2
    nf = jnp.maximum(n, max_exact).astype(jnp.float32)
    large = max_exact + (jnp.log(nf / max_exact) / math.log(MAX_DISTANCE / max_exact)
                         * (NUM_BUCKETS - max_exact)).astype(jnp.int32)
    large = jnp.minimum(large, NUM_BUCKETS - 1)
    return jnp.where(n < max_exact, n, large)


def dilated_window_attention(q, k, v, bias_table):
    Bsz, T, H, dh = q.shape
    scale = dh ** -0.5
    qi = jnp.arange(BAND)[:, None]
    kj = jnp.arange(2 * BAND)[None, :]
    rel = qi + BAND - kj
    outs, lses = [], []
    for window, dil in DILATED_PATTERNS:
        n = T // dil
        nb = -(-n // BAND)
        n_pad = nb * BAND

        def to_blocks(t):
            t = t.reshape(Bsz, n, dil, H, dh).transpose(0, 2, 1, 3, 4)
            t = jnp.pad(t, ((0, 0), (0, 0), (0, n_pad - n), (0, 0), (0, 0)))
            return t.reshape(Bsz, dil, nb, BAND, H, dh)

        def with_prev(t):
            prev = jnp.pad(t[:, :, :-1], ((0, 0), (0, 0), (1, 0), (0, 0), (0, 0), (0, 0)))
            return jnp.concatenate([prev, t], axis=3)

        qb = to_blocks(q)
        kb = with_prev(to_blocks(k))
        vb = with_prev(to_blocks(v))
        bias = bias_table[t5_bucket(rel * dil)].transpose(2, 0, 1)
        key_idx = jnp.arange(nb)[:, None, None] * BAND + kj[None] - BAND
        valid = ((rel >= 0) & (rel <= window // dil))[None] & (key_idx >= 0)
        logits = jnp.einsum('bcnqhd,bcnkhd->bcnhqk', qb, kb) * scale + bias
        logits = jnp.where(valid[:, None], logits, -jnp.inf)
        m = jnp.max(logits, axis=-1, keepdims=True)
        p = jnp.exp(logits - m)
        l = jnp.sum(p, axis=-1, keepdims=True)
        o = jnp.einsum('bcnhqk,bcnkhd->bcnhqd', p, vb) / l
        lse = (m + jnp.log(l))[..., 0]
        o = o.transpose(0, 1, 2, 4, 3, 5).reshape(Bsz, dil, n_pad, H, dh)[:, :, :n]
        lse = lse.transpose(0, 1, 2, 4, 3).reshape(Bsz, dil, n_pad, H)[:, :, :n]
        outs.append(o.transpose(0, 2, 1, 3, 4).reshape(Bsz, T, H, dh))
        lses.append(lse.transpose(0, 2, 1, 3).reshape(Bsz, T, H))
    w = jax.nn.softmax(jnp.stack(lses), axis=0)
    return jnp.einsum('pbth,pbthd->bthd', w, jnp.stack(outs))


def diff_attention(q, k, v, bias_table, lam):
    Bsz, T, H, _, dqk = q.shape
    scale = dqk ** -0.5
    nb = T // Q_BLOCK
    qb = q.reshape(Bsz, nb, Q_BLOCK, H, 2, dqk).transpose(1, 0, 2, 3, 4, 5)
    kpos = jnp.arange(T)

    def block(args):
        qblk, bi = args
        qpos = bi * Q_BLOCK + jnp.arange(Q_BLOCK)
        rel = qpos[:, None] - kpos[None, :]
        bias = bias_table[t5_bucket(rel)].transpose(2, 0, 1)
        logits = jnp.einsum('bqhmd,bkhmd->bhmqk', qblk, k) * scale + bias[:, None]
        logits = jnp.where(rel >= 0, logits, -jnp.inf)
        p = jax.nn.softmax(logits, axis=-1)
        attn = p[:, :, 0] - lam * p[:, :, 1]
        return jnp.einsum('bhqk,bkhd->bqhd', attn, v)

    o = lax.map(block, (qb, jnp.arange(nb)))
    return o.transpose(1, 0, 2, 3, 4).reshape(Bsz, T, H, -1)


def to_chunks(t):
    Bsz, T, H = t.shape[:3]
    t = t.reshape(Bsz, T // CHUNK, CHUNK, H, *t.shape[3:])
    return jnp.moveaxis(t, (1, 3), (0, 2))


def from_chunks(o):
    o = jnp.moveaxis(o, (0, 2), (1, 3))
    Bsz, nc, C, H, d = o.shape
    return o.reshape(Bsz, nc * C, H, d)


def short_conv(x, w):
    K, C = w.shape
    return lax.conv_general_dilated(x, w[:, None, :], window_strides=(1,), padding=[(K - 1, 0)],
                                    dimension_numbers=('NWC', 'WIO', 'NWC'), feature_group_count=C)


def gated_delta_net(q, k, v, beta, g):
    Bsz, T, H, dk = q.shape
    dv = v.shape[-1]
    qc, kc, vc = to_chunks(q), to_chunks(k), to_chunks(v)
    bc, gc = to_chunks(beta), to_chunks(g)
    G = jnp.cumsum(gc, axis=-1)
    tri = jnp.tril(jnp.ones((CHUNK, CHUNK), bool))
    strict = jnp.tril(jnp.ones((CHUNK, CHUNK), bool), -1)
    gamma = jnp.exp(jnp.where(tri, G[..., :, None] - G[..., None, :], -jnp.inf))
    kk = jnp.einsum('nbhid,nbhjd->nbhij', kc, kc)
    M = jnp.where(strict, bc[..., :, None] * kk * gamma, 0.0)
    eye = jnp.eye(CHUNK, dtype=M.dtype)
    rhs = jnp.concatenate([vc * bc[..., None], kc * (bc * jnp.exp(G))[..., None]], axis=-1)
    sol = lax.linalg.triangular_solve(M + eye, rhs, left_side=True, lower=True, unit_diagonal=True)
    U, W = sol[..., :dv], sol[..., dv:]
    Aqk = jnp.einsum('nbhid,nbhjd->nbhij', qc, kc) * gamma
    q_dec = qc * jnp.exp(G)[..., None]
    k_dec = kc * jnp.exp(G[..., -1:] - G)[..., None]
    g_last = jnp.exp(G[..., -1])

    def step(S, inp):
        u, w, a, qd, kd, gl = inp
        v_new = u - jnp.einsum('bhcd,bhde->bhce', w, S)
        o = jnp.einsum('bhcd,bhde->bhce', qd, S) + jnp.einsum('bhij,bhje->bhie', a, v_new)
        S = gl[..., None, None] * S + jnp.einsum('bhcd,bhce->bhde', kd, v_new)
        return S, o

    S0 = jnp.zeros((Bsz, H, dk, dv), jnp.float32)
    _, o = lax.scan(step, S0, (U, W, Aqk, q_dec, k_dec, g_last))
    return from_chunks(o)


def hgrn2(q, k, logf, v):
    Bsz, T, H, dk = q.shape
    dv = v.shape[-1]
    tri = jnp.tril(jnp.ones((CHUNK, CHUNK), bool))

    def step(S, inp):
        qc, kc, gc, vc = inp
        Bc = jnp.cumsum(gc, axis=2)
        o_inter = jnp.einsum('bhcd,bhde->bhce', qc * jnp.exp(Bc), S)
        diff = Bc[:, :, :, None, :] - Bc[:, :, None, :, :]
        dec = jnp.exp(jnp.where(tri[:, :, None], diff, -jnp.inf))
        A = jnp.einsum('bhid,bhjd,bhijd->bhij', qc, kc, dec)
        o = o_inter + jnp.einsum('bhij,bhje->bhie', A, vc)
        bl = Bc[:, :, -1]
        S = jnp.exp(bl)[..., None] * S + jnp.einsum('bhcd,bhce->bhde', kc * jnp.exp(bl[:, :, None] - Bc), vc)
        return S, o

    S0 = jnp.zeros((Bsz, H, dk, dv), jnp.float32)
    _, o = lax.scan(step, S0, (to_chunks(q), to_chunks(k), to_chunks(logf), to_chunks(v)))
    return from_chunks(o)


def setup_inputs(seed: int = 0) -> dict:
    key = jax.random.key(seed)
    ks = jax.random.split(key, 16)
    f32 = jnp.float32
    nrm = lambda k, s: jax.random.normal(k, s, f32)
    x = nrm(ks[0], (BATCH, SEQ, D_MODEL))
    norm_gain = 1.0 + 0.02 * nrm(ks[1], (DEPTH, D_MODEL))
    w_in = nrm(ks[2], (DEPTH, D_MODEL, N_IN_COLS)) * D_MODEL ** -0.5
    rel_bias = 0.1 * nrm(ks[3], (NUM_BUCKETS, A_HEADS + B_HEADS))
    diff_lambda = 0.1 * nrm(ks[4], (DEPTH, 4, DIFF_QK_DIM))
    diff_subln_gain = 1.0 + 0.02 * nrm(ks[5], (DEPTH, DIFF_V_DIM))
    dn_conv = nrm(ks[6], (DEPTH, CONV_K, C_HEADS * (2 * DN_HEAD_K + DN_HEAD_V))) * CONV_K ** -0.5
    dn_a_log = jnp.log(jax.random.uniform(ks[7], (DEPTH, C_HEADS), f32, 1.0, 16.0))
    dt = jnp.exp(jax.random.uniform(ks[8], (DEPTH, C_HEADS), f32, math.log(1e-3), math.log(1e-1)))
    dn_dt_bias = dt + jnp.log(-jnp.expm1(-dt))
    dn_norm_gain = 1.0 + 0.02 * nrm(ks[9], (DEPTH, DN_HEAD_V))
    hg_lb_logits = 0.5 * nrm(ks[10], (DEPTH, D_HEADS * HG_EXPAND))
    hg_norm_gain = 1.0 + 0.02 * nrm(ks[11], (DEPTH, HG_HEAD_V))
    w_branch = nrm(ks[12], (DEPTH, N_BRANCH, BRANCH_W, D_MODEL)) * BRANCH_W ** -0.5
    w_out = nrm(ks[13], (DEPTH, D_MODEL, D_MODEL)) * D_MODEL ** -0.5
    final_gain = 1.0 + 0.02 * nrm(ks[14], (D_MODEL,))
    return {"x": x, "norm_gain": norm_gain, "w_in": w_in, "rel_bias": rel_bias,
            "diff_lambda": diff_lambda, "diff_subln_gain": diff_subln_gain, "dn_conv": dn_conv,
            "dn_a_log": dn_a_log, "dn_dt_bias": dn_dt_bias, "dn_norm_gain": dn_norm_gain,
            "hg_lb_logits": hg_lb_logits, "hg_norm_gain": hg_norm_gain, "w_branch": w_branch,
            "w_out": w_out, "final_gain": final_gain}


def reference(x, norm_gain, w_in, rel_bias, diff_lambda, diff_subln_gain, dn_conv, dn_a_log,
              dn_dt_bias, dn_norm_gain, hg_lb_logits, hg_norm_gain, w_branch, w_out, final_gain):
    f32 = jnp.float32
    Bsz, T, _ = x.shape
    silu = jax.nn.silu
    heads = lambda t, n: t.reshape(Bsz, T, n, -1)
    lb_p = jax.nn.softmax(hg_lb_logits.astype(f32), axis=0)
    hg_lb = jnp.clip(jnp.cumsum(lb_p, axis=0) - lb_p[0], 0.0, 1.0)
    bias_a = rel_bias[:, :A_HEADS].astype(f32)
    bias_b = rel_bias[:, A_HEADS:].astype(f32)
    split_at = [int(s) for s in np.cumsum(IN_SPLITS)[:-1]]
    for layer in range(DEPTH):
        h = rms_norm(x, norm_gain[layer])
        (a_q, a_k, a_v, a_gate, b_q, b_k, b_v, b_gate, c_qkv, c_z, c_beta, c_a,
         d_q, d_f, d_i, d_gate, merge) = jnp.split(h @ w_in[layer].astype(f32), split_at, axis=-1)

        o_a = dilated_window_attention(heads(a_q, A_HEADS), heads(a_k, A_HEADS), heads(a_v, A_HEADS), bias_a)
        y_a = o_a.reshape(Bsz, T, -1) * silu(a_gate)

        lam_init = 0.8 - 0.6 * math.exp(-0.3 * layer)
        lq1, lk1, lq2, lk2 = diff_lambda[layer].astype(f32)
        lam = jnp.exp(jnp.sum(lq1 * lk1)) - jnp.exp(jnp.sum(lq2 * lk2)) + lam_init
        o_b = diff_attention(b_q.reshape(Bsz, T, B_HEADS, 2, DIFF_QK_DIM),
                             b_k.reshape(Bsz, T, B_HEADS, 2, DIFF_QK_DIM),
                             heads(b_v, B_HEADS), bias_b, lam)
        o_b = rms_norm(o_b, diff_subln_gain[layer]) * (1.0 - lam_init)
        y_b = o_b.reshape(Bsz, T, -1) * silu(b_gate)

        c_qkv = silu(short_conv(c_qkv, dn_conv[layer].astype(f32)))
        c_q, c_k, c_v = jnp.split(c_qkv, [C_HEADS * DN_HEAD_K, 2 * C_HEADS * DN_HEAD_K], axis=-1)
        c_q = l2norm(heads(c_q, C_HEADS)) * DN_HEAD_K ** -0.5
        c_k = l2norm(heads(c_k, C_HEADS))
        beta = jax.nn.sigmoid(c_beta)
        g = -jnp.exp(dn_a_log[layer].astype(f32)) * jax.nn.softplus(c_a + dn_dt_bias[layer].astype(f32))
        o_c = gated_delta_net(c_q, c_k, heads(c_v, C_HEADS), beta, g)
        y_c = rms_norm(o_c, dn_norm_gain[layer]).reshape(Bsz, T, -1) * silu(c_z)

        lb = hg_lb[layer]
        logf = jnp.logaddexp(jnp.log(lb), jnp.log1p(-lb) + jax.nn.log_sigmoid(d_f))
        d_k = (1.0 - lb) * jax.nn.sigmoid(-d_f)
        o_d = hgrn2(heads(d_q, D_HEADS), heads(d_k, D_HEADS), heads(logf, D_HEADS), heads(d_i, D_HEADS))
        y_d = rms_norm(o_d, hg_norm_gain[layer]).reshape(Bsz, T, -1) * silu(d_gate)

        gates = jax.nn.sigmoid(merge.reshape(Bsz, T, N_BRANCH, D_MODEL))
        w_br = w_branch[layer].astype(f32)
        mixed = gates[:, :, 0] * (y_a @ w_br[0])
        mixed = mixed + gates[:, :, 1] * (y_b @ w_br[1])
        mixed = mixed + gates[:, :, 2] * (y_c @ w_br[2])
        mixed = mixed + gates[:, :, 3] * (y_d @ w_br[3])
        x = x + (mixed @ w_out[layer].astype(f32)).astype(x.dtype)
    return rms_norm(x, final_gain).astype(x.dtype)
```

```python
import functools
import math

import jax
import jax.numpy as jnp
import numpy as np
from jax import lax
from jax.experimental import pallas as pl
from jax.experimental.pallas import tpu as pltpu

F32 = jnp.float32
BF16 = jnp.bfloat16
HI = lax.Precision.HIGHEST

D_MODEL = 2048
BRANCH_W = 512
HEAD_W = 128
A_HEAD_DIM = 64
DILATIONS = (1, 4, 16)
BAND = 128
A_TILE = BAND * DILATIONS[-1]
DIFF_QK = 64
B_TQ = 256
B_NEAR = 2048 // B_TQ + 1
CHUNK = 64
SUB = 16
CONV_K = 4
NUM_BUCKETS = 32
MAX_DISTANCE = 2048
RMS_EPS = 1e-6
NEG = -1e30
VMEM_LIMIT = 56 * 1024 * 1024

PF_CQKV, PF_CZ, PF_AGATE, PF_BGATE, PF_DQ, PF_DF, PF_DI, PF_DGATE, PF_AQ, PF_AK, PF_AV = 0, 3, 4, 5, 6, 7, 8, 9, 10, 11, 12
PF_COLS = 13 * 512


def _cparams(sem):
    return pltpu.CompilerParams(dimension_semantics=sem, vmem_limit_bytes=VMEM_LIMIT)


def _nt(a, b, precision=None):
    return lax.dot_general(a, b, (((1,), (1,)), ((), ())), precision=precision,
                           preferred_element_type=F32)


def _tn(a, b, precision=None):
    return lax.dot_general(a, b, (((0,), (0,)), ((), ())), precision=precision,
                           preferred_element_type=F32)


def _nn(a, b, precision=None):
    return jnp.dot(a, b, precision=precision, preferred_element_type=F32)


def _sigmoid(x):
    return 1.0 / (1.0 + jnp.exp(-x))


def _silu(x):
    return x * _sigmoid(x)


def _log1p_exp_neg_abs(x):
    return jnp.log1p(jnp.exp(-jnp.abs(x)))


def _rmsnorm_kernel(x_ref, g_ref, o_ref):
    x = x_ref[...]
    ms = jnp.mean(x * x, axis=-1, keepdims=True)
    o_ref[...] = (x * lax.rsqrt(ms + RMS_EPS) * g_ref[...]).astype(o_ref.dtype)


def _rmsnorm(x, gain, out_dtype, tm=512):
    n, d = x.shape
    return pl.pallas_call(
        _rmsnorm_kernel,
        grid=(n // tm,),
        in_specs=[pl.BlockSpec((tm, d), lambda i: (i, 0)),
                  pl.BlockSpec((1, d), lambda i: (0, 0))],
        out_specs=pl.BlockSpec((tm, d), lambda i: (i, 0)),
        out_shape=jax.ShapeDtypeStruct((n, d), out_dtype),
        compiler_params=_cparams(("parallel",)),
        name="rmsnorm",
    )(x, gain.reshape(1, d).astype(F32))


def _mm_kernel(a_ref, w_ref, o_ref, *, act):
    acc = _nn(a_ref[...], w_ref[...])
    if act == "sigmoid":
        acc = _sigmoid(acc)
    o_ref[...] = acc.astype(o_ref.dtype)


def _mm_res_kernel(a_ref, w_ref, r_ref, o_ref):
    o_ref[...] = r_ref[...] + _nn(a_ref[...], w_ref[...])


def _mm(a, w, out_dtype, act=None, residual=None, tm=512, tn=1024, name="mm"):
    n, k = a.shape
    c = w.shape[1]
    tn = min(tn, c)
    in_specs = [pl.BlockSpec((tm, k), lambda i, j: (i, 0)),
                pl.BlockSpec((k, tn), lambda i, j: (0, j))]
    args = [a, w]
    if residual is None:
        body = functools.partial(_mm_kernel, act=act)
    else:
        body = _mm_res_kernel
        in_specs.append(pl.BlockSpec((tm, tn), lambda i, j: (i, j)))
        args.append(residual)
    return pl.pallas_call(
        body,
        grid=(n // tm, c // tn),
        in_specs=in_specs,
        out_specs=pl.BlockSpec((tm, tn), lambda i, j: (i, j)),
        out_shape=jax.ShapeDtypeStruct((n, c), out_dtype),
        compiler_params=_cparams(("parallel", "parallel")),
        name=name,
    )(*args)


def _merge_kernel(g0, g1, g2, g3, y0, y1, y2, y3, w_ref, o_ref):
    acc = g0[...].astype(F32) * _nn(y0[...], w_ref[0])
    acc += g1[...].astype(F32) * _nn(y1[...], w_ref[1])
    acc += g2[...].astype(F32) * _nn(y2[...], w_ref[2])
    acc += g3[...].astype(F32) * _nn(y3[...], w_ref[3])
    o_ref[...] = acc.astype(o_ref.dtype)


def _merge(gates, ys, w_br, tm=512, tn=512):
    n = gates.shape[0]
    nj = D_MODEL // tn
    gate_specs = [pl.BlockSpec((tm, tn), functools.partial(lambda i, j, b: (i, b * nj + j), b=b))
                  for b in range(4)]
    y_specs = [pl.BlockSpec((tm, BRANCH_W), lambda i, j: (i, 0)) for _ in range(4)]
    return pl.pallas_call(
        _merge_kernel,
        grid=(n // tm, nj),
        in_specs=gate_specs + y_specs + [pl.BlockSpec((4, BRANCH_W, tn), lambda i, j: (0, 0, j))],
        out_specs=pl.BlockSpec((tm, tn), lambda i, j: (i, j)),
        out_shape=jax.ShapeDtypeStruct((n, D_MODEL), BF16),
        compiler_params=_cparams(("parallel", "parallel")),
        name="merge",
    )(gates, gates, gates, gates, *ys, w_br)


def _t5_bucket(dist):
    n = jnp.maximum(dist, 0)
    max_exact = NUM_BUCKETS // 2
    nf = jnp.maximum(n, max_exact).astype(F32)
    large = max_exact + (jnp.log(nf / max_exact) / math.log(MAX_DISTANCE / max_exact)
                         * (NUM_BUCKETS - max_exact)).astype(jnp.int32)
    large = jnp.minimum(large, NUM_BUCKETS - 1)
    return jnp.where(n < max_exact, n, large)


def _dilated_bias(bias_a):
    qi = jnp.arange(BAND)[:, None]
    kj = jnp.arange(2 * BAND)[None, :]
    rel = qi + BAND - kj
    valid = (rel >= 0) & (rel <= BAND)
    tabs = []
    for dil in DILATIONS:
        b = bias_a[_t5_bucket(rel * dil)].transpose(2, 0, 1)
        tabs.append(jnp.where(valid[None], b, NEG))
    return jnp.stack(tabs)


def _diff_bias(bias_b):
    qi = jnp.arange(B_TQ)[:, None]
    kj = jnp.arange(B_TQ)[None, :]
    tabs = []
    for d in range(B_NEAR):
        rel = d * B_TQ + qi - kj
        b = bias_b[_t5_bucket(rel)].transpose(2, 0, 1)
        tabs.append(jnp.where((rel >= 0)[None], b, NEG))
    return jnp.stack(tabs, axis=1)


def _dil_attn_kernel(q_ref, kp_ref, kc_ref, vp_ref, vc_ref, gate_ref, bias_ref, o_ref,
                     kcat, vcat, acc_ref, m_ref, l_ref):
    n = pl.program_id(1)
    kcat[0:A_TILE, :] = kp_ref[0]
    kcat[A_TILE:2 * A_TILE, :] = kc_ref[0]
    vcat[0:A_TILE, :] = vp_ref[0]
    vcat[A_TILE:2 * A_TILE, :] = vc_ref[0]
    lane = lax.broadcasted_iota(jnp.int32, (BAND, HEAD_W), 1)
    lo = lane < A_HEAD_DIM
    col = lax.broadcasted_iota(jnp.int32, (BAND, 2 * BAND), 1)
    nblk_tile = A_TILE // BAND

    for p, r in enumerate(DILATIONS):
        def rows(start, r=r):
            return pl.ds(start, BAND, stride=r) if r > 1 else pl.ds(pl.multiple_of(start, BAND), BAND)

        def body(idx, carry, p=p, r=r, rows=rows):
            j = idx // r
            start = j * (BAND * r) + idx % r
            q = q_ref[0, rows(start), :]
            q = (q * (A_HEAD_DIM ** -0.5)).astype(BF16)
            kk = jnp.concatenate([kcat[rows(A_TILE + start - BAND * r), :],
                                  kcat[rows(A_TILE + start), :]], axis=0).astype(BF16)
            vv = jnp.concatenate([vcat[rows(A_TILE + start - BAND * r), :],
                                  vcat[rows(A_TILE + start), :]], axis=0).astype(BF16)
            no_prev = jnp.logical_and(n == 0, j == 0)
            outs, ms, ls = [], [], []
            for hh in range(2):
                qz = jnp.where(lo if hh == 0 else jnp.logical_not(lo), q, jnp.zeros_like(q))
                s = _nt(qz, kk) + bias_ref[p, hh]
                s = jnp.where(jnp.logical_and(no_prev, col < BAND), NEG, s)
                m = jnp.max(s, axis=-1, keepdims=True)
                e = jnp.exp(s - m)
                l = jnp.sum(e, axis=-1, keepdims=True)
                outs.append(_nn(e.astype(BF16), vv))
                ms.append(jnp.broadcast_to(m, (BAND, HEAD_W)))
                ls.append(jnp.broadcast_to(l, (BAND, HEAD_W)))
            o_new = jnp.where(lo, outs[0], outs[1])
            m_new = jnp.where(lo, ms[0], ms[1])
            l_new = jnp.where(lo, ls[0], ls[1])
            if p == 0:
                acc_ref[rows(start), :] = o_new
                m_ref[rows(start), :] = m_new
                l_ref[rows(start), :] = l_new
            else:
                m_old = m_ref[rows(start), :]
                m_tot = jnp.maximum(m_old, m_new)
                a = jnp.exp(m_old - m_tot)
                b = jnp.exp(m_new - m_tot)
                acc_ref[rows(start), :] = a * acc_ref[rows(start), :] + b * o_new
                l_ref[rows(start), :] = a * l_ref[rows(start), :] + b * l_new
                m_ref[rows(start), :] = m_tot
            return carry

        lax.fori_loop(0, nblk_tile, body, 0)

    o_ref[0] = (acc_ref[...] / l_ref[...] * _silu(gate_ref[0])).astype(o_ref.dtype)


def _dil_attn(pf, bias_tab, bsz, t):
    nt = t // A_TILE
    hp = BRANCH_W // HEAD_W
    blk = (1, A_TILE, HEAD_W)

    def cur(cb):
        return pl.BlockSpec(blk, lambda b, n, h: (b, n, cb * hp + h))

    def prev(cb):
        return pl.BlockSpec(blk, lambda b, n, h: (b, jnp.maximum(n - 1, 0), cb * hp + h))

    return pl.pallas_call(
        _dil_attn_kernel,
        grid=(bsz, nt, hp),
        in_specs=[cur(PF_AQ), prev(PF_AK), cur(PF_AK), prev(PF_AV), cur(PF_AV), cur(PF_AGATE),
                  pl.BlockSpec((3, 2, BAND, 2 * BAND), lambda b, n, h: (0, h, 0, 0))],
        out_specs=pl.BlockSpec(blk, lambda b, n, h: (b, n, h)),
        out_shape=jax.ShapeDtypeStruct((bsz, t, BRANCH_W), BF16),
        scratch_shapes=[pltpu.VMEM((2 * A_TILE, HEAD_W), F32), pltpu.VMEM((2 * A_TILE, HEAD_W), F32),
                        pltpu.VMEM((A_TILE, HEAD_W), F32), pltpu.VMEM((A_TILE, HEAD_W), F32),
                        pltpu.VMEM((A_TILE, HEAD_W), F32)],
        compiler_params=_cparams(("parallel", "parallel", "parallel")),
        name="dilated_attn",
    )(pf, pf, pf, pf, pf, pf, bias_tab)


def _diff_attn_kernel(scal_ref, q_ref, k_ref, v_ref, gate_ref, bias_ref, gain_ref, o_ref,
                      kt_ref, acc1, acc2, m1, m2, l1, l2):
    h = pl.program_id(1)
    qi = pl.program_id(2)
    nkb = kt_ref.shape[0]

    @pl.when(qi == 0)
    def _():
        def tr(kb, c):
            blk = k_ref[0, pl.ds(pl.multiple_of(kb * B_TQ, B_TQ), B_TQ), :]
            kt_ref[kb] = blk.astype(F32).T.astype(BF16)
            return c
        lax.fori_loop(0, nkb, tr, 0)

    lane = lax.broadcasted_iota(jnp.int32, (B_TQ, HEAD_W), 1)
    lo = lane < DIFF_QK
    q = q_ref[0] * (DIFF_QK ** -0.5)
    zero = jnp.zeros_like(q)
    qz = (jnp.where(lo, q, zero), jnp.where(lo, zero, q))
    maps = ((acc1, m1, l1), (acc2, m2, l2))
    for acc, m, l in maps:
        acc[...] = jnp.zeros_like(acc)
        m[...] = jnp.full_like(m, NEG)
        l[...] = jnp.zeros_like(l)
    cfar = scal_ref[2 + h]

    def step(kb, near):
        kt = kt_ref[kb]
        vb = v_ref[0, pl.ds(pl.multiple_of(kb * B_TQ, B_TQ), B_TQ), :]
        for mi, (acc, m, l) in enumerate(maps):
            s = _nn(qz[mi], kt)
            if near:
                s = s + bias_ref[0, qi - kb]
                shift = 0.0
            else:
                shift = cfar
            m_prev = m[...]
            m_cur = jnp.max(s, axis=-1, keepdims=True) + shift
            m_next = jnp.maximum(m_prev, m_cur)
            alpha = jnp.exp(m_prev - m_next)
            e = jnp.exp(s - (m_next[:, :1] - shift))
            l[...] = alpha * l[...] + jnp.sum(e, axis=-1, keepdims=True)
            acc[...] = alpha * acc[...] + _nn(e.astype(BF16), vb)
            m[...] = m_next

    n_far = jnp.maximum(qi - (B_NEAR - 1), 0)

    def far_body(kb, c):
        step(kb, False)
        return c

    def near_body(kb, c):
        step(kb, True)
        return c

    lax.fori_loop(0, n_far, far_body, 0)
    lax.fori_loop(n_far, qi + 1, near_body, 0)

    lam = scal_ref[0]
    o = acc1[...] / l1[...] - lam * (acc2[...] / l2[...])
    ms = jnp.mean(o * o, axis=-1, keepdims=True)
    o = o * lax.rsqrt(ms + RMS_EPS) * gain_ref[...] * scal_ref[1]
    o_ref[0] = (o * _silu(gate_ref[0])).astype(o_ref.dtype)


def _diff_attn(pb, pf, bias_tab, scal, gain, bsz, t):
    nh = BRANCH_W // HEAD_W
    nq = t // B_TQ
    return pl.pallas_call(
        _diff_attn_kernel,
        grid=(bsz, nh, nq),
        in_specs=[pl.BlockSpec(memory_space=pltpu.SMEM),
                  pl.BlockSpec((1, B_TQ, HEAD_W), lambda b, h, i: (b, i, h)),
                  pl.BlockSpec((1, t, HEAD_W), lambda b, h, i: (b, 0, nh + h)),
                  pl.BlockSpec((1, t, HEAD_W), lambda b, h, i: (b, 0, 2 * nh + h)),
                  pl.BlockSpec((1, B_TQ, HEAD_W), lambda b, h, i: (b, i, PF_BGATE * nh + h)),
                  pl.BlockSpec((1, B_NEAR, B_TQ, B_TQ), lambda b, h, i: (h, 0, 0, 0)),
                  pl.BlockSpec((1, HEAD_W), lambda b, h, i: (0, 0))],
        out_specs=pl.BlockSpec((1, B_TQ, HEAD_W), lambda b, h, i: (b, i, h)),
        out_shape=jax.ShapeDtypeStruct((bsz, t, BRANCH_W), BF16),
        scratch_shapes=[pltpu.VMEM((nq, HEAD_W, B_TQ), BF16)]
        + [pltpu.VMEM((B_TQ, HEAD_W), F32) for _ in range(6)],
        compiler_params=_cparams(("arbitrary", "arbitrary", "arbitrary")),
        name="diff_attn",
    )(scal, pb, pb, pb, pf, bias_tab, gain.reshape(1, HEAD_W).astype(F32))


def _delta_kernel(qkv_ref, z_ref, small_ref, conv_ref, par_ref, gain_ref, o_ref,
                  xe_ref, s_ref):
    tc = qkv_ref.shape[1]
    nchunk = tc // CHUNK

    @pl.when(pl.program_id(1) == 0)
    def _():
        xe_ref[0:8, :] = jnp.zeros((8, xe_ref.shape[1]), F32)
        s_ref[...] = jnp.zeros_like(s_ref)

    xe_ref[8:8 + tc, :] = qkv_ref[0]
    conv = conv_ref[0:1, :] * xe_ref[pl.ds(8 - 3, tc), :]
    for kk in range(1, CONV_K):
        conv = conv + conv_ref[kk:kk + 1, :] * xe_ref[pl.ds(8 - 3 + kk, tc), :]
    xe_ref[0:8, :] = xe_ref[tc:tc + 8, :]
    c = _silu(conv)

    small = small_ref[0]
    beta_all = _sigmoid(small)
    xa = small + par_ref[1:2, :]
    softplus = jnp.maximum(xa, 0.0) + _log1p_exp_neg_abs(xa)
    g_all = -jnp.exp(par_ref[0:1, :]) * softplus

    ri = lax.broadcasted_iota(jnp.int32, (CHUNK, CHUNK), 0)
    ci = lax.broadcasted_iota(jnp.int32, (CHUNK, CHUNK), 1)
    tri = ri >= ci
    strict = ri > ci
    tri_f = tri.astype(F32)
    eye = (ri == ci).astype(F32)
    sel = (lax.broadcasted_iota(jnp.int32, (8, HEAD_W), 1)
           == lax.broadcasted_iota(jnp.int32, (8, HEAD_W), 0) + 4).astype(F32)
    nh = BRANCH_W // HEAD_W

    for ch in range(nchunk):
        r0 = ch * CHUNK
        g_c = g_all[r0:r0 + CHUNK]
        gcum = _nn(tri_f, g_c, HI)
        grow_all = _nt(sel, gcum, HI)
        for h in range(nh):
            q = c[r0:r0 + CHUNK, h * HEAD_W:(h + 1) * HEAD_W]
            k = c[r0:r0 + CHUNK, BRANCH_W + h * HEAD_W:BRANCH_W + (h + 1) * HEAD_W]
            v = c[r0:r0 + CHUNK, 2 * BRANCH_W + h * HEAD_W:2 * BRANCH_W + (h + 1) * HEAD_W]
            q = q * lax.rsqrt(jnp.sum(q * q, axis=-1, keepdims=True) + 1e-6) * (HEAD_W ** -0.5)
            k = k * lax.rsqrt(jnp.sum(k * k, axis=-1, keepdims=True) + 1e-6)
            bcol = beta_all[r0:r0 + CHUNK, h:h + 1]
            gcol = gcum[:, 4 + h:5 + h]
            grow = grow_all[h:h + 1, :]
            glast = gcum[CHUNK - 1:CHUNK, 4 + h:5 + h]
            gamma = jnp.exp(jnp.where(tri, gcol - grow, NEG))
            kkt = _nt(k, k, HI)
            x = jnp.where(strict, -(bcol * kkt * gamma), 0.0)
            tinv = eye + x
            pw = x
            for _ in range(5):
                pw = _nn(pw, pw, HI)
                tinv = tinv + _nn(tinv, pw, HI)
            egc = jnp.exp(gcol)
            u = _nn(tinv, v * bcol, HI)
            w = _nn(tinv, k * (bcol * egc), HI)
            aqk = _nt(q, k, HI) * gamma
            qd = q * egc
            kd = k * jnp.exp(glast - gcol)
            s = s_ref[h]
            v_new = u - _nn(w, s, HI)
            o = _nn(qd, s, HI) + _nn(aqk, v_new, HI)
            s_ref[h] = jnp.exp(glast) * s + _tn(kd, v_new, HI)
            ms = jnp.mean(o * o, axis=-1, keepdims=True)
            o = o * lax.rsqrt(ms + RMS_EPS) * gain_ref[...]
            zg = z_ref[0, r0:r0 + CHUNK, h * HEAD_W:(h + 1) * HEAD_W]
            o_ref[0, r0:r0 + CHUNK, h * HEAD_W:(h + 1) * HEAD_W] = (o * _silu(zg)).astype(o_ref.dtype)


def _delta_net(pf, small, conv_w, par, gain, bsz, t, tc=CHUNK):
    cw = 3 * BRANCH_W
    return pl.pallas_call(
        _delta_kernel,
        grid=(bsz, t // tc),
        in_specs=[pl.BlockSpec((1, tc, cw), lambda b, i: (b, i, 0)),
                  pl.BlockSpec((1, tc, BRANCH_W), lambda b, i: (b, i, PF_CZ)),
                  pl.BlockSpec((1, tc, HEAD_W), lambda b, i: (b, i, 0)),
                  pl.BlockSpec((8, cw), lambda b, i: (0, 0)),
                  pl.BlockSpec((8, HEAD_W), lambda b, i: (0, 0)),
                  pl.BlockSpec((1, HEAD_W), lambda b, i: (0, 0))],
        out_specs=pl.BlockSpec((1, tc, BRANCH_W), lambda b, i: (b, i, 0)),
        out_shape=jax.ShapeDtypeStruct((bsz, t, BRANCH_W), BF16),
        scratch_shapes=[pltpu.VMEM((tc + 8, cw), F32),
                        pltpu.VMEM((BRANCH_W // HEAD_W, HEAD_W, HEAD_W), F32)],
        compiler_params=_cparams(("arbitrary", "arbitrary")),
        name="delta_net",
    )(pf, pf, small, conv_w, par, gain.reshape(1, HEAD_W).astype(F32))


def _hgrn_kernel(q_ref, f_ref, i_ref, gate_ref, lb_ref, gain_ref, o_ref, st_ref):
    tc = q_ref.shape[1]
    nchunk = tc // CHUNK
    nh = BRANCH_W // HEAD_W
    nsub = CHUNK // SUB

    @pl.when(pl.program_id(1) == 0)
    def _():
        st_ref[...] = jnp.zeros_like(st_ref)

    lb = lb_ref[...]
    df = f_ref[0]
    log_sig = jnp.minimum(df, 0.0) - _log1p_exp_neg_abs(df)
    a = jnp.log(lb)
    b = jnp.log1p(-lb) + log_sig
    logf_all = jnp.maximum(a, b) + _log1p_exp_neg_abs(a - b)
    k_all = (1.0 - lb) * _sigmoid(-df)

    ri = lax.broadcasted_iota(jnp.int32, (CHUNK, CHUNK), 0)
    ci = lax.broadcasted_iota(jnp.int32, (CHUNK, CHUNK), 1)
    tri_f = (ri >= ci).astype(F32)
    sub_row = lax.broadcasted_iota(jnp.int32, (SUB, HEAD_W), 0)

    for ch in range(nchunk):
        r0 = ch * CHUNK
        for h in range(nh):
            cs = slice(h * HEAD_W, (h + 1) * HEAD_W)
            q = q_ref[0, r0:r0 + CHUNK, cs]
            v = i_ref[0, r0:r0 + CHUNK, cs]
            k = k_all[r0:r0 + CHUNK, cs]
            lf = logf_all[r0:r0 + CHUNK, cs]
            bc = _nn(tri_f, lf, HI)
            bl = bc[CHUNK - 1:CHUNK, :]
            st = st_ref[h]
            o_inter = _nt(q * jnp.exp(bc), st, HI)
            pieces = []
            for si in range(nsub):
                rs = slice(si * SUB, (si + 1) * SUB)
                q_s, bc_s = q[rs], bc[rs]
                o_s = o_inter[rs]
                if si > 0:
                    ref_row = bc[si * SUB - 1:si * SUB, :]
                    q_dec = q_s * jnp.exp(bc_s - ref_row)
                    k_dec = k[:si * SUB] * jnp.exp(ref_row - bc[:si * SUB])
                    o_s = o_s + _nn(_nt(q_dec, k_dec, HI), v[:si * SUB], HI)
                for j in range(SUB):
                    jj = si * SUB + j
                    e = jnp.exp(jnp.where(sub_row >= j, bc_s - bc[jj:jj + 1, :], NEG))
                    a_col = jnp.sum(q_s * k[jj:jj + 1, :] * e, axis=-1, keepdims=True)
                    o_s = o_s + a_col * v[jj:jj + 1, :]
                pieces.append(o_s)
            o = jnp.concatenate(pieces, axis=0)
            st_ref[h] = st * jnp.exp(bl) + _tn(v, k * jnp.exp(bl - bc), HI)
            ms = jnp.mean(o * o, axis=-1, keepdims=True)
            o = o * lax.rsqrt(ms + RMS_EPS) * gain_ref[...]
            o_ref[0, r0:r0 + CHUNK, cs] = (o * _silu(gate_ref[0, r0:r0 + CHUNK, cs])).astype(o_ref.dtype)


def _hgrn(pf, lb, gain, bsz, t, tc=CHUNK):
    def spec(cb):
        return pl.BlockSpec((1, tc, BRANCH_W), lambda b, i: (b, i, cb))

    return pl.pallas_call(
        _hgrn_kernel,
        grid=(bsz, t // tc),
        in_specs=[spec(PF_DQ), spec(PF_DF), spec(PF_DI), spec(PF_DGATE),
                  pl.BlockSpec((1, BRANCH_W), lambda b, i: (0, 0)),
                  pl.BlockSpec((1, HEAD_W), lambda b, i: (0, 0))],
        out_specs=pl.BlockSpec((1, tc, BRANCH_W), lambda b, i: (b, i, 0)),
        out_shape=jax.ShapeDtypeStruct((bsz, t, BRANCH_W), BF16),
        scratch_shapes=[pltpu.VMEM((BRANCH_W // HEAD_W, HEAD_W, HEAD_W), F32)],
        compiler_params=_cparams(("arbitrary", "arbitrary")),
        name="hgrn2",
    )(pf, pf, pf, pf, lb.reshape(1, BRANCH_W), gain.reshape(1, HEAD_W).astype(F32))


def _split_w_in(w):
    o = 0
    parts = {}
    for name, width in (("a_q", 512), ("a_k", 512), ("a_v", 512), ("a_gate", 512),
                        ("b_q", 512), ("b_k", 512), ("b_v", 512), ("b_gate", 512),
                        ("c_qkv", 1536), ("c_z", 512), ("c_beta", 4), ("c_a", 4),
                        ("d_q", 512), ("d_f", 512), ("d_i", 512), ("d_gate", 512),
                        ("merge", 4 * D_MODEL)):
        parts[name] = w[:, o:o + width]
        o += width
    w_f = jnp.concatenate([parts[k] for k in ("c_qkv", "c_z", "a_gate", "b_gate", "d_q", "d_f",
                                              "d_i", "d_gate", "a_q", "a_k", "a_v")], axis=1)
    w_b = jnp.concatenate([parts[k] for k in ("b_q", "b_k", "b_v")], axis=1)
    w_s = jnp.concatenate([parts["c_beta"], parts["c_a"],
                           jnp.zeros((w.shape[0], HEAD_W - 8), w.dtype)], axis=1)
    return w_f.astype(BF16), w_b.astype(BF16), w_s.astype(BF16), parts["merge"].astype(BF16)


def kernel(x, norm_gain, w_in, rel_bias, diff_lambda, diff_subln_gain, dn_conv, dn_a_log, dn_dt_bias,
           dn_norm_gain, hg_lb_logits, hg_norm_gain, w_branch, w_out, final_gain):
    bsz, t, d = x.shape
    n = bsz * t
    depth = w_in.shape[0]
    lb_p = jax.nn.softmax(hg_lb_logits.astype(F32), axis=0)
    hg_lb = jnp.clip(jnp.cumsum(lb_p, axis=0) - lb_p[0], 0.0, 1.0)
    bias_a = _dilated_bias(rel_bias[:, :8].astype(F32))
    bias_b = _diff_bias(rel_bias[:, 8:].astype(F32))
    cfar = rel_bias[NUM_BUCKETS - 1, 8:].astype(F32)

    xf = x.reshape(n, d).astype(F32)
    for layer in range(depth):
        w_f, w_b, w_s, w_g = _split_w_in(w_in[layer])
        h = _rmsnorm(xf, norm_gain[layer], BF16)
        pf = _mm(h, w_f, F32, tn=512, name="proj_f32").reshape(bsz, t, PF_COLS)
        pb = _mm(h, w_b, BF16, tn=512, name="proj_bf16").reshape(bsz, t, 3 * BRANCH_W)
        ps = _mm(h, w_s, F32, name="proj_small").reshape(bsz, t, HEAD_W)
        gates = _mm(h, w_g, BF16, act="sigmoid", name="proj_gates")

        y_a = _dil_attn(pf, bias_a, bsz, t)

        lam_init = 0.8 - 0.6 * math.exp(-0.3 * layer)
        lq1, lk1, lq2, lk2 = diff_lambda[layer].astype(F32)
        lam = jnp.exp(jnp.sum(lq1 * lk1)) - jnp.exp(jnp.sum(lq2 * lk2)) + lam_init
        scal = jnp.concatenate([jnp.stack([lam, jnp.asarray(1.0 - lam_init, F32)]), cfar,
                                jnp.zeros((2,), F32)])
        y_b = _diff_attn(pb, pf, bias_b, scal, diff_subln_gain[layer], bsz, t)

        conv_w = jnp.concatenate([dn_conv[layer].astype(F32),
                                  jnp.zeros((8 - CONV_K, 3 * BRANCH_W), F32)], axis=0)
        par = jnp.zeros((8, HEAD_W), F32)
        par = par.at[0, 4:8].set(dn_a_log[layer].astype(F32)).at[1, 4:8].set(dn_dt_bias[layer].astype(F32))
        y_c = _delta_net(pf, ps, conv_w, par, dn_norm_gain[layer], bsz, t)

        y_d = _hgrn(pf, hg_lb[layer], hg_norm_gain[layer], bsz, t)

        ys = [y.reshape(n, BRANCH_W) for y in (y_a, y_b, y_c, y_d)]
        mixed = _merge(gates, ys, w_branch[layer].astype(BF16))
        xf = _mm(mixed, w_out[layer].astype(BF16), F32, residual=xf, name="out_proj")
    out = _rmsnorm(xf, final_gain, F32)
    return out.reshape(bsz, t, d).astype(x.dtype)
```

```python
import functools
import math

import jax
import jax.numpy as jnp
import numpy as np
from jax import lax
from jax.experimental import pallas as pl
from jax.experimental.pallas import tpu as pltpu

F32 = jnp.float32
BF16 = jnp.bfloat16
HI = lax.Precision.HIGHEST

D_MODEL = 2048
BRANCH_W = 512
HEAD_W = 128
A_HEAD_DIM = 64
DILATIONS = (1, 4, 16)
BAND = 128
A_TILE = BAND * DILATIONS[-1]
DIFF_QK = 64
B_TQ = 256
B_TK = 512
B_NEAR = 2048 // B_TQ + 1
CHUNK = 64
SUB = 16
CONV_K = 4
NUM_BUCKETS = 32
MAX_DISTANCE = 2048
RMS_EPS = 1e-6
NEG = -1e30
VMEM_LIMIT = 56 * 1024 * 1024

PF_CQKV, PF_CZ, PF_AGATE, PF_BGATE, PF_DQ, PF_DF, PF_DI, PF_DGATE, PF_AQ, PF_AK, PF_AV = 0, 3, 4, 5, 6, 7, 8, 9, 10, 11, 12
PF_COLS = 13 * 512


def _cparams(sem):
    return pltpu.CompilerParams(dimension_semantics=sem, vmem_limit_bytes=VMEM_LIMIT)


def _nt(a, b, precision=None):
    return lax.dot_general(a, b, (((1,), (1,)), ((), ())), precision=precision,
                           preferred_element_type=F32)


def _tn(a, b, precision=None):
    return lax.dot_general(a, b, (((0,), (0,)), ((), ())), precision=precision,
                           preferred_element_type=F32)


def _nn(a, b, precision=None):
    return jnp.dot(a, b, precision=precision, preferred_element_type=F32)


def _sigmoid(x):
    return 1.0 / (1.0 + jnp.exp(-x))


def _silu(x):
    return x * _sigmoid(x)


def _log1p_exp_neg_abs(x):
    return jnp.log1p(jnp.exp(-jnp.abs(x)))


def _rmsnorm_kernel(x_ref, g_ref, o_ref):
    x = x_ref[...]
    ms = jnp.mean(x * x, axis=-1, keepdims=True)
    o_ref[...] = (x * lax.rsqrt(ms + RMS_EPS) * g_ref[...]).astype(o_ref.dtype)


def _rmsnorm(x, gain, out_dtype, tm=512):
    n, d = x.shape
    return pl.pallas_call(
        _rmsnorm_kernel,
        grid=(n // tm,),
        in_specs=[pl.BlockSpec((tm, d), lambda i: (i, 0)),
                  pl.BlockSpec((1, d), lambda i: (0, 0))],
        out_specs=pl.BlockSpec((tm, d), lambda i: (i, 0)),
        out_shape=jax.ShapeDtypeStruct((n, d), out_dtype),
        compiler_params=_cparams(("parallel",)),
        name="rmsnorm",
    )(x, gain.reshape(1, d).astype(F32))


def _mm_kernel(a_ref, w_ref, o_ref, *, act):
    acc = _nn(a_ref[...], w_ref[...])
    if act == "sigmoid":
        acc = _sigmoid(acc)
    o_ref[...] = acc.astype(o_ref.dtype)


def _mm_res_kernel(a_ref, w_ref, r_ref, o_ref):
    o_ref[...] = r_ref[...] + _nn(a_ref[...], w_ref[...])


def _mm(a, w, out_dtype, act=None, residual=None, tm=512, tn=1024, name="mm"):
    n, k = a.shape
    c = w.shape[1]
    tn = min(tn, c)
    in_specs = [pl.BlockSpec((tm, k), lambda i, j: (i, 0)),
                pl.BlockSpec((k, tn), lambda i, j: (0, j))]
    args = [a, w]
    if residual is None:
        body = functools.partial(_mm_kernel, act=act)
    else:
        body = _mm_res_kernel
        in_specs.append(pl.BlockSpec((tm, tn), lambda i, j: (i, j)))
        args.append(residual)
    return pl.pallas_call(
        body,
        grid=(n // tm, c // tn),
        in_specs=in_specs,
        out_specs=pl.BlockSpec((tm, tn), lambda i, j: (i, j)),
        out_shape=jax.ShapeDtypeStruct((n, c), out_dtype),
        compiler_params=_cparams(("parallel", "parallel")),
        name=name,
    )(*args)


def _merge_kernel(g0, g1, g2, g3, y0, y1, y2, y3, w_ref, o_ref):
    acc = g0[...].astype(F32) * _nn(y0[...], w_ref[0])
    acc += g1[...].astype(F32) * _nn(y1[...], w_ref[1])
    acc += g2[...].astype(F32) * _nn(y2[...], w_ref[2])
    acc += g3[...].astype(F32) * _nn(y3[...], w_ref[3])
    o_ref[...] = acc.astype(o_ref.dtype)


def _merge(gates, ys, w_br, tm=512, tn=512):
    n = gates.shape[0]
    nj = D_MODEL // tn
    gate_specs = [pl.BlockSpec((tm, tn), functools.partial(lambda i, j, b: (i, b * nj + j), b=b))
                  for b in range(4)]
    y_specs = [pl.BlockSpec((tm, BRANCH_W), lambda i, j: (i, 0)) for _ in range(4)]
    return pl.pallas_call(
        _merge_kernel,
        grid=(n // tm, nj),
        in_specs=gate_specs + y_specs + [pl.BlockSpec((4, BRANCH_W, tn), lambda i, j: (0, 0, j))],
        out_specs=pl.BlockSpec((tm, tn), lambda i, j: (i, j)),
        out_shape=jax.ShapeDtypeStruct((n, D_MODEL), BF16),
        compiler_params=_cparams(("parallel", "parallel")),
        name="merge",
    )(gates, gates, gates, gates, *ys, w_br)


def _t5_bucket(dist):
    n = jnp.maximum(dist, 0)
    max_exact = NUM_BUCKETS // 2
    nf = jnp.maximum(n, max_exact).astype(F32)
    large = max_exact + (jnp.log(nf / max_exact) / math.log(MAX_DISTANCE / max_exact)
                         * (NUM_BUCKETS - max_exact)).astype(jnp.int32)
    large = jnp.minimum(large, NUM_BUCKETS - 1)
    return jnp.where(n < max_exact, n, large)


def _bucket_lookup(table, bucket):
    tab = table.T.reshape((table.shape[1],) + (1,) * bucket.ndim + (NUM_BUCKETS,))
    out = jnp.zeros((table.shape[1],) + bucket.shape, F32)
    for b in range(NUM_BUCKETS):
        out = jnp.where(bucket[None] == b, tab[..., b], out)
    return out


def _dilated_bias(bias_a):
    qi = jnp.arange(BAND)[:, None]
    kj = jnp.arange(2 * BAND)[None, :]
    rel = qi + BAND - kj
    valid = (rel >= 0) & (rel <= BAND)
    tabs = []
    for dil in DILATIONS:
        b = _bucket_lookup(bias_a, _t5_bucket(rel * dil))
        tabs.append(jnp.where(valid[None], b, NEG))
    return jnp.stack(tabs)


def _diff_bias(bias_b):
    qi = jnp.arange(B_TQ)[:, None]
    kj = jnp.arange(B_TQ)[None, :]
    tabs = []
    for d in range(-1, B_NEAR + 1):
        rel = d * B_TQ + qi - kj
        b = _bucket_lookup(bias_b, _t5_bucket(rel))
        tabs.append(jnp.where((rel >= 0)[None], b, NEG))
    return jnp.stack(tabs, axis=1)


def _dil_attn_kernel(q_ref, kp_ref, kc_ref, vp_ref, vc_ref, gate_ref, bias_ref, o_ref,
                     kcat, vcat, acc_ref, m_ref, l_ref):
    n = pl.program_id(1)
    kcat[0:A_TILE, :] = kp_ref[0]
    kcat[A_TILE:2 * A_TILE, :] = kc_ref[0]
    vcat[0:A_TILE, :] = vp_ref[0]
    vcat[A_TILE:2 * A_TILE, :] = vc_ref[0]
    lane = lax.broadcasted_iota(jnp.int32, (BAND, HEAD_W), 1)
    lo = lane < A_HEAD_DIM
    col = lax.broadcasted_iota(jnp.int32, (BAND, 2 * BAND), 1)
    nblk_tile = A_TILE // BAND

    for p, r in enumerate(DILATIONS):
        def rows(start, r=r):
            return pl.ds(start, BAND, stride=r) if r > 1 else pl.ds(pl.multiple_of(start, BAND), BAND)

        def body(idx, carry, p=p, r=r, rows=rows):
            j = idx // r
            start = j * (BAND * r) + idx % r
            q = q_ref[0, rows(start), :]
            q = (q * (A_HEAD_DIM ** -0.5)).astype(BF16)
            kk = jnp.concatenate([kcat[rows(A_TILE + start - BAND * r), :],
                                  kcat[rows(A_TILE + start), :]], axis=0).astype(BF16)
            vv = jnp.concatenate([vcat[rows(A_TILE + start - BAND * r), :],
                                  vcat[rows(A_TILE + start), :]], axis=0).astype(BF16)
            no_prev = jnp.logical_and(n == 0, j == 0)
            outs, ms, ls = [], [], []
            for hh in range(2):
                qz = jnp.where(lo if hh == 0 else jnp.logical_not(lo), q, jnp.zeros_like(q))
                s = _nt(qz, kk) + bias_ref[p, hh]
                s = jnp.where(jnp.logical_and(no_prev, col < BAND), NEG, s)
                m = jnp.max(s, axis=-1, keepdims=True)
                e = jnp.exp(s - m)
                l = jnp.sum(e, axis=-1, keepdims=True)
                outs.append(_nn(e.astype(BF16), vv))
                ms.append(jnp.broadcast_to(m, (BAND, HEAD_W)))
                ls.append(jnp.broadcast_to(l, (BAND, HEAD_W)))
            o_new = jnp.where(lo, outs[0], outs[1])
            m_new = jnp.where(lo, ms[0], ms[1])
            l_new = jnp.where(lo, ls[0], ls[1])
            if p == 0:
                acc_ref[rows(start), :] = o_new
                m_ref[rows(start), :] = m_new
                l_ref[rows(start), :] = l_new
            else:
                m_old = m_ref[rows(start), :]
                m_tot = jnp.maximum(m_old, m_new)
                a = jnp.exp(m_old - m_tot)
                b = jnp.exp(m_new - m_tot)
                acc_ref[rows(start), :] = a * acc_ref[rows(start), :] + b * o_new
                l_ref[rows(start), :] = a * l_ref[rows(start), :] + b * l_new
                m_ref[rows(start), :] = m_tot
            return carry

        lax.fori_loop(0, nblk_tile, body, 0)

    o_ref[0] = (acc_ref[...] / l_ref[...] * _silu(gate_ref[0])).astype(o_ref.dtype)


def _dil_attn(pf, bias_tab, bsz, t):
    nt = t // A_TILE
    hp = BRANCH_W // HEAD_W
    blk = (1, A_TILE, HEAD_W)

    def cur(cb):
        return pl.BlockSpec(blk, lambda b, n, h: (b, n, cb * hp + h))

    def prev(cb):
        return pl.BlockSpec(blk, lambda b, n, h: (b, jnp.maximum(n - 1, 0), cb * hp + h))

    return pl.pallas_call(
        _dil_attn_kernel,
        grid=(bsz, nt, hp),
        in_specs=[cur(PF_AQ), prev(PF_AK), cur(PF_AK), prev(PF_AV), cur(PF_AV), cur(PF_AGATE),
                  pl.BlockSpec((3, 2, BAND, 2 * BAND), lambda b, n, h: (0, h, 0, 0))],
        out_specs=pl.BlockSpec(blk, lambda b, n, h: (b, n, h)),
        out_shape=jax.ShapeDtypeStruct((bsz, t, BRANCH_W), BF16),
        scratch_shapes=[pltpu.VMEM((2 * A_TILE, HEAD_W), F32), pltpu.VMEM((2 * A_TILE, HEAD_W), F32),
                        pltpu.VMEM((A_TILE, HEAD_W), F32), pltpu.VMEM((A_TILE, HEAD_W), F32),
                        pltpu.VMEM((A_TILE, HEAD_W), F32)],
        compiler_params=_cparams(("parallel", "parallel", "parallel")),
        name="dilated_attn",
    )(pf, pf, pf, pf, pf, pf, bias_tab)


def _diff_attn_kernel(scal_ref, q_ref, k_ref, v_ref, gate_ref, bias_ref, gain_ref, o_ref,
                      kt_ref, vx_ref, s0_ref, s1_ref, p0_ref, p1_ref, al0_ref, al1_ref,
                      acc_ref, m_ref):
    h = pl.program_id(1)
    qi = pl.program_id(2)
    nkb = kt_ref.shape[0]
    nsub = B_TK // B_TQ
    ncol = B_TK // HEAD_W

    @pl.when(qi == 0)
    def _():
        def tr(kb, c):
            rows = pl.ds(pl.multiple_of(kb * B_TK, B_TK), B_TK)
            kt_ref[kb] = k_ref[0, rows, :].astype(F32).T.astype(BF16)
            vx_ref[rows, 0:HEAD_W] = v_ref[0, rows, :]
            vx_ref[rows, HEAD_W:2 * HEAD_W] = jnp.ones((B_TK, HEAD_W), BF16)
            return c
        lax.fori_loop(0, nkb, tr, 0)

    lane = lax.broadcasted_iota(jnp.int32, (B_TQ, HEAD_W), 1)
    lo = lane < DIFF_QK
    q = q_ref[0] * (DIFF_QK ** -0.5)
    zero = jnp.zeros_like(q)
    qz = (jnp.where(lo, q, zero), jnp.where(lo, zero, q))
    cfar = scal_ref[2 + h]
    nk = qi // nsub + 1
    n_far = jnp.maximum((qi - (B_NEAR - 1)) // nsub, 0)
    per_sub = B_TQ // HEAD_W

    s_bufs, p_bufs, al_bufs = (s0_ref, s1_ref), (p0_ref, p1_ref), (al0_ref, al1_ref)
    acc_ref[...] = jnp.zeros_like(acc_ref)
    m_ref[...] = jnp.full_like(m_ref, NEG)
    al1_ref[...] = jnp.ones_like(al1_ref)
    p1_ref[...] = jnp.zeros_like(p1_ref)

    def qk(kb, par):
        kt = kt_ref[jnp.minimum(kb, nk - 1)]
        for mi in range(2):
            s_bufs[par][mi] = _nn(qz[mi], kt)

    def pv(kb, par):
        kb = jnp.clip(kb, 0, nk - 1)
        vx = vx_ref[pl.ds(pl.multiple_of(kb * B_TK, B_TK), B_TK), :]
        for mi in range(2):
            a = al_bufs[par][mi]
            upd = _nn(p_bufs[par][mi], vx)
            acc_ref[mi] = jnp.concatenate([a, a], axis=1) * acc_ref[mi] + upd

    def col(par, mi, c, kb, near):
        x = s_bufs[par][mi, :, c * HEAD_W:(c + 1) * HEAD_W]
        if near:
            tile = jnp.clip(qi - nsub * kb - c // per_sub, -1, B_NEAR) + 1
            cc = (c % per_sub) * HEAD_W
            x = x + bias_ref[0, tile, :, cc:cc + HEAD_W]
        return x

    def softmax(kb, par, near):
        shift = 0.0 if near else cfar
        m_sub = []
        for mi in range(2):
            mx = col(par, mi, 0, kb, near)
            for c in range(1, ncol):
                mx = jnp.maximum(mx, col(par, mi, c, kb, near))
            m_prev = m_ref[mi]
            m_next = jnp.maximum(m_prev, jnp.max(mx, axis=-1, keepdims=True) + shift)
            al_bufs[par][mi] = jnp.exp(m_prev - m_next)
            m_ref[mi] = m_next
            m_sub.append(m_next - shift)
        for mi in range(2):
            for c in range(ncol):
                e = jnp.exp(col(par, mi, c, kb, near) - m_sub[mi])
                p_bufs[par][mi, :, c * HEAD_W:(c + 1) * HEAD_W] = e.astype(BF16)

    def pair(u, near):
        for par in range(2):
            t = 2 * u + par
            pv(t - 1, 1 - par)
            qk(t + 1, 1 - par)
            softmax(t, par, near)

    def far_body(u, c):
        pair(u, False)
        return c

    def near_body(u, c):
        pair(u, True)
        return c

    far_pairs = n_far // 2
    all_pairs = (nk + 1) // 2
    qk(0, 0)
    lax.fori_loop(0, far_pairs, far_body, 0)
    lax.fori_loop(far_pairs, all_pairs, near_body, 0)
    pv(2 * all_pairs - 1, 1)

    lam = scal_ref[0]
    a1, a2 = acc_ref[0], acc_ref[1]
    o = a1[:, :HEAD_W] / a1[:, HEAD_W:] - lam * (a2[:, :HEAD_W] / a2[:, HEAD_W:])
    ms = jnp.mean(o * o, axis=-1, keepdims=True)
    o = o * lax.rsqrt(ms + RMS_EPS) * gain_ref[...] * scal_ref[1]
    o_ref[0] = (o * _silu(gate_ref[0])).astype(o_ref.dtype)


def _diff_attn(pb, pf, bias_tab, scal, gain, bsz, t):
    nh = BRANCH_W // HEAD_W
    nq = t // B_TQ
    return pl.pallas_call(
        _diff_attn_kernel,
        grid=(bsz, nh, nq),
        in_specs=[pl.BlockSpec(memory_space=pltpu.SMEM),
                  pl.BlockSpec((1, B_TQ, HEAD_W), lambda b, h, i: (b, i, h)),
                  pl.BlockSpec((1, t, HEAD_W), lambda b, h, i: (b, 0, nh + h)),
                  pl.BlockSpec((1, t, HEAD_W), lambda b, h, i: (b, 0, 2 * nh + h)),
                  pl.BlockSpec((1, B_TQ, HEAD_W), lambda b, h, i: (b, i, PF_BGATE * nh + h)),
                  pl.BlockSpec((1, B_NEAR + 2, B_TQ, B_TQ), lambda b, h, i: (h, 0, 0, 0)),
                  pl.BlockSpec((1, HEAD_W), lambda b, h, i: (0, 0))],
        out_specs=pl.BlockSpec((1, B_TQ, HEAD_W), lambda b, h, i: (b, i, h)),
        out_shape=jax.ShapeDtypeStruct((bsz, t, BRANCH_W), BF16),
        scratch_shapes=[pltpu.VMEM((t // B_TK, HEAD_W, B_TK), BF16),
                        pltpu.VMEM((t, 2 * HEAD_W), BF16),
                        pltpu.VMEM((2, B_TQ, B_TK), F32),
                        pltpu.VMEM((2, B_TQ, B_TK), F32),
                        pltpu.VMEM((2, B_TQ, B_TK), BF16),
                        pltpu.VMEM((2, B_TQ, B_TK), BF16),
                        pltpu.VMEM((2, B_TQ, HEAD_W), F32),
                        pltpu.VMEM((2, B_TQ, HEAD_W), F32),
                        pltpu.VMEM((2, B_TQ, 2 * HEAD_W), F32),
                        pltpu.VMEM((2, B_TQ, HEAD_W), F32)],
        compiler_params=_cparams(("arbitrary", "arbitrary", "arbitrary")),
        name="diff_attn",
    )(scal, pb, pb, pb, pf, bias_tab, gain.reshape(1, HEAD_W).astype(F32))


def _delta_kernel(qkv_ref, z_ref, small_ref, conv_ref, par_ref, gain_ref, o_ref,
                  xe_ref, s_ref):
    tc = qkv_ref.shape[1]
    nchunk = tc // CHUNK

    @pl.when(pl.program_id(1) == 0)
    def _():
        xe_ref[0:8, :] = jnp.zeros((8, xe_ref.shape[1]), F32)
        s_ref[...] = jnp.zeros_like(s_ref)

    xe_ref[8:8 + tc, :] = qkv_ref[0]
    conv = conv_ref[0:1, :] * xe_ref[pl.ds(8 - 3, tc), :]
    for kk in range(1, CONV_K):
        conv = conv + conv_ref[kk:kk + 1, :] * xe_ref[pl.ds(8 - 3 + kk, tc), :]
    xe_ref[0:8, :] = xe_ref[tc:tc + 8, :]
    c = _silu(conv)

    small = small_ref[0]
    beta_all = _sigmoid(small)
    xa = small + par_ref[1:2, :]
    softplus = jnp.maximum(xa, 0.0) + _log1p_exp_neg_abs(xa)
    g_all = -jnp.exp(par_ref[0:1, :]) * softplus

    ri = lax.broadcasted_iota(jnp.int32, (CHUNK, CHUNK), 0)
    ci = lax.broadcasted_iota(jnp.int32, (CHUNK, CHUNK), 1)
    tri = ri >= ci
    strict = ri > ci
    tri_f = tri.astype(F32)
    eye = (ri == ci).astype(F32)
    sel = (lax.broadcasted_iota(jnp.int32, (8, HEAD_W), 1)
           == lax.broadcasted_iota(jnp.int32, (8, HEAD_W), 0) + 4).astype(F32)
    nh = BRANCH_W // HEAD_W

    for ch in range(nchunk):
        r0 = ch * CHUNK
        g_c = g_all[r0:r0 + CHUNK]
        gcum = _nn(tri_f, g_c, HI)
        grow_all = _nt(sel, gcum, HI)
        for h in range(nh):
            q = c[r0:r0 + CHUNK, h * HEAD_W:(h + 1) * HEAD_W]
            k = c[r0:r0 + CHUNK, BRANCH_W + h * HEAD_W:BRANCH_W + (h + 1) * HEAD_W]
            v = c[r0:r0 + CHUNK, 2 * BRANCH_W + h * HEAD_W:2 * BRANCH_W + (h + 1) * HEAD_W]
            q = q * lax.rsqrt(jnp.sum(q * q, axis=-1, keepdims=True) + 1e-6) * (HEAD_W ** -0.5)
            k = k * lax.rsqrt(jnp.sum(k * k, axis=-1, keepdims=True) + 1e-6)
            bcol = beta_all[r0:r0 + CHUNK, h:h + 1]
            gcol = gcum[:, 4 + h:5 + h]
            grow = grow_all[h:h + 1, :]
            glast = gcum[CHUNK - 1:CHUNK, 4 + h:5 + h]
            gamma = jnp.exp(jnp.where(tri, gcol - grow, NEG))
            kkt = _nt(k, k, HI)
            x = jnp.where(strict, -(bcol * kkt * gamma), 0.0)
            tinv = eye + x
            pw = x
            for _ in range(5):
                pw = _nn(pw, pw, HI)
                tinv = tinv + _nn(tinv, pw, HI)
            egc = jnp.exp(gcol)
            u = _nn(tinv, v * bcol, HI)
            w = _nn(tinv, k * (bcol * egc), HI)
            aqk = _nt(q, k, HI) * gamma
            qd = q * egc
            kd = k * jnp.exp(glast - gcol)
            s = s_ref[h]
            v_new = u - _nn(w, s, HI)
            o = _nn(qd, s, HI) + _nn(aqk, v_new, HI)
            s_ref[h] = jnp.exp(glast) * s + _tn(kd, v_new, HI)
            ms = jnp.mean(o * o, axis=-1, keepdims=True)
            o = o * lax.rsqrt(ms + RMS_EPS) * gain_ref[...]
            zg = z_ref[0, r0:r0 + CHUNK, h * HEAD_W:(h + 1) * HEAD_W]
            o_ref[0, r0:r0 + CHUNK, h * HEAD_W:(h + 1) * HEAD_W] = (o * _silu(zg)).astype(o_ref.dtype)


def _delta_net(pf, small, conv_w, par, gain, bsz, t, tc=CHUNK):
    cw = 3 * BRANCH_W
    return pl.pallas_call(
        _delta_kernel,
        grid=(bsz, t // tc),
        in_specs=[pl.BlockSpec((1, tc, cw), lambda b, i: (b, i, 0)),
                  pl.BlockSpec((1, tc, BRANCH_W), lambda b, i: (b, i, PF_CZ)),
                  pl.BlockSpec((1, tc, HEAD_W), lambda b, i: (b, i, 0)),
                  pl.BlockSpec((8, cw), lambda b, i: (0, 0)),
                  pl.BlockSpec((8, HEAD_W), lambda b, i: (0, 0)),
                  pl.BlockSpec((1, HEAD_W), lambda b, i: (0, 0))],
        out_specs=pl.BlockSpec((1, tc, BRANCH_W), lambda b, i: (b, i, 0)),
        out_shape=jax.ShapeDtypeStruct((bsz, t, BRANCH_W), BF16),
        scratch_shapes=[pltpu.VMEM((tc + 8, cw), F32),
                        pltpu.VMEM((BRANCH_W // HEAD_W, HEAD_W, HEAD_W), F32)],
        compiler_params=_cparams(("arbitrary", "arbitrary")),
        name="delta_net",
    )(pf, pf, small, conv_w, par, gain.reshape(1, HEAD_W).astype(F32))


def _hgrn_kernel(q_ref, f_ref, i_ref, gate_ref, lb_ref, gain_ref, o_ref, st_ref):
    tc = q_ref.shape[1]
    nchunk = tc // CHUNK
    nh = BRANCH_W // HEAD_W
    nsub = CHUNK // SUB

    @pl.when(pl.program_id(1) == 0)
    def _():
        st_ref[...] = jnp.zeros_like(st_ref)

    lb = lb_ref[...]
    df = f_ref[0]
    log_sig = jnp.minimum(df, 0.0) - _log1p_exp_neg_abs(df)
    a = jnp.log(lb)
    b = jnp.log1p(-lb) + log_sig
    logf_all = jnp.maximum(a, b) + _log1p_exp_neg_abs(a - b)
    k_all = (1.0 - lb) * _sigmoid(-df)

    ri = lax.broadcasted_iota(jnp.int32, (CHUNK, CHUNK), 0)
    ci = lax.broadcasted_iota(jnp.int32, (CHUNK, CHUNK), 1)
    tri_f = (ri >= ci).astype(F32)
    sub_row = lax.broadcasted_iota(jnp.int32, (SUB, HEAD_W), 0)

    for ch in range(nchunk):
        r0 = ch * CHUNK
        for h in range(nh):
            cs = slice(h * HEAD_W, (h + 1) * HEAD_W)
            q = q_ref[0, r0:r0 + CHUNK, cs]
            v = i_ref[0, r0:r0 + CHUNK, cs]
            k = k_all[r0:r0 + CHUNK, cs]
            lf = logf_all[r0:r0 + CHUNK, cs]
            bc = _nn(tri_f, lf, HI)
            bl = bc[CHUNK - 1:CHUNK, :]
            st = st_ref[h]
            o_inter = _nt(q * jnp.exp(bc), st, HI)
            pieces = []
            for si in range(nsub):
                rs = slice(si * SUB, (si + 1) * SUB)
                q_s, bc_s = q[rs], bc[rs]
                o_s = o_inter[rs]
                if si > 0:
                    ref_row = bc[si * SUB - 1:si * SUB, :]
                    q_dec = q_s * jnp.exp(bc_s - ref_row)
                    k_dec = k[:si * SUB] * jnp.exp(ref_row - bc[:si * SUB])
                    o_s = o_s + _nn(_nt(q_dec, k_dec, HI), v[:si * SUB], HI)
                for j in range(SUB):
                    jj = si * SUB + j
                    e = jnp.exp(jnp.where(sub_row >= j, bc_s - bc[jj:jj + 1, :], NEG))
                    a_col = jnp.sum(q_s * k[jj:jj + 1, :] * e, axis=-1, keepdims=True)
                    o_s = o_s + a_col * v[jj:jj + 1, :]
                pieces.append(o_s)
            o = jnp.concatenate(pieces, axis=0)
            st_ref[h] = st * jnp.exp(bl) + _tn(v, k * jnp.exp(bl - bc), HI)
            ms = jnp.mean(o * o, axis=-1, keepdims=True)
            o = o * lax.rsqrt(ms + RMS_EPS) * gain_ref[...]
            o_ref[0, r0:r0 + CHUNK, cs] = (o * _silu(gate_ref[0, r0:r0 + CHUNK, cs])).astype(o_ref.dtype)


def _hgrn(pf, lb, gain, bsz, t, tc=CHUNK):
    def spec(cb):
        return pl.BlockSpec((1, tc, BRANCH_W), lambda b, i: (b, i, cb))

    return pl.pallas_call(
        _hgrn_kernel,
        grid=(bsz, t // tc),
        in_specs=[spec(PF_DQ), spec(PF_DF), spec(PF_DI), spec(PF_DGATE),
                  pl.BlockSpec((1, BRANCH_W), lambda b, i: (0, 0)),
                  pl.BlockSpec((1, HEAD_W), lambda b, i: (0, 0))],
        out_specs=pl.BlockSpec((1, tc, BRANCH_W), lambda b, i: (b, i, 0)),
        out_shape=jax.ShapeDtypeStruct((bsz, t, BRANCH_W), BF16),
        scratch_shapes=[pltpu.VMEM((BRANCH_W // HEAD_W, HEAD_W, HEAD_W), F32)],
        compiler_params=_cparams(("arbitrary", "arbitrary")),
        name="hgrn2",
    )(pf, pf, pf, pf, lb.reshape(1, BRANCH_W), gain.reshape(1, HEAD_W).astype(F32))


def _split_w_in(w):
    o = 0
    parts = {}
    for name, width in (("a_q", 512), ("a_k", 512), ("a_v", 512), ("a_gate", 512),
                        ("b_q", 512), ("b_k", 512), ("b_v", 512), ("b_gate", 512),
                        ("c_qkv", 1536), ("c_z", 512), ("c_beta", 4), ("c_a", 4),
                        ("d_q", 512), ("d_f", 512), ("d_i", 512), ("d_gate", 512),
                        ("merge", 4 * D_MODEL)):
        parts[name] = w[:, o:o + width]
        o += width
    w_f = jnp.concatenate([parts[k] for k in ("c_qkv", "c_z", "a_gate", "b_gate", "d_q", "d_f",
                                              "d_i", "d_gate", "a_q", "a_k", "a_v")], axis=1)
    w_b = jnp.concatenate([parts[k] for k in ("b_q", "b_k", "b_v")], axis=1)
    w_s = jnp.concatenate([parts["c_beta"], parts["c_a"],
                           jnp.zeros((w.shape[0], HEAD_W - 8), w.dtype)], axis=1)
    return w_f.astype(BF16), w_b.astype(BF16), w_s.astype(BF16), parts["merge"].astype(BF16)


def kernel(x, norm_gain, w_in, rel_bias, diff_lambda, diff_subln_gain, dn_conv, dn_a_log, dn_dt_bias,
           dn_norm_gain, hg_lb_logits, hg_norm_gain, w_branch, w_out, final_gain):
    bsz, t, d = x.shape
    n = bsz * t
    depth = w_in.shape[0]
    lb_p = jax.nn.softmax(hg_lb_logits.astype(F32), axis=0)
    hg_lb = jnp.clip(jnp.cumsum(lb_p, axis=0) - lb_p[0], 0.0, 1.0)
    bias_a = _dilated_bias(rel_bias[:, :8].astype(F32))
    bias_b = _diff_bias(rel_bias[:, 8:].astype(F32))
    cfar = rel_bias[NUM_BUCKETS - 1, 8:].astype(F32)

    xf = x.reshape(n, d).astype(F32)
    for layer in range(depth):
        w_f, w_b, w_s, w_g = _split_w_in(w_in[layer])
        h = _rmsnorm(xf, norm_gain[layer], BF16)
        pf = _mm(h, w_f, F32, tn=512, name="proj_f32").reshape(bsz, t, PF_COLS)
        pb = _mm(h, w_b, BF16, tn=512, name="proj_bf16").reshape(bsz, t, 3 * BRANCH_W)
        ps = _mm(h, w_s, F32, name="proj_small").reshape(bsz, t, HEAD_W)
        gates = _mm(h, w_g, BF16, act="sigmoid", name="proj_gates")

        y_a = _dil_attn(pf, bias_a, bsz, t)

        lam_init = 0.8 - 0.6 * math.exp(-0.3 * layer)
        lq1, lk1, lq2, lk2 = diff_lambda[layer].astype(F32)
        lam = jnp.exp(jnp.sum(lq1 * lk1)) - jnp.exp(jnp.sum(lq2 * lk2)) + lam_init
        scal = jnp.concatenate([jnp.stack([lam, jnp.asarray(1.0 - lam_init, F32)]), cfar,
                                jnp.zeros((2,), F32)])
        y_b = _diff_attn(pb, pf, bias_b, scal, diff_subln_gain[layer], bsz, t)

        conv_w = jnp.concatenate([dn_conv[layer].astype(F32),
                                  jnp.zeros((8 - CONV_K, 3 * BRANCH_W), F32)], axis=0)
        par = jnp.zeros((8, HEAD_W), F32)
        par = par.at[0, 4:8].set(dn_a_log[layer].astype(F32)).at[1, 4:8].set(dn_dt_bias[layer].astype(F32))
        y_c = _delta_net(pf, ps, conv_w, par, dn_norm_gain[layer], bsz, t)

        y_d = _hgrn(pf, hg_lb[layer], hg_norm_gain[layer], bsz, t)

        ys = [y.reshape(n, BRANCH_W) for y in (y_a, y_b, y_c, y_d)]
        mixed = _merge(gates, ys, w_branch[layer].astype(BF16))
        xf = _mm(mixed, w_out[layer].astype(BF16), F32, residual=xf, name="out_proj")
    out = _rmsnorm(xf, final_gain, F32)
    return out.reshape(bsz, t, d).astype(x.dtype)
```

```python
import functools
import math

import jax
import jax.numpy as jnp
import numpy as np
from jax import lax
from jax.experimental import pallas as pl
from jax.experimental.pallas import tpu as pltpu

F32 = jnp.float32
BF16 = jnp.bfloat16
HI = lax.Precision.HIGHEST

D_MODEL = 2048
BRANCH_W = 512
HEAD_W = 128
A_HEAD_DIM = 64
DILATIONS = (1, 4, 16)
BAND = 128
A_TILE = BAND * DILATIONS[-1]
A_GROUP = 4
DIFF_QK = 64
B_TQ = 256
B_TK = 512
B_NEAR = 2048 // B_TQ + 1
CHUNK = 64
DELTA_TILE = 256
HGRN_TILE = 256
SUB = 16
CONV_K = 4
NUM_BUCKETS = 32
MAX_DISTANCE = 2048
RMS_EPS = 1e-6
NEG = -1e30
VMEM_LIMIT = 56 * 1024 * 1024

PF_CQKV, PF_CZ, PF_AGATE, PF_BGATE, PF_DQ, PF_DF, PF_DI, PF_DGATE, PF_AQ, PF_AK, PF_AV = 0, 3, 4, 5, 6, 7, 8, 9, 10, 11, 12
PF_COLS = 13 * 512


def _cparams(sem):
    return pltpu.CompilerParams(dimension_semantics=sem, vmem_limit_bytes=VMEM_LIMIT)


def _nt(a, b, precision=None):
    return lax.dot_general(a, b, (((1,), (1,)), ((), ())), precision=precision,
                           preferred_element_type=F32)


def _tn(a, b, precision=None):
    return lax.dot_general(a, b, (((0,), (0,)), ((), ())), precision=precision,
                           preferred_element_type=F32)


def _nn(a, b, precision=None):
    return jnp.dot(a, b, precision=precision, preferred_element_type=F32)


def _sigmoid(x):
    return 1.0 / (1.0 + jnp.exp(-x))


def _silu(x):
    return x * _sigmoid(x)


def _log1p_exp_neg_abs(x):
    return jnp.log1p(jnp.exp(-jnp.abs(x)))


def _hi_lo(x):
    hi = x.astype(BF16)
    return hi, (x - hi.astype(F32)).astype(BF16)


def _split_cols(a):
    hi = a.astype(BF16).astype(F32)
    lo = a - hi
    return jnp.concatenate([hi, lo, hi, lo], axis=1).astype(BF16)


def _split_rows(b):
    hi, lo = _hi_lo(b)
    return jnp.concatenate([hi, hi, lo, lo], axis=0)


def _split_rows_lhs(a):
    hi, lo = _hi_lo(a)
    return jnp.concatenate([hi, lo, hi, lo], axis=0)


def _nt_acc(a, b):
    ah, al = _hi_lo(a)
    bh, bl = _hi_lo(b)
    return (_nt(jnp.concatenate([ah, al], axis=1), jnp.concatenate([bh, bh], axis=1))
            + _nt(ah, bl))


def _rmsnorm_kernel(x_ref, g_ref, o_ref):
    x = x_ref[...]
    ms = jnp.mean(x * x, axis=-1, keepdims=True)
    o_ref[...] = (x * lax.rsqrt(ms + RMS_EPS) * g_ref[...]).astype(o_ref.dtype)


def _rmsnorm(x, gain, out_dtype, tm=512):
    n, d = x.shape
    return pl.pallas_call(
        _rmsnorm_kernel,
        grid=(n // tm,),
        in_specs=[pl.BlockSpec((tm, d), lambda i: (i, 0)),
                  pl.BlockSpec((1, d), lambda i: (0, 0))],
        out_specs=pl.BlockSpec((tm, d), lambda i: (i, 0)),
        out_shape=jax.ShapeDtypeStruct((n, d), out_dtype),
        compiler_params=_cparams(("parallel",)),
        name="rmsnorm",
    )(x, gain.reshape(1, d).astype(F32))


def _mm_kernel(a_ref, w_ref, o_ref, *, act):
    acc = _nn(a_ref[...], w_ref[...])
    if act == "sigmoid":
        acc = _sigmoid(acc)
    o_ref[...] = acc.astype(o_ref.dtype)


def _mm_res_kernel(a_ref, w_ref, r_ref, o_ref):
    o_ref[...] = r_ref[...] + _nn(a_ref[...], w_ref[...])


def _mm(a, w, out_dtype, act=None, residual=None, tm=512, tn=1024, name="mm"):
    n, k = a.shape
    c = w.shape[1]
    tn = min(tn, c)
    in_specs = [pl.BlockSpec((tm, k), lambda i, j: (i, 0)),
                pl.BlockSpec((k, tn), lambda i, j: (0, j))]
    args = [a, w]
    if residual is None:
        body = functools.partial(_mm_kernel, act=act)
    else:
        body = _mm_res_kernel
        in_specs.append(pl.BlockSpec((tm, tn), lambda i, j: (i, j)))
        args.append(residual)
    return pl.pallas_call(
        body,
        grid=(n // tm, c // tn),
        in_specs=in_specs,
        out_specs=pl.BlockSpec((tm, tn), lambda i, j: (i, j)),
        out_shape=jax.ShapeDtypeStruct((n, c), out_dtype),
        compiler_params=_cparams(("parallel", "parallel")),
        name=name,
    )(*args)


def _merge_kernel(g0, g1, g2, g3, y0, y1, y2, y3, w_ref, o_ref):
    acc = g0[...].astype(F32) * _nn(y0[...], w_ref[0])
    acc += g1[...].astype(F32) * _nn(y1[...], w_ref[1])
    acc += g2[...].astype(F32) * _nn(y2[...], w_ref[2])
    acc += g3[...].astype(F32) * _nn(y3[...], w_ref[3])
    o_ref[...] = acc.astype(o_ref.dtype)


def _merge(gates, ys, w_br, tm=512, tn=512):
    n = gates.shape[0]
    nj = D_MODEL // tn
    gate_specs = [pl.BlockSpec((tm, tn), functools.partial(lambda i, j, b: (i, b * nj + j), b=b))
                  for b in range(4)]
    y_specs = [pl.BlockSpec((tm, BRANCH_W), lambda i, j: (i, 0)) for _ in range(4)]
    return pl.pallas_call(
        _merge_kernel,
        grid=(n // tm, nj),
        in_specs=gate_specs + y_specs + [pl.BlockSpec((4, BRANCH_W, tn), lambda i, j: (0, 0, j))],
        out_specs=pl.BlockSpec((tm, tn), lambda i, j: (i, j)),
        out_shape=jax.ShapeDtypeStruct((n, D_MODEL), BF16),
        compiler_params=_cparams(("parallel", "parallel")),
        name="merge",
    )(gates, gates, gates, gates, *ys, w_br)


def _t5_bucket(dist):
    n = jnp.maximum(dist, 0)
    max_exact = NUM_BUCKETS // 2
    nf = jnp.maximum(n, max_exact).astype(F32)
    large = max_exact + (jnp.log(nf / max_exact) / math.log(MAX_DISTANCE / max_exact)
                         * (NUM_BUCKETS - max_exact)).astype(jnp.int32)
    large = jnp.minimum(large, NUM_BUCKETS - 1)
    return jnp.where(n < max_exact, n, large)


def _bucket_lookup(table, bucket):
    tab = table.T.reshape((table.shape[1],) + (1,) * bucket.ndim + (NUM_BUCKETS,))
    out = jnp.zeros((table.shape[1],) + bucket.shape, F32)
    for b in range(NUM_BUCKETS):
        out = jnp.where(bucket[None] == b, tab[..., b], out)
    return out


def _dilated_bias(bias_a):
    qi = jnp.arange(BAND)[:, None]
    kj = jnp.arange(2 * BAND)[None, :]
    rel = qi + BAND - kj
    valid = (rel >= 0) & (rel <= BAND)
    tabs = []
    for dil in DILATIONS:
        b = _bucket_lookup(bias_a, _t5_bucket(rel * dil))
        tabs.append(jnp.where(valid[None], b, NEG))
    return jnp.stack(tabs)


def _diff_bias(bias_b):
    qi = jnp.arange(B_TQ)[:, None]
    kj = jnp.arange(B_TQ)[None, :]
    tabs = []
    for d in range(-1, B_NEAR + 1):
        rel = d * B_TQ + qi - kj
        b = _bucket_lookup(bias_b, _t5_bucket(rel))
        tabs.append(jnp.where((rel >= 0)[None], b, NEG))
    return jnp.stack(tabs, axis=1)


def _dil_attn_kernel(q_ref, kp_ref, kc_ref, vp_ref, vc_ref, gate_ref, bias_ref, o_ref,
                     kcat, vcat, acc_ref, m_ref, l_ref):
    n = pl.program_id(1)
    kcat[0:A_TILE, :] = kp_ref[0]
    kcat[A_TILE:2 * A_TILE, :] = kc_ref[0]
    vcat[0:A_TILE, :] = vp_ref[0]
    vcat[A_TILE:2 * A_TILE, :] = vc_ref[0]
    lane = lax.broadcasted_iota(jnp.int32, (BAND, HEAD_W), 1)
    lo = lane < A_HEAD_DIM
    col = lax.broadcasted_iota(jnp.int32, (BAND, 2 * BAND), 1)
    nblk_tile = A_TILE // BAND

    for p, r in enumerate(DILATIONS):
        def rows(start, r=r):
            return pl.ds(start, BAND, stride=r) if r > 1 else pl.ds(pl.multiple_of(start, BAND), BAND)

        def body(it, carry, p=p, r=r, rows=rows):
            starts, no_prevs, vvs = [], [], []
            ss, es, ms, ls, outs = {}, {}, {}, {}, {}
            for g in range(A_GROUP):
                idx = it * A_GROUP + g
                j = idx // r
                start = j * (BAND * r) + idx % r
                starts.append(start)
                no_prevs.append(jnp.logical_and(n == 0, j == 0))
                q = (q_ref[0, rows(start), :] * (A_HEAD_DIM ** -0.5)).astype(BF16)
                kk = jnp.concatenate([kcat[rows(A_TILE + start - BAND * r), :],
                                      kcat[rows(A_TILE + start), :]], axis=0).astype(BF16)
                vvs.append(jnp.concatenate([vcat[rows(A_TILE + start - BAND * r), :],
                                            vcat[rows(A_TILE + start), :]], axis=0).astype(BF16))
                for hh in range(2):
                    qz = jnp.where(lo if hh == 0 else jnp.logical_not(lo), q, jnp.zeros_like(q))
                    ss[g, hh] = _nt(qz, kk)
            for g in range(A_GROUP):
                for hh in range(2):
                    s = ss[g, hh] + bias_ref[p, hh]
                    s = jnp.where(jnp.logical_and(no_prevs[g], col < BAND), NEG, s)
                    m = jnp.max(s, axis=-1, keepdims=True)
                    e = jnp.exp(s - m)
                    ms[g, hh] = jnp.broadcast_to(m, (BAND, HEAD_W))
                    ls[g, hh] = jnp.broadcast_to(jnp.sum(e, axis=-1, keepdims=True), (BAND, HEAD_W))
                    es[g, hh] = e.astype(BF16)
            for g in range(A_GROUP):
                for hh in range(2):
                    outs[g, hh] = _nn(es[g, hh], vvs[g])
            merged = []
            for g in range(A_GROUP):
                o_new = jnp.where(lo, outs[g, 0], outs[g, 1])
                m_new = jnp.where(lo, ms[g, 0], ms[g, 1])
                l_new = jnp.where(lo, ls[g, 0], ls[g, 1])
                if p > 0:
                    rws = rows(starts[g])
                    m_old = m_ref[rws, :]
                    m_tot = jnp.maximum(m_old, m_new)
                    a = jnp.exp(m_old - m_tot)
                    b = jnp.exp(m_new - m_tot)
                    o_new = a * acc_ref[rws, :] + b * o_new
                    l_new = a * l_ref[rws, :] + b * l_new
                    m_new = m_tot
                merged.append((o_new, m_new, l_new))
            for g in range(A_GROUP):
                rws = rows(starts[g])
                acc_ref[rws, :], m_ref[rws, :], l_ref[rws, :] = merged[g]
            return carry

        lax.fori_loop(0, nblk_tile // A_GROUP, body, 0)

    o_ref[0] = (acc_ref[...] / l_ref[...] * _silu(gate_ref[0])).astype(o_ref.dtype)


def _dil_attn(pf, bias_tab, bsz, t):
    nt = t // A_TILE
    hp = BRANCH_W // HEAD_W
    blk = (1, A_TILE, HEAD_W)

    def cur(cb):
        return pl.BlockSpec(blk, lambda b, n, h: (b, n, cb * hp + h))

    def prev(cb):
        return pl.BlockSpec(blk, lambda b, n, h: (b, jnp.maximum(n - 1, 0), cb * hp + h))

    return pl.pallas_call(
        _dil_attn_kernel,
        grid=(bsz, nt, hp),
        in_specs=[cur(PF_AQ), prev(PF_AK), cur(PF_AK), prev(PF_AV), cur(PF_AV), cur(PF_AGATE),
                  pl.BlockSpec((3, 2, BAND, 2 * BAND), lambda b, n, h: (0, h, 0, 0))],
        out_specs=pl.BlockSpec(blk, lambda b, n, h: (b, n, h)),
        out_shape=jax.ShapeDtypeStruct((bsz, t, BRANCH_W), BF16),
        scratch_shapes=[pltpu.VMEM((2 * A_TILE, HEAD_W), F32), pltpu.VMEM((2 * A_TILE, HEAD_W), F32),
                        pltpu.VMEM((A_TILE, HEAD_W), F32), pltpu.VMEM((A_TILE, HEAD_W), F32),
                        pltpu.VMEM((A_TILE, HEAD_W), F32)],
        compiler_params=_cparams(("parallel", "parallel", "parallel")),
        name="dilated_attn",
    )(pf, pf, pf, pf, pf, pf, bias_tab)


def _diff_attn_kernel(scal_ref, q_ref, k_ref, v_ref, gate_ref, bias_ref, gain_ref, o_ref,
                      kt_ref, vx_ref, s0_ref, s1_ref, p0_ref, p1_ref, al0_ref, al1_ref,
                      acc_ref, m_ref):
    h = pl.program_id(1)
    qi = pl.program_id(2)
    nkb = kt_ref.shape[0]
    nsub = B_TK // B_TQ
    ncol = B_TK // HEAD_W

    @pl.when(qi == 0)
    def _():
        def tr(kb, c):
            rows = pl.ds(pl.multiple_of(kb * B_TK, B_TK), B_TK)
            kt_ref[kb] = k_ref[0, rows, :].astype(F32).T.astype(BF16)
            vx_ref[rows, 0:HEAD_W] = v_ref[0, rows, :]
            vx_ref[rows, HEAD_W:2 * HEAD_W] = jnp.ones((B_TK, HEAD_W), BF16)
            return c
        lax.fori_loop(0, nkb, tr, 0)

    lane = lax.broadcasted_iota(jnp.int32, (B_TQ, HEAD_W), 1)
    lo = lane < DIFF_QK
    q = q_ref[0] * (DIFF_QK ** -0.5)
    zero = jnp.zeros_like(q)
    qz = (jnp.where(lo, q, zero), jnp.where(lo, zero, q))
    cfar = scal_ref[2 + h]
    nk = qi // nsub + 1
    n_far = jnp.maximum((qi - (B_NEAR - 1)) // nsub, 0)
    per_sub = B_TQ // HEAD_W

    s_bufs, p_bufs, al_bufs = (s0_ref, s1_ref), (p0_ref, p1_ref), (al0_ref, al1_ref)
    acc_ref[...] = jnp.zeros_like(acc_ref)
    m_ref[...] = jnp.full_like(m_ref, NEG)
    al1_ref[...] = jnp.ones_like(al1_ref)
    p1_ref[...] = jnp.zeros_like(p1_ref)

    def qk(kb, par):
        kt = kt_ref[jnp.minimum(kb, nk - 1)]
        for mi in range(2):
            s_bufs[par][mi] = _nn(qz[mi], kt)

    def pv(kb, par):
        kb = jnp.clip(kb, 0, nk - 1)
        vx = vx_ref[pl.ds(pl.multiple_of(kb * B_TK, B_TK), B_TK), :]
        for mi in range(2):
            a = al_bufs[par][mi]
            upd = _nn(p_bufs[par][mi], vx)
            acc_ref[mi] = jnp.concatenate([a, a], axis=1) * acc_ref[mi] + upd

    def col(par, mi, c, kb, near):
        x = s_bufs[par][mi, :, c * HEAD_W:(c + 1) * HEAD_W]
        if near:
            tile = jnp.clip(qi - nsub * kb - c // per_sub, -1, B_NEAR) + 1
            cc = (c % per_sub) * HEAD_W
            x = x + bias_ref[0, tile, :, cc:cc + HEAD_W]
        return x

    def softmax(kb, par, near):
        shift = 0.0 if near else cfar
        m_sub = []
        for mi in range(2):
            mx = col(par, mi, 0, kb, near)
            for c in range(1, ncol):
                mx = jnp.maximum(mx, col(par, mi, c, kb, near))
            m_prev = m_ref[mi]
            m_next = jnp.maximum(m_prev, jnp.max(mx, axis=-1, keepdims=True) + shift)
            al_bufs[par][mi] = jnp.exp(m_prev - m_next)
            m_ref[mi] = m_next
            m_sub.append(m_next - shift)
        for mi in range(2):
            for c in range(ncol):
                e = jnp.exp(col(par, mi, c, kb, near) - m_sub[mi])
                p_bufs[par][mi, :, c * HEAD_W:(c + 1) * HEAD_W] = e.astype(BF16)

    def pair(u, near):
        for par in range(2):
            t = 2 * u + par
            pv(t - 1, 1 - par)
            qk(t + 1, 1 - par)
            softmax(t, par, near)

    def far_body(u, c):
        pair(u, False)
        return c

    def near_body(u, c):
        pair(u, True)
        return c

    far_pairs = n_far // 2
    all_pairs = (nk + 1) // 2
    qk(0, 0)
    lax.fori_loop(0, far_pairs, far_body, 0)
    lax.fori_loop(far_pairs, all_pairs, near_body, 0)
    pv(2 * all_pairs - 1, 1)

    lam = scal_ref[0]
    a1, a2 = acc_ref[0], acc_ref[1]
    o = a1[:, :HEAD_W] / a1[:, HEAD_W:] - lam * (a2[:, :HEAD_W] / a2[:, HEAD_W:])
    ms = jnp.mean(o * o, axis=-1, keepdims=True)
    o = o * lax.rsqrt(ms + RMS_EPS) * gain_ref[...] * scal_ref[1]
    o_ref[0] = (o * _silu(gate_ref[0])).astype(o_ref.dtype)


def _diff_attn(pb, pf, bias_tab, scal, gain, bsz, t):
    nh = BRANCH_W // HEAD_W
    nq = t // B_TQ
    return pl.pallas_call(
        _diff_attn_kernel,
        grid=(bsz, nh, nq),
        in_specs=[pl.BlockSpec(memory_space=pltpu.SMEM),
                  pl.BlockSpec((1, B_TQ, HEAD_W), lambda b, h, i: (b, i, h)),
                  pl.BlockSpec((1, t, HEAD_W), lambda b, h, i: (b, 0, nh + h)),
                  pl.BlockSpec((1, t, HEAD_W), lambda b, h, i: (b, 0, 2 * nh + h)),
                  pl.BlockSpec((1, B_TQ, HEAD_W), lambda b, h, i: (b, i, PF_BGATE * nh + h)),
                  pl.BlockSpec((1, B_NEAR + 2, B_TQ, B_TQ), lambda b, h, i: (h, 0, 0, 0)),
                  pl.BlockSpec((1, HEAD_W), lambda b, h, i: (0, 0))],
        out_specs=pl.BlockSpec((1, B_TQ, HEAD_W), lambda b, h, i: (b, i, h)),
        out_shape=jax.ShapeDtypeStruct((bsz, t, BRANCH_W), BF16),
        scratch_shapes=[pltpu.VMEM((t // B_TK, HEAD_W, B_TK), BF16),
                        pltpu.VMEM((t, 2 * HEAD_W), BF16),
                        pltpu.VMEM((2, B_TQ, B_TK), F32),
                        pltpu.VMEM((2, B_TQ, B_TK), F32),
                        pltpu.VMEM((2, B_TQ, B_TK), BF16),
                        pltpu.VMEM((2, B_TQ, B_TK), BF16),
                        pltpu.VMEM((2, B_TQ, HEAD_W), F32),
                        pltpu.VMEM((2, B_TQ, HEAD_W), F32),
                        pltpu.VMEM((2, B_TQ, 2 * HEAD_W), F32),
                        pltpu.VMEM((2, B_TQ, HEAD_W), F32)],
        compiler_params=_cparams(("arbitrary", "arbitrary", "arbitrary")),
        name="diff_attn",
    )(scal, pb, pb, pb, pf, bias_tab, gain.reshape(1, HEAD_W).astype(F32))


def _delta_kernel(qkv_ref, z_ref, small_ref, conv_ref, par_ref, gain_ref, o_ref,
                  xe_ref, s_ref):
    tc = qkv_ref.shape[1]
    nchunk = tc // CHUNK

    @pl.when(pl.program_id(1) == 0)
    def _():
        xe_ref[0:8, :] = jnp.zeros((8, xe_ref.shape[1]), F32)
        s_ref[...] = jnp.zeros_like(s_ref)

    xe_ref[8:8 + tc, :] = qkv_ref[0]
    conv = conv_ref[0:1, :] * xe_ref[pl.ds(8 - 3, tc), :]
    for kk in range(1, CONV_K):
        conv = conv + conv_ref[kk:kk + 1, :] * xe_ref[pl.ds(8 - 3 + kk, tc), :]
    xe_ref[0:8, :] = xe_ref[tc:tc + 8, :]
    c = _silu(conv)

    small = small_ref[0]
    beta_all = _sigmoid(small)
    xa = small + par_ref[1:2, :]
    softplus = jnp.maximum(xa, 0.0) + _log1p_exp_neg_abs(xa)
    g_all = -jnp.exp(par_ref[0:1, :]) * softplus

    ri = lax.broadcasted_iota(jnp.int32, (CHUNK, CHUNK), 0)
    ci = lax.broadcasted_iota(jnp.int32, (CHUNK, CHUNK), 1)
    tri = ri >= ci
    strict = ri > ci
    eye = (ri == ci).astype(F32)
    rt = lax.broadcasted_iota(jnp.int32, (tc, tc), 0)
    ct = lax.broadcasted_iota(jnp.int32, (tc, tc), 1)
    tri_blocks = jnp.logical_and(rt >= ct, rt // CHUNK == ct // CHUNK).astype(F32)
    sel = (lax.broadcasted_iota(jnp.int32, (8, HEAD_W), 1)
           == lax.broadcasted_iota(jnp.int32, (8, HEAD_W), 0) + 4).astype(F32)
    nh = BRANCH_W // HEAD_W
    units = [(ch, h) for ch in range(nchunk) for h in range(nh)]

    gcum_all = _nn(tri_blocks, g_all, HI)
    grow_all = _nt(sel, gcum_all, HI)

    qs, ks, vs, bcols, gcols, glasts, gammas, xs = {}, {}, {}, {}, {}, {}, {}, {}
    for h in range(nh):
        qf = c[:, h * HEAD_W:(h + 1) * HEAD_W]
        kf = c[:, BRANCH_W + h * HEAD_W:BRANCH_W + (h + 1) * HEAD_W]
        qf = qf * lax.rsqrt(jnp.sum(qf * qf, axis=-1, keepdims=True) + 1e-6) * (HEAD_W ** -0.5)
        kf = kf * lax.rsqrt(jnp.sum(kf * kf, axis=-1, keepdims=True) + 1e-6)
        for ch in range(nchunk):
            r0 = ch * CHUNK
            u_ = (ch, h)
            qs[u_] = qf[r0:r0 + CHUNK]
            ks[u_] = kf[r0:r0 + CHUNK]
            vs[u_] = c[r0:r0 + CHUNK, 2 * BRANCH_W + h * HEAD_W:2 * BRANCH_W + (h + 1) * HEAD_W]
            bcols[u_] = beta_all[r0:r0 + CHUNK, h:h + 1]
            gcols[u_] = gcum_all[r0:r0 + CHUNK, 4 + h:5 + h]
            glasts[u_] = gcum_all[r0 + CHUNK - 1:r0 + CHUNK, 4 + h:5 + h]
            grow = grow_all[h:h + 1, r0:r0 + CHUNK]
            gammas[u_] = jnp.exp(jnp.where(tri, gcols[u_] - grow, NEG))
    for u_ in units:
        kkt = _nt_acc(ks[u_], ks[u_])
        xs[u_] = jnp.where(strict, -(bcols[u_] * kkt * gammas[u_]), 0.0)

    tinv = {u_: eye + xs[u_] for u_ in units}
    pw = xs
    for _ in range(5):
        pw_split = {u_: _split_rows(pw[u_]) for u_ in units}
        pw = {u_: _nn(_split_cols(pw[u_]), pw_split[u_]) for u_ in units}
        pw_split = {u_: _split_rows(pw[u_]) for u_ in units}
        tinv = {u_: tinv[u_] + _nn(_split_cols(tinv[u_]), pw_split[u_]) for u_ in units}

    us, ws, aqks, qds, kds = {}, {}, {}, {}, {}
    for u_ in units:
        egc = jnp.exp(gcols[u_])
        rhs = jnp.concatenate([vs[u_] * bcols[u_], ks[u_] * (bcols[u_] * egc)], axis=1)
        uw = _nn(_split_cols(tinv[u_]), _split_rows(rhs))
        us[u_], ws[u_] = uw[:, :HEAD_W], uw[:, HEAD_W:]
        aqks[u_] = _nt(qs[u_].astype(BF16), ks[u_].astype(BF16)) * gammas[u_]
        qds[u_] = (qs[u_] * egc).astype(BF16)
        kds[u_] = ks[u_] * jnp.exp(glasts[u_] - gcols[u_])

    for ch in range(nchunk):
        r0 = ch * CHUNK
        for h in range(nh):
            u_ = (ch, h)
            s = s_ref[h]
            s_hi = s.astype(BF16)
            s_lo = (s - s_hi.astype(F32)).astype(BF16)
            w_hi = ws[u_].astype(BF16)
            w_lo = (ws[u_] - w_hi.astype(F32)).astype(BF16)
            ws_prod = (_nn(jnp.concatenate([w_hi, w_lo], axis=1), jnp.concatenate([s_hi, s_hi], axis=0))
                       + _nn(w_hi, s_lo))
            v_new = us[u_] - ws_prod
            o = _nn(qds[u_], s_hi) + _nn(aqks[u_].astype(BF16), v_new.astype(BF16))
            s_ref[h] = jnp.exp(glasts[u_]) * s + _tn(_split_rows_lhs(kds[u_]), _split_rows(v_new))
            ms = jnp.mean(o * o, axis=-1, keepdims=True)
            o = o * lax.rsqrt(ms + RMS_EPS) * gain_ref[...]
            zg = z_ref[0, r0:r0 + CHUNK, h * HEAD_W:(h + 1) * HEAD_W]
            o_ref[0, r0:r0 + CHUNK, h * HEAD_W:(h + 1) * HEAD_W] = (o * _silu(zg)).astype(o_ref.dtype)


def _delta_net(pf, small, conv_w, par, gain, bsz, t, tc=CHUNK):
    cw = 3 * BRANCH_W
    return pl.pallas_call(
        _delta_kernel,
        grid=(bsz, t // tc),
        in_specs=[pl.BlockSpec((1, tc, cw), lambda b, i: (b, i, 0)),
                  pl.BlockSpec((1, tc, BRANCH_W), lambda b, i: (b, i, PF_CZ)),
                  pl.BlockSpec((1, tc, HEAD_W), lambda b, i: (b, i, 0)),
                  pl.BlockSpec((8, cw), lambda b, i: (0, 0)),
                  pl.BlockSpec((8, HEAD_W), lambda b, i: (0, 0)),
                  pl.BlockSpec((1, HEAD_W), lambda b, i: (0, 0))],
        out_specs=pl.BlockSpec((1, tc, BRANCH_W), lambda b, i: (b, i, 0)),
        out_shape=jax.ShapeDtypeStruct((bsz, t, BRANCH_W), BF16),
        scratch_shapes=[pltpu.VMEM((tc + 8, cw), F32),
                        pltpu.VMEM((BRANCH_W // HEAD_W, HEAD_W, HEAD_W), F32)],
        compiler_params=_cparams(("arbitrary", "arbitrary")),
        name="delta_net",
    )(pf, pf, small, conv_w, par, gain.reshape(1, HEAD_W).astype(F32))


def _hgrn_kernel(q_ref, f_ref, i_ref, gate_ref, lb_ref, gain_ref, o_ref, st_ref):
    tc = q_ref.shape[1]
    nchunk = tc // CHUNK
    nh = BRANCH_W // HEAD_W
    nsub = CHUNK // SUB

    @pl.when(pl.program_id(1) == 0)
    def _():
        st_ref[...] = jnp.zeros_like(st_ref)

    lb = lb_ref[...]
    df = f_ref[0]
    log_sig = jnp.minimum(df, 0.0) - _log1p_exp_neg_abs(df)
    a = jnp.log(lb)
    b = jnp.log1p(-lb) + log_sig
    logf_all = jnp.maximum(a, b) + _log1p_exp_neg_abs(a - b)
    k_all = (1.0 - lb) * _sigmoid(-df)

    ri = lax.broadcasted_iota(jnp.int32, (CHUNK, 3 * CHUNK), 0)
    ci = lax.broadcasted_iota(jnp.int32, (CHUNK, 3 * CHUNK), 1)
    tri3 = jnp.where(ri >= ci % CHUNK, 1.0, 0.0).astype(BF16)
    sub_row = lax.broadcasted_iota(jnp.int32, (SUB, HEAD_W), 0)
    units = [(ch, h) for ch in range(nchunk) for h in range(nh)]

    bcs = []
    for ch in range(nchunk):
        lf = logf_all[ch * CHUNK:(ch + 1) * CHUNK]
        hi = lf.astype(BF16)
        r1 = lf - hi.astype(F32)
        mid = r1.astype(BF16)
        lo = (r1 - mid.astype(F32)).astype(BF16)
        bcs.append(_nn(tri3, jnp.concatenate([hi, mid, lo], axis=0)))

    qs, ks, vs, bcu, bls, qes, dstate, intra = {}, {}, {}, {}, {}, {}, {}, {}
    for u_ in units:
        ch, h = u_
        r0 = ch * CHUNK
        cs = slice(h * HEAD_W, (h + 1) * HEAD_W)
        qs[u_] = q_ref[0, r0:r0 + CHUNK, cs]
        vs[u_] = i_ref[0, r0:r0 + CHUNK, cs]
        ks[u_] = k_all[r0:r0 + CHUNK, cs]
        bcu[u_] = bcs[ch][:, cs]
        bls[u_] = bcu[u_][CHUNK - 1:CHUNK, :]
        qes[u_] = (qs[u_] * jnp.exp(bcu[u_])).astype(BF16)
        k_end = ks[u_] * jnp.exp(bls[u_] - bcu[u_])
        dstate[u_] = _tn(_split_rows_lhs(vs[u_]), _split_rows(k_end))

    for u_ in units:
        q, k, v, bc = qs[u_], ks[u_], vs[u_], bcu[u_]
        pieces = []
        for si in range(nsub):
            rs = slice(si * SUB, (si + 1) * SUB)
            q_s, bc_s = q[rs], bc[rs]
            if si > 0:
                ref_row = bc[si * SUB - 1:si * SUB, :]
                q_dec = (q_s * jnp.exp(bc_s - ref_row)).astype(BF16)
                k_dec = (k[:si * SUB] * jnp.exp(ref_row - bc[:si * SUB])).astype(BF16)
                o_s = _nn(_nt(q_dec, k_dec).astype(BF16), v[:si * SUB].astype(BF16))
            else:
                o_s = jnp.zeros((SUB, HEAD_W), F32)
            for j in range(SUB):
                jj = si * SUB + j
                e = jnp.exp(jnp.where(sub_row >= j, bc_s - bc[jj:jj + 1, :], NEG))
                a_col = jnp.sum(q_s * k[jj:jj + 1, :] * e, axis=-1, keepdims=True)
                o_s = o_s + a_col * v[jj:jj + 1, :]
            pieces.append(o_s)
        intra[u_] = jnp.concatenate(pieces, axis=0)

    for u_ in units:
        ch, h = u_
        r0 = ch * CHUNK
        cs = slice(h * HEAD_W, (h + 1) * HEAD_W)
        st = st_ref[h]
        o = _nt(qes[u_], st.astype(BF16)) + intra[u_]
        st_ref[h] = st * jnp.exp(bls[u_]) + dstate[u_]
        ms = jnp.mean(o * o, axis=-1, keepdims=True)
        o = o * lax.rsqrt(ms + RMS_EPS) * gain_ref[...]
        o_ref[0, r0:r0 + CHUNK, cs] = (o * _silu(gate_ref[0, r0:r0 + CHUNK, cs])).astype(o_ref.dtype)


def _hgrn(pf, lb, gain, bsz, t, tc=CHUNK):
    def spec(cb):
        return pl.BlockSpec((1, tc, BRANCH_W), lambda b, i: (b, i, cb))

    return pl.pallas_call(
        _hgrn_kernel,
        grid=(bsz, t // tc),
        in_specs=[spec(PF_DQ), spec(PF_DF), spec(PF_DI), spec(PF_DGATE),
                  pl.BlockSpec((1, BRANCH_W), lambda b, i: (0, 0)),
                  pl.BlockSpec((1, HEAD_W), lambda b, i: (0, 0))],
        out_specs=pl.BlockSpec((1, tc, BRANCH_W), lambda b, i: (b, i, 0)),
        out_shape=jax.ShapeDtypeStruct((bsz, t, BRANCH_W), BF16),
        scratch_shapes=[pltpu.VMEM((BRANCH_W // HEAD_W, HEAD_W, HEAD_W), F32)],
        compiler_params=_cparams(("arbitrary", "arbitrary")),
        name="hgrn2",
    )(pf, pf, pf, pf, lb.reshape(1, BRANCH_W), gain.reshape(1, HEAD_W).astype(F32))


def _split_w_in(w):
    o = 0
    parts = {}
    for name, width in (("a_q", 512), ("a_k", 512), ("a_v", 512), ("a_gate", 512),
                        ("b_q", 512), ("b_k", 512), ("b_v", 512), ("b_gate", 512),
                        ("c_qkv", 1536), ("c_z", 512), ("c_beta", 4), ("c_a", 4),
                        ("d_q", 512), ("d_f", 512), ("d_i", 512), ("d_gate", 512),
                        ("merge", 4 * D_MODEL)):
        parts[name] = w[:, o:o + width]
        o += width
    w_f = jnp.concatenate([parts[k] for k in ("c_qkv", "c_z", "a_gate", "b_gate", "d_q", "d_f",
                                              "d_i", "d_gate", "a_q", "a_k", "a_v")], axis=1)
    w_b = jnp.concatenate([parts[k] for k in ("b_q", "b_k", "b_v")], axis=1)
    w_s = jnp.concatenate([parts["c_beta"], parts["c_a"],
                           jnp.zeros((w.shape[0], HEAD_W - 8), w.dtype)], axis=1)
    return w_f.astype(BF16), w_b.astype(BF16), w_s.astype(BF16), parts["merge"].astype(BF16)


def kernel(x, norm_gain, w_in, rel_bias, diff_lambda, diff_subln_gain, dn_conv, dn_a_log, dn_dt_bias,
           dn_norm_gain, hg_lb_logits, hg_norm_gain, w_branch, w_out, final_gain):
    bsz, t, d = x.shape
    n = bsz * t
    depth = w_in.shape[0]
    lb_p = jax.nn.softmax(hg_lb_logits.astype(F32), axis=0)
    hg_lb = jnp.clip(jnp.cumsum(lb_p, axis=0) - lb_p[0], 0.0, 1.0)
    bias_a = _dilated_bias(rel_bias[:, :8].astype(F32))
    bias_b = _diff_bias(rel_bias[:, 8:].astype(F32))
    cfar = rel_bias[NUM_BUCKETS - 1, 8:].astype(F32)

    xf = x.reshape(n, d).astype(F32)
    for layer in range(depth):
        w_f, w_b, w_s, w_g = _split_w_in(w_in[layer])
        h = _rmsnorm(xf, norm_gain[layer], BF16)
        pf = _mm(h, w_f, F32, tn=512, name="proj_f32").reshape(bsz, t, PF_COLS)
        pb = _mm(h, w_b, BF16, tn=512, name="proj_bf16").reshape(bsz, t, 3 * BRANCH_W)
        ps = _mm(h, w_s, F32, name="proj_small").reshape(bsz, t, HEAD_W)
        gates = _mm(h, w_g, BF16, act="sigmoid", name="proj_gates")

        y_a = _dil_attn(pf, bias_a, bsz, t)

        lam_init = 0.8 - 0.6 * math.exp(-0.3 * layer)
        lq1, lk1, lq2, lk2 = diff_lambda[layer].astype(F32)
        lam = jnp.exp(jnp.sum(lq1 * lk1)) - jnp.exp(jnp.sum(lq2 * lk2)) + lam_init
        scal = jnp.concatenate([jnp.stack([lam, jnp.asarray(1.0 - lam_init, F32)]), cfar,
                                jnp.zeros((2,), F32)])
        y_b = _diff_attn(pb, pf, bias_b, scal, diff_subln_gain[layer], bsz, t)

        conv_w = jnp.concatenate([dn_conv[layer].astype(F32),
                                  jnp.zeros((8 - CONV_K, 3 * BRANCH_W), F32)], axis=0)
        par = jnp.zeros((8, HEAD_W), F32)
        par = par.at[0, 4:8].set(dn_a_log[layer].astype(F32)).at[1, 4:8].set(dn_dt_bias[layer].astype(F32))
        y_c = _delta_net(pf, ps, conv_w, par, dn_norm_gain[layer], bsz, t, tc=DELTA_TILE)

        y_d = _hgrn(pf, hg_lb[layer], hg_norm_gain[layer], bsz, t, tc=HGRN_TILE)

        ys = [y.reshape(n, BRANCH_W) for y in (y_a, y_b, y_c, y_d)]
        mixed = _merge(gates, ys, w_branch[layer].astype(BF16))
        xf = _mm(mixed, w_out[layer].astype(BF16), F32, residual=xf, name="out_proj")
    out = _rmsnorm(xf, final_gain, F32)
    return out.reshape(bsz, t, d).astype(x.dtype)
```

```python
import functools
import math

import jax
import jax.numpy as jnp
import numpy as np
from jax import lax
from jax.experimental import pallas as pl
from jax.experimental.pallas import tpu as pltpu

F32 = jnp.float32
BF16 = jnp.bfloat16
HI = lax.Precision.HIGHEST

D_MODEL = 2048
BRANCH_W = 512
HEAD_W = 128
A_HEAD_DIM = 64
DILATIONS = (1, 4, 16)
BAND = 128
A_TILE = BAND * DILATIONS[-1]
A_GROUP = 4
DIFF_QK = 64
B_BT = 256
B_TQ = 512
B_TK = 512
B_NEAR = 2048 // B_BT + 1
CHUNK = 64
DELTA_TILE = 256
HGRN_TILE = 256
SUB = 16
CONV_K = 4
NUM_BUCKETS = 32
MAX_DISTANCE = 2048
RMS_EPS = 1e-6
NEG = -1e30
VMEM_LIMIT = 56 * 1024 * 1024
MM_TM, MM_TN = 2048, 512
MERGE_TM = 256

PF_CQKV, PF_CZ, PF_AGATE, PF_BGATE, PF_DQ, PF_DF, PF_DI, PF_DGATE, PF_AQ, PF_AK, PF_AV = 0, 3, 4, 5, 6, 7, 8, 9, 10, 11, 12
PF_COLS = 13 * 512


def _cparams(sem):
    return pltpu.CompilerParams(dimension_semantics=sem, vmem_limit_bytes=VMEM_LIMIT)


def _nt(a, b, precision=None):
    return lax.dot_general(a, b, (((1,), (1,)), ((), ())), precision=precision,
                           preferred_element_type=F32)


def _tn(a, b, precision=None):
    return lax.dot_general(a, b, (((0,), (0,)), ((), ())), precision=precision,
                           preferred_element_type=F32)


def _nn(a, b, precision=None):
    return jnp.dot(a, b, precision=precision, preferred_element_type=F32)


def _sigmoid(x):
    return 1.0 / (1.0 + jnp.exp(-x))


def _silu(x):
    return x * _sigmoid(x)


def _log1p_exp_neg_abs(x):
    return jnp.log1p(jnp.exp(-jnp.abs(x)))


def _hi_lo(x):
    hi = x.astype(BF16)
    return hi, (x - hi.astype(F32)).astype(BF16)


def _split_cols(a):
    hi = a.astype(BF16).astype(F32)
    lo = a - hi
    return jnp.concatenate([hi, lo, hi, lo], axis=1).astype(BF16)


def _split_rows(b):
    hi, lo = _hi_lo(b)
    return jnp.concatenate([hi, hi, lo, lo], axis=0)


def _split_rows_lhs(a):
    hi, lo = _hi_lo(a)
    return jnp.concatenate([hi, lo, hi, lo], axis=0)


def _nt_acc(a, b):
    ah, al = _hi_lo(a)
    bh, bl = _hi_lo(b)
    return (_nt(jnp.concatenate([ah, al], axis=1), jnp.concatenate([bh, bh], axis=1))
            + _nt(ah, bl))


def _rmsnorm_kernel(x_ref, g_ref, o_ref):
    x = x_ref[...]
    ms = jnp.mean(x * x, axis=-1, keepdims=True)
    o_ref[...] = (x * lax.rsqrt(ms + RMS_EPS) * g_ref[...]).astype(o_ref.dtype)


def _rmsnorm(x, gain, out_dtype, tm=512):
    n, d = x.shape
    return pl.pallas_call(
        _rmsnorm_kernel,
        grid=(n // tm,),
        in_specs=[pl.BlockSpec((tm, d), lambda i: (i, 0)),
                  pl.BlockSpec((1, d), lambda i: (0, 0))],
        out_specs=pl.BlockSpec((tm, d), lambda i: (i, 0)),
        out_shape=jax.ShapeDtypeStruct((n, d), out_dtype),
        compiler_params=_cparams(("parallel",)),
        name="rmsnorm",
    )(x, gain.reshape(1, d).astype(F32))


def _mm_kernel(a_ref, w_ref, o_ref, *, act):
    acc = _nn(a_ref[...], w_ref[...])
    if act == "sigmoid":
        acc = _sigmoid(acc)
    o_ref[...] = acc.astype(o_ref.dtype)


def _mm(a, w, out_dtype, act=None, tm=MM_TM, tn=MM_TN, name="mm"):
    n, k = a.shape
    c = w.shape[1]
    tn = min(tn, c)
    return pl.pallas_call(
        functools.partial(_mm_kernel, act=act),
        grid=(n // tm, c // tn),
        in_specs=[pl.BlockSpec((tm, k), lambda i, j: (i, 0)),
                  pl.BlockSpec((k, tn), lambda i, j: (0, j))],
        out_specs=pl.BlockSpec((tm, tn), lambda i, j: (i, j)),
        out_shape=jax.ShapeDtypeStruct((n, c), out_dtype),
        compiler_params=_cparams(("parallel", "parallel")),
        name=name,
    )(a, w)


def _merge_out_kernel(g0, g1, g2, g3, y0, y1, y2, y3, wbr_ref, wout_ref, x_ref, o_ref):
    mixed = g0[...].astype(F32) * _nn(y0[...], wbr_ref[0])
    mixed += g1[...].astype(F32) * _nn(y1[...], wbr_ref[1])
    mixed += g2[...].astype(F32) * _nn(y2[...], wbr_ref[2])
    mixed += g3[...].astype(F32) * _nn(y3[...], wbr_ref[3])
    o_ref[...] = x_ref[...] + _nn(mixed.astype(BF16), wout_ref[...])


def _merge_out(gates, ys, w_br, w_out, x, tm=MERGE_TM):
    n = gates.shape[0]
    gate_specs = [pl.BlockSpec((tm, D_MODEL), functools.partial(lambda i, b: (i, b), b=b))
                  for b in range(4)]
    y_specs = [pl.BlockSpec((tm, BRANCH_W), lambda i: (i, 0)) for _ in range(4)]
    resident = dict(pipeline_mode=pl.Buffered(1))
    return pl.pallas_call(
        _merge_out_kernel,
        grid=(n // tm,),
        in_specs=gate_specs + y_specs + [
            pl.BlockSpec((4, BRANCH_W, D_MODEL), lambda i: (0, 0, 0), **resident),
            pl.BlockSpec((D_MODEL, D_MODEL), lambda i: (0, 0), **resident),
            pl.BlockSpec((tm, D_MODEL), lambda i: (i, 0))],
        out_specs=pl.BlockSpec((tm, D_MODEL), lambda i: (i, 0)),
        out_shape=jax.ShapeDtypeStruct((n, D_MODEL), F32),
        compiler_params=_cparams(("parallel",)),
        name="merge_out",
    )(gates, gates, gates, gates, *ys, w_br, w_out, x)


def _t5_bucket(dist):
    n = jnp.maximum(dist, 0)
    max_exact = NUM_BUCKETS // 2
    nf = jnp.maximum(n, max_exact).astype(F32)
    large = max_exact + (jnp.log(nf / max_exact) / math.log(MAX_DISTANCE / max_exact)
                         * (NUM_BUCKETS - max_exact)).astype(jnp.int32)
    large = jnp.minimum(large, NUM_BUCKETS - 1)
    return jnp.where(n < max_exact, n, large)


def _bucket_lookup(table, bucket):
    tab = table.T.reshape((table.shape[1],) + (1,) * bucket.ndim + (NUM_BUCKETS,))
    out = jnp.zeros((table.shape[1],) + bucket.shape, F32)
    for b in range(NUM_BUCKETS):
        out = jnp.where(bucket[None] == b, tab[..., b], out)
    return out


def _dilated_bias(bias_a):
    qi = jnp.arange(BAND)[:, None]
    kj = jnp.arange(2 * BAND)[None, :]
    rel = qi + BAND - kj
    valid = (rel >= 0) & (rel <= BAND)
    tabs = []
    for dil in DILATIONS:
        b = _bucket_lookup(bias_a, _t5_bucket(rel * dil))
        tabs.append(jnp.where(valid[None], b, NEG))
    return jnp.stack(tabs)


def _diff_bias(bias_b):
    qi = jnp.arange(B_BT)[:, None]
    kj = jnp.arange(B_BT)[None, :]
    tabs = []
    for d in range(-1, B_NEAR + 1):
        rel = d * B_BT + qi - kj
        b = _bucket_lookup(bias_b, _t5_bucket(rel))
        tabs.append(jnp.where((rel >= 0)[None], b, NEG))
    return jnp.stack(tabs, axis=1)


def _dil_attn_kernel(q_ref, kp_ref, kc_ref, vp_ref, vc_ref, gate_ref, bias_ref, o_ref,
                     kcat, vcat, acc_ref, m_ref, l_ref):
    n = pl.program_id(1)
    kcat[0:A_TILE, :] = kp_ref[0]
    kcat[A_TILE:2 * A_TILE, :] = kc_ref[0]
    vcat[0:A_TILE, :] = vp_ref[0]
    vcat[A_TILE:2 * A_TILE, :] = vc_ref[0]
    lane = lax.broadcasted_iota(jnp.int32, (BAND, HEAD_W), 1)
    lo = lane < A_HEAD_DIM
    col = lax.broadcasted_iota(jnp.int32, (BAND, 2 * BAND), 1)
    nblk_tile = A_TILE // BAND

    for p, r in enumerate(DILATIONS):
        def rows(start, r=r):
            return pl.ds(start, BAND, stride=r) if r > 1 else pl.ds(pl.multiple_of(start, BAND), BAND)

        def body(it, carry, p=p, r=r, rows=rows):
            starts, no_prevs, vvs = [], [], []
            ss, es, ms, ls, outs = {}, {}, {}, {}, {}
            for g in range(A_GROUP):
                idx = it * A_GROUP + g
                j = idx // r
                start = j * (BAND * r) + idx % r
                starts.append(start)
                no_prevs.append(jnp.logical_and(n == 0, j == 0))
                q = (q_ref[0, rows(start), :] * (A_HEAD_DIM ** -0.5)).astype(BF16)
                kk = jnp.concatenate([kcat[rows(A_TILE + start - BAND * r), :],
                                      kcat[rows(A_TILE + start), :]], axis=0).astype(BF16)
                vvs.append(jnp.concatenate([vcat[rows(A_TILE + start - BAND * r), :],
                                            vcat[rows(A_TILE + start), :]], axis=0).astype(BF16))
                for hh in range(2):
                    qz = jnp.where(lo if hh == 0 else jnp.logical_not(lo), q, jnp.zeros_like(q))
                    ss[g, hh] = _nt(qz, kk)
            for g in range(A_GROUP):
                for hh in range(2):
                    s = ss[g, hh] + bias_ref[p, hh]
                    s = jnp.where(jnp.logical_and(no_prevs[g], col < BAND), NEG, s)
                    m = jnp.max(s, axis=-1, keepdims=True)
                    e = jnp.exp(s - m)
                    ms[g, hh] = jnp.broadcast_to(m, (BAND, HEAD_W))
                    ls[g, hh] = jnp.broadcast_to(jnp.sum(e, axis=-1, keepdims=True), (BAND, HEAD_W))
                    es[g, hh] = e.astype(BF16)
            for g in range(A_GROUP):
                for hh in range(2):
                    outs[g, hh] = _nn(es[g, hh], vvs[g])
            merged = []
            for g in range(A_GROUP):
                o_new = jnp.where(lo, outs[g, 0], outs[g, 1])
                m_new = jnp.where(lo, ms[g, 0], ms[g, 1])
                l_new = jnp.where(lo, ls[g, 0], ls[g, 1])
                if p > 0:
                    rws = rows(starts[g])
                    m_old = m_ref[rws, :]
                    m_tot = jnp.maximum(m_old, m_new)
                    a = jnp.exp(m_old - m_tot)
                    b = jnp.exp(m_new - m_tot)
                    o_new = a * acc_ref[rws, :] + b * o_new
                    l_new = a * l_ref[rws, :] + b * l_new
                    m_new = m_tot
                merged.append((o_new, m_new, l_new))
            for g in range(A_GROUP):
                rws = rows(starts[g])
                acc_ref[rws, :], m_ref[rws, :], l_ref[rws, :] = merged[g]
            return carry

        lax.fori_loop(0, nblk_tile // A_GROUP, body, 0)

    o_ref[0] = (acc_ref[...] / l_ref[...] * _silu(gate_ref[0])).astype(o_ref.dtype)


def _dil_attn(pf, bias_tab, bsz, t):
    nt = t // A_TILE
    hp = BRANCH_W // HEAD_W
    blk = (1, A_TILE, HEAD_W)

    def cur(cb):
        return pl.BlockSpec(blk, lambda b, n, h: (b, n, cb * hp + h))

    def prev(cb):
        return pl.BlockSpec(blk, lambda b, n, h: (b, jnp.maximum(n - 1, 0), cb * hp + h))

    return pl.pallas_call(
        _dil_attn_kernel,
        grid=(bsz, nt, hp),
        in_specs=[cur(PF_AQ), prev(PF_AK), cur(PF_AK), prev(PF_AV), cur(PF_AV), cur(PF_AGATE),
                  pl.BlockSpec((3, 2, BAND, 2 * BAND), lambda b, n, h: (0, h, 0, 0))],
        out_specs=pl.BlockSpec(blk, lambda b, n, h: (b, n, h)),
        out_shape=jax.ShapeDtypeStruct((bsz, t, BRANCH_W), BF16),
        scratch_shapes=[pltpu.VMEM((2 * A_TILE, HEAD_W), F32), pltpu.VMEM((2 * A_TILE, HEAD_W), F32),
                        pltpu.VMEM((A_TILE, HEAD_W), F32), pltpu.VMEM((A_TILE, HEAD_W), F32),
                        pltpu.VMEM((A_TILE, HEAD_W), F32)],
        compiler_params=_cparams(("parallel", "parallel", "parallel")),
        name="dilated_attn",
    )(pf, pf, pf, pf, pf, pf, bias_tab)


def _diff_attn_kernel(scal_ref, q_ref, k_ref, v_ref, gate_ref, bias_ref, gain_ref, o_ref,
                      kt_ref, vx_ref, s0_ref, s1_ref, p0_ref, p1_ref, al0_ref, al1_ref,
                      acc_ref, m_ref):
    h = pl.program_id(1)
    qi = pl.program_id(2)
    nkb = kt_ref.shape[0]
    nrh = B_TQ // B_BT
    nsub = B_TK // B_BT
    ncol = B_TK // HEAD_W

    @pl.when(qi == 0)
    def _():
        def tr(kb, c):
            rows = pl.ds(pl.multiple_of(kb * B_TK, B_TK), B_TK)
            kt_ref[kb] = k_ref[0, rows, :].astype(F32).T.astype(BF16)
            vx_ref[rows, 0:HEAD_W] = v_ref[0, rows, :]
            vx_ref[rows, HEAD_W:2 * HEAD_W] = jnp.ones((B_TK, HEAD_W), BF16)
            return c
        lax.fori_loop(0, nkb, tr, 0)

    lane = lax.broadcasted_iota(jnp.int32, (B_TQ, HEAD_W), 1)
    lo = lane < DIFF_QK
    q = q_ref[0] * (DIFF_QK ** -0.5)
    zero = jnp.zeros_like(q)
    qz = (jnp.where(lo, q, zero), jnp.where(lo, zero, q))
    cfar = scal_ref[2 + h]
    nk = ((qi + 1) * nrh - 1) // nsub + 1
    n_far = jnp.maximum((nrh * qi - B_NEAR + 1) // nsub, 0)
    per_sub = B_BT // HEAD_W

    s_bufs, p_bufs, al_bufs = (s0_ref, s1_ref), (p0_ref, p1_ref), (al0_ref, al1_ref)
    acc_ref[...] = jnp.zeros_like(acc_ref)
    m_ref[...] = jnp.full_like(m_ref, NEG)
    al1_ref[...] = jnp.ones_like(al1_ref)
    p1_ref[...] = jnp.zeros_like(p1_ref)

    row_halves = [slice(rh * B_BT, (rh + 1) * B_BT) for rh in range(nrh)]

    def qk(kb, par):
        kt = kt_ref[jnp.minimum(kb, nk - 1)]
        for mi in range(2):
            for rows in row_halves:
                s_bufs[par][mi, rows, :] = _nn(qz[mi][rows], kt)

    def pv(kb, par):
        kb = jnp.clip(kb, 0, nk - 1)
        vx = vx_ref[pl.ds(pl.multiple_of(kb * B_TK, B_TK), B_TK), :]
        for mi in range(2):
            for rows in row_halves:
                a = al_bufs[par][mi, rows]
                upd = _nn(p_bufs[par][mi, rows, :], vx)
                acc_ref[mi, rows] = jnp.concatenate([a, a], axis=1) * acc_ref[mi, rows] + upd

    def col(par, mi, c, kb, near):
        x = s_bufs[par][mi, :, c * HEAD_W:(c + 1) * HEAD_W]
        if near:
            cc = (c % per_sub) * HEAD_W
            tiles = [jnp.clip(nrh * qi + rh - nsub * kb - c // per_sub, -1, B_NEAR) + 1
                     for rh in range(nrh)]
            x = x + jnp.concatenate([bias_ref[0, tl, :, cc:cc + HEAD_W] for tl in tiles], axis=0)
        return x

    def softmax(kb, par, near):
        shift = 0.0 if near else cfar
        m_sub = []
        for mi in range(2):
            mx = col(par, mi, 0, kb, near)
            for c in range(1, ncol):
                mx = jnp.maximum(mx, col(par, mi, c, kb, near))
            m_prev = m_ref[mi]
            m_next = jnp.maximum(m_prev, jnp.max(mx, axis=-1, keepdims=True) + shift)
            al_bufs[par][mi] = jnp.exp(m_prev - m_next)
            m_ref[mi] = m_next
            m_sub.append(m_next - shift)
        for mi in range(2):
            for c in range(ncol):
                e = jnp.exp(col(par, mi, c, kb, near) - m_sub[mi])
                p_bufs[par][mi, :, c * HEAD_W:(c + 1) * HEAD_W] = e.astype(BF16)

    def pair(u, near):
        for par in range(2):
            t = 2 * u + par
            qk(t + 1, 1 - par)
            pv(t - 1, 1 - par)
            softmax(t, par, near)

    def far_body(u, c):
        pair(u, False)
        return c

    def near_body(u, c):
        pair(u, True)
        return c

    far_pairs = n_far // 2
    all_pairs = (nk + 1) // 2
    qk(0, 0)
    lax.fori_loop(0, far_pairs, far_body, 0)
    lax.fori_loop(far_pairs, all_pairs, near_body, 0)
    pv(2 * all_pairs - 1, 1)

    lam = scal_ref[0]
    a1, a2 = acc_ref[0], acc_ref[1]
    o = a1[:, :HEAD_W] / a1[:, HEAD_W:] - lam * (a2[:, :HEAD_W] / a2[:, HEAD_W:])
    ms = jnp.mean(o * o, axis=-1, keepdims=True)
    o = o * lax.rsqrt(ms + RMS_EPS) * gain_ref[...] * scal_ref[1]
    o_ref[0] = (o * _silu(gate_ref[0])).astype(o_ref.dtype)


def _diff_attn(pb, pf, bias_tab, scal, gain, bsz, t):
    nh = BRANCH_W // HEAD_W
    nq = t // B_TQ
    return pl.pallas_call(
        _diff_attn_kernel,
        grid=(bsz, nh, nq),
        in_specs=[pl.BlockSpec(memory_space=pltpu.SMEM),
                  pl.BlockSpec((1, B_TQ, HEAD_W), lambda b, h, i: (b, i, h)),
                  pl.BlockSpec((1, t, HEAD_W), lambda b, h, i: (b, 0, nh + h)),
                  pl.BlockSpec((1, t, HEAD_W), lambda b, h, i: (b, 0, 2 * nh + h)),
                  pl.BlockSpec((1, B_TQ, HEAD_W), lambda b, h, i: (b, i, PF_BGATE * nh + h)),
                  pl.BlockSpec((1, B_NEAR + 2, B_BT, B_BT), lambda b, h, i: (h, 0, 0, 0)),
                  pl.BlockSpec((1, HEAD_W), lambda b, h, i: (0, 0))],
        out_specs=pl.BlockSpec((1, B_TQ, HEAD_W), lambda b, h, i: (b, i, h)),
        out_shape=jax.ShapeDtypeStruct((bsz, t, BRANCH_W), BF16),
        scratch_shapes=[pltpu.VMEM((t // B_TK, HEAD_W, B_TK), BF16),
                        pltpu.VMEM((t, 2 * HEAD_W), BF16),
                        pltpu.VMEM((2, B_TQ, B_TK), F32),
                        pltpu.VMEM((2, B_TQ, B_TK), F32),
                        pltpu.VMEM((2, B_TQ, B_TK), BF16),
                        pltpu.VMEM((2, B_TQ, B_TK), BF16),
                        pltpu.VMEM((2, B_TQ, HEAD_W), F32),
                        pltpu.VMEM((2, B_TQ, HEAD_W), F32),
                        pltpu.VMEM((2, B_TQ, 2 * HEAD_W), F32),
                        pltpu.VMEM((2, B_TQ, HEAD_W), F32)],
        compiler_params=_cparams(("arbitrary", "arbitrary", "arbitrary")),
        name="diff_attn",
    )(scal, pb, pb, pb, pf, bias_tab, gain.reshape(1, HEAD_W).astype(F32))


def _delta_kernel(qkv_ref, z_ref, small_ref, conv_ref, par_ref, gain_ref, o_ref,
                  xe_ref, s_ref):
    tc = qkv_ref.shape[1]
    nchunk = tc // CHUNK

    @pl.when(pl.program_id(1) == 0)
    def _():
        xe_ref[0:8, :] = jnp.zeros((8, xe_ref.shape[1]), F32)
        s_ref[...] = jnp.zeros_like(s_ref)

    xe_ref[8:8 + tc, :] = qkv_ref[0]
    conv = conv_ref[0:1, :] * xe_ref[pl.ds(8 - 3, tc), :]
    for kk in range(1, CONV_K):
        conv = conv + conv_ref[kk:kk + 1, :] * xe_ref[pl.ds(8 - 3 + kk, tc), :]
    xe_ref[0:8, :] = xe_ref[tc:tc + 8, :]
    c = _silu(conv)

    small = small_ref[0]
    beta_all = _sigmoid(small)
    xa = small + par_ref[1:2, :]
    softplus = jnp.maximum(xa, 0.0) + _log1p_exp_neg_abs(xa)
    g_all = -jnp.exp(par_ref[0:1, :]) * softplus

    ri = lax.broadcasted_iota(jnp.int32, (CHUNK, CHUNK), 0)
    ci = lax.broadcasted_iota(jnp.int32, (CHUNK, CHUNK), 1)
    tri = ri >= ci
    strict = ri > ci
    eye = (ri == ci).astype(F32)
    rt = lax.broadcasted_iota(jnp.int32, (tc, tc), 0)
    ct = lax.broadcasted_iota(jnp.int32, (tc, tc), 1)
    tri_blocks = jnp.logical_and(rt >= ct, rt // CHUNK == ct // CHUNK).astype(F32)
    sel = (lax.broadcasted_iota(jnp.int32, (8, HEAD_W), 1)
           == lax.broadcasted_iota(jnp.int32, (8, HEAD_W), 0) + 4).astype(F32)
    nh = BRANCH_W // HEAD_W
    units = [(ch, h) for ch in range(nchunk) for h in range(nh)]

    gcum_all = _nn(tri_blocks, g_all, HI)
    grow_all = _nt(sel, gcum_all, HI)

    qs, ks, vs, bcols, gcols, glasts, gammas, xs = {}, {}, {}, {}, {}, {}, {}, {}
    for h in range(nh):
        qf = c[:, h * HEAD_W:(h + 1) * HEAD_W]
        kf = c[:, BRANCH_W + h * HEAD_W:BRANCH_W + (h + 1) * HEAD_W]
        qf = qf * lax.rsqrt(jnp.sum(qf * qf, axis=-1, keepdims=True) + 1e-6) * (HEAD_W ** -0.5)
        kf = kf * lax.rsqrt(jnp.sum(kf * kf, axis=-1, keepdims=True) + 1e-6)
        for ch in range(nchunk):
            r0 = ch * CHUNK
            u_ = (ch, h)
            qs[u_] = qf[r0:r0 + CHUNK]
            ks[u_] = kf[r0:r0 + CHUNK]
            vs[u_] = c[r0:r0 + CHUNK, 2 * BRANCH_W + h * HEAD_W:2 * BRANCH_W + (h + 1) * HEAD_W]
            bcols[u_] = beta_all[r0:r0 + CHUNK, h:h + 1]
            gcols[u_] = gcum_all[r0:r0 + CHUNK, 4 + h:5 + h]
            glasts[u_] = gcum_all[r0 + CHUNK - 1:r0 + CHUNK, 4 + h:5 + h]
            grow = grow_all[h:h + 1, r0:r0 + CHUNK]
            gammas[u_] = jnp.exp(jnp.where(tri, gcols[u_] - grow, NEG))
    for u_ in units:
        kkt = _nt_acc(ks[u_], ks[u_])
        xs[u_] = jnp.where(strict, -(bcols[u_] * kkt * gammas[u_]), 0.0)

    tinv = {u_: eye + xs[u_] for u_ in units}
    pw = xs
    for _ in range(5):
        pw_split = {u_: _split_rows(pw[u_]) for u_ in units}
        pw = {u_: _nn(_split_cols(pw[u_]), pw_split[u_]) for u_ in units}
        pw_split = {u_: _split_rows(pw[u_]) for u_ in units}
        tinv = {u_: tinv[u_] + _nn(_split_cols(tinv[u_]), pw_split[u_]) for u_ in units}

    us, ws, aqks, qds, kds = {}, {}, {}, {}, {}
    for u_ in units:
        egc = jnp.exp(gcols[u_])
        rhs = jnp.concatenate([vs[u_] * bcols[u_], ks[u_] * (bcols[u_] * egc)], axis=1)
        uw = _nn(_split_cols(tinv[u_]), _split_rows(rhs))
        us[u_], ws[u_] = uw[:, :HEAD_W], uw[:, HEAD_W:]
        aqks[u_] = _nt(qs[u_].astype(BF16), ks[u_].astype(BF16)) * gammas[u_]
        qds[u_] = (qs[u_] * egc).astype(BF16)
        kds[u_] = ks[u_] * jnp.exp(glasts[u_] - gcols[u_])

    for ch in range(nchunk):
        r0 = ch * CHUNK
        for h in range(nh):
            u_ = (ch, h)
            s = s_ref[h]
            s_hi = s.astype(BF16)
            s_lo = (s - s_hi.astype(F32)).astype(BF16)
            w_hi = ws[u_].astype(BF16)
            w_lo = (ws[u_] - w_hi.astype(F32)).astype(BF16)
            ws_prod = (_nn(jnp.concatenate([w_hi, w_lo], axis=1), jnp.concatenate([s_hi, s_hi], axis=0))
                       + _nn(w_hi, s_lo))
            v_new = us[u_] - ws_prod
            o = _nn(qds[u_], s_hi) + _nn(aqks[u_].astype(BF16), v_new.astype(BF16))
            s_ref[h] = jnp.exp(glasts[u_]) * s + _tn(_split_rows_lhs(kds[u_]), _split_rows(v_new))
            ms = jnp.mean(o * o, axis=-1, keepdims=True)
            o = o * lax.rsqrt(ms + RMS_EPS) * gain_ref[...]
            zg = z_ref[0, r0:r0 + CHUNK, h * HEAD_W:(h + 1) * HEAD_W]
            o_ref[0, r0:r0 + CHUNK, h * HEAD_W:(h + 1) * HEAD_W] = (o * _silu(zg)).astype(o_ref.dtype)


def _delta_net(pf, small, conv_w, par, gain, bsz, t, tc=CHUNK):
    cw = 3 * BRANCH_W
    return pl.pallas_call(
        _delta_kernel,
        grid=(bsz, t // tc),
        in_specs=[pl.BlockSpec((1, tc, cw), lambda b, i: (b, i, 0)),
                  pl.BlockSpec((1, tc, BRANCH_W), lambda b, i: (b, i, PF_CZ)),
                  pl.BlockSpec((1, tc, HEAD_W), lambda b, i: (b, i, 0)),
                  pl.BlockSpec((8, cw), lambda b, i: (0, 0)),
                  pl.BlockSpec((8, HEAD_W), lambda b, i: (0, 0)),
                  pl.BlockSpec((1, HEAD_W), lambda b, i: (0, 0))],
        out_specs=pl.BlockSpec((1, tc, BRANCH_W), lambda b, i: (b, i, 0)),
        out_shape=jax.ShapeDtypeStruct((bsz, t, BRANCH_W), BF16),
        scratch_shapes=[pltpu.VMEM((tc + 8, cw), F32),
                        pltpu.VMEM((BRANCH_W // HEAD_W, HEAD_W, HEAD_W), F32)],
        compiler_params=_cparams(("arbitrary", "arbitrary")),
        name="delta_net",
    )(pf, pf, small, conv_w, par, gain.reshape(1, HEAD_W).astype(F32))


def _hgrn_kernel(q_ref, f_ref, i_ref, gate_ref, lb_ref, gain_ref, o_ref, st_ref):
    tc = q_ref.shape[1]
    nchunk = tc // CHUNK
    nh = BRANCH_W // HEAD_W
    nsub = CHUNK // SUB

    @pl.when(pl.program_id(1) == 0)
    def _():
        st_ref[...] = jnp.zeros_like(st_ref)

    lb = lb_ref[...]
    df = f_ref[0]
    log_sig = jnp.minimum(df, 0.0) - _log1p_exp_neg_abs(df)
    a = jnp.log(lb)
    b = jnp.log1p(-lb) + log_sig
    logf_all = jnp.maximum(a, b) + _log1p_exp_neg_abs(a - b)
    k_all = (1.0 - lb) * _sigmoid(-df)

    ri = lax.broadcasted_iota(jnp.int32, (CHUNK, 3 * CHUNK), 0)
    ci = lax.broadcasted_iota(jnp.int32, (CHUNK, 3 * CHUNK), 1)
    tri3 = jnp.where(ri >= ci % CHUNK, 1.0, 0.0).astype(BF16)
    sub_row = lax.broadcasted_iota(jnp.int32, (SUB, HEAD_W), 0)
    units = [(ch, h) for ch in range(nchunk) for h in range(nh)]

    bcs = []
    for ch in range(nchunk):
        lf = logf_all[ch * CHUNK:(ch + 1) * CHUNK]
        hi = lf.astype(BF16)
        r1 = lf - hi.astype(F32)
        mid = r1.astype(BF16)
        lo = (r1 - mid.astype(F32)).astype(BF16)
        bcs.append(_nn(tri3, jnp.concatenate([hi, mid, lo], axis=0)))

    qs, ks, vs, bcu, bls, qes, dstate, intra = {}, {}, {}, {}, {}, {}, {}, {}
    for u_ in units:
        ch, h = u_
        r0 = ch * CHUNK
        cs = slice(h * HEAD_W, (h + 1) * HEAD_W)
        qs[u_] = q_ref[0, r0:r0 + CHUNK, cs]
        vs[u_] = i_ref[0, r0:r0 + CHUNK, cs]
        ks[u_] = k_all[r0:r0 + CHUNK, cs]
        bcu[u_] = bcs[ch][:, cs]
        bls[u_] = bcu[u_][CHUNK - 1:CHUNK, :]
        qes[u_] = (qs[u_] * jnp.exp(bcu[u_])).astype(BF16)
        k_end = ks[u_] * jnp.exp(bls[u_] - bcu[u_])
        dstate[u_] = _tn(_split_rows_lhs(vs[u_]), _split_rows(k_end))

    for u_ in units:
        q, k, v, bc = qs[u_], ks[u_], vs[u_], bcu[u_]
        pieces = []
        for si in range(nsub):
            rs = slice(si * SUB, (si + 1) * SUB)
            q_s, bc_s = q[rs], bc[rs]
            if si > 0:
                ref_row = bc[si * SUB - 1:si * SUB, :]
                q_dec = (q_s * jnp.exp(bc_s - ref_row)).astype(BF16)
                k_dec = (k[:si * SUB] * jnp.exp(ref_row - bc[:si * SUB])).astype(BF16)
                o_s = _nn(_nt(q_dec, k_dec).astype(BF16), v[:si * SUB].astype(BF16))
            else:
                o_s = jnp.zeros((SUB, HEAD_W), F32)
            for j in range(SUB):
                jj = si * SUB + j
                e = jnp.exp(jnp.where(sub_row >= j, bc_s - bc[jj:jj + 1, :], NEG))
                a_col = jnp.sum(q_s * k[jj:jj + 1, :] * e, axis=-1, keepdims=True)
                o_s = o_s + a_col * v[jj:jj + 1, :]
            pieces.append(o_s)
        intra[u_] = jnp.concatenate(pieces, axis=0)

    for u_ in units:
        ch, h = u_
        r0 = ch * CHUNK
        cs = slice(h * HEAD_W, (h + 1) * HEAD_W)
        st = st_ref[h]
        o = _nt(qes[u_], st.astype(BF16)) + intra[u_]
        st_ref[h] = st * jnp.exp(bls[u_]) + dstate[u_]
        ms = jnp.mean(o * o, axis=-1, keepdims=True)
        o = o * lax.rsqrt(ms + RMS_EPS) * gain_ref[...]
        o_ref[0, r0:r0 + CHUNK, cs] = (o * _silu(gate_ref[0, r0:r0 + CHUNK, cs])).astype(o_ref.dtype)


def _hgrn(pf, lb, gain, bsz, t, tc=CHUNK):
    def spec(cb):
        return pl.BlockSpec((1, tc, BRANCH_W), lambda b, i: (b, i, cb))

    return pl.pallas_call(
        _hgrn_kernel,
        grid=(bsz, t // tc),
        in_specs=[spec(PF_DQ), spec(PF_DF), spec(PF_DI), spec(PF_DGATE),
                  pl.BlockSpec((1, BRANCH_W), lambda b, i: (0, 0)),
                  pl.BlockSpec((1, HEAD_W), lambda b, i: (0, 0))],
        out_specs=pl.BlockSpec((1, tc, BRANCH_W), lambda b, i: (b, i, 0)),
        out_shape=jax.ShapeDtypeStruct((bsz, t, BRANCH_W), BF16),
        scratch_shapes=[pltpu.VMEM((BRANCH_W // HEAD_W, HEAD_W, HEAD_W), F32)],
        compiler_params=_cparams(("arbitrary", "arbitrary")),
        name="hgrn2",
    )(pf, pf, pf, pf, lb.reshape(1, BRANCH_W), gain.reshape(1, HEAD_W).astype(F32))


def _split_w_in(w):
    o = 0
    parts = {}
    for name, width in (("a_q", 512), ("a_k", 512), ("a_v", 512), ("a_gate", 512),
                        ("b_q", 512), ("b_k", 512), ("b_v", 512), ("b_gate", 512),
                        ("c_qkv", 1536), ("c_z", 512), ("c_beta", 4), ("c_a", 4),
                        ("d_q", 512), ("d_f", 512), ("d_i", 512), ("d_gate", 512),
                        ("merge", 4 * D_MODEL)):
        parts[name] = w[:, o:o + width]
        o += width
    w_f = jnp.concatenate([parts[k] for k in ("c_qkv", "c_z", "a_gate", "b_gate", "d_q", "d_f",
                                              "d_i", "d_gate", "a_q", "a_k", "a_v")], axis=1)
    w_b = jnp.concatenate([parts[k] for k in ("b_q", "b_k", "b_v")], axis=1)
    w_s = jnp.concatenate([parts["c_beta"], parts["c_a"],
                           jnp.zeros((w.shape[0], HEAD_W - 8), w.dtype)], axis=1)
    return w_f.astype(BF16), w_b.astype(BF16), w_s.astype(BF16), parts["merge"].astype(BF16)


def kernel(x, norm_gain, w_in, rel_bias, diff_lambda, diff_subln_gain, dn_conv, dn_a_log, dn_dt_bias,
           dn_norm_gain, hg_lb_logits, hg_norm_gain, w_branch, w_out, final_gain):
    bsz, t, d = x.shape
    n = bsz * t
    depth = w_in.shape[0]
    lb_p = jax.nn.softmax(hg_lb_logits.astype(F32), axis=0)
    hg_lb = jnp.clip(jnp.cumsum(lb_p, axis=0) - lb_p[0], 0.0, 1.0)
    bias_a = _dilated_bias(rel_bias[:, :8].astype(F32))
    bias_b = _diff_bias(rel_bias[:, 8:].astype(F32))
    cfar = rel_bias[NUM_BUCKETS - 1, 8:].astype(F32)

    xf = x.reshape(n, d).astype(F32)
    for layer in range(depth):
        w_f, w_b, w_s, w_g = _split_w_in(w_in[layer])
        h = _rmsnorm(xf, norm_gain[layer], BF16)
        pf = _mm(h, w_f, F32, name="proj_f32").reshape(bsz, t, PF_COLS)
        pb = _mm(h, w_b, BF16, name="proj_bf16").reshape(bsz, t, 3 * BRANCH_W)
        ps = _mm(h, w_s, F32, name="proj_small").reshape(bsz, t, HEAD_W)
        gates = _mm(h, w_g, BF16, act="sigmoid", name="proj_gates")

        y_a = _dil_attn(pf, bias_a, bsz, t)

        lam_init = 0.8 - 0.6 * math.exp(-0.3 * layer)
        lq1, lk1, lq2, lk2 = diff_lambda[layer].astype(F32)
        lam = jnp.exp(jnp.sum(lq1 * lk1)) - jnp.exp(jnp.sum(lq2 * lk2)) + lam_init
        scal = jnp.concatenate([jnp.stack([lam, jnp.asarray(1.0 - lam_init, F32)]), cfar,
                                jnp.zeros((2,), F32)])
        y_b = _diff_attn(pb, pf, bias_b, scal, diff_subln_gain[layer], bsz, t)

        conv_w = jnp.concatenate([dn_conv[layer].astype(F32),
                                  jnp.zeros((8 - CONV_K, 3 * BRANCH_W), F32)], axis=0)
        par = jnp.zeros((8, HEAD_W), F32)
        par = par.at[0, 4:8].set(dn_a_log[layer].astype(F32)).at[1, 4:8].set(dn_dt_bias[layer].astype(F32))
        y_c = _delta_net(pf, ps, conv_w, par, dn_norm_gain[layer], bsz, t, tc=DELTA_TILE)

        y_d = _hgrn(pf, hg_lb[layer], hg_norm_gain[layer], bsz, t, tc=HGRN_TILE)

        ys = [y.reshape(n, BRANCH_W) for y in (y_a, y_b, y_c, y_d)]
        xf = _merge_out(gates, ys, w_branch[layer].astype(BF16), w_out[layer].astype(BF16), xf)
    out = _rmsnorm(xf, final_gain, F32)
    return out.reshape(bsz, t, d).astype(x.dtype)
```

```python
import functools
import math

import jax
import jax.numpy as jnp
import numpy as np
from jax import lax
from jax.experimental import pallas as pl
from jax.experimental.pallas import tpu as pltpu

F32 = jnp.float32
BF16 = jnp.bfloat16
HI = lax.Precision.HIGHEST

D_MODEL = 2048
BRANCH_W = 512
HEAD_W = 128
A_HEAD_DIM = 64
DILATIONS = (1, 4, 16)
BAND = 128
A_TILE = BAND * DILATIONS[-1]
A_GROUP = 4
DIFF_QK = 64
B_BT = 256
B_TQ = 512
B_TK = 512
B_NEAR = 2048 // B_BT + 1
CHUNK = 64
DELTA_TILE = 256
HGRN_TILE = 256
SUB = 16
CONV_K = 4
NUM_BUCKETS = 32
MAX_DISTANCE = 2048
RMS_EPS = 1e-6
NEG = -1e30
VMEM_LIMIT = 56 * 1024 * 1024
MM_TM, MM_TN = 2048, 512
MERGE_TM = 256

PF_CQKV, PF_CZ, PF_AGATE, PF_BGATE, PF_DQ, PF_DF, PF_DI, PF_DGATE, PF_AQ, PF_AK, PF_AV = 0, 3, 4, 5, 6, 7, 8, 9, 10, 11, 12
PF_COLS = 13 * 512


def _cparams(sem):
    return pltpu.CompilerParams(dimension_semantics=sem, vmem_limit_bytes=VMEM_LIMIT)


def _nt(a, b, precision=None):
    return lax.dot_general(a, b, (((1,), (1,)), ((), ())), precision=precision,
                           preferred_element_type=F32)


def _tn(a, b, precision=None):
    return lax.dot_general(a, b, (((0,), (0,)), ((), ())), precision=precision,
                           preferred_element_type=F32)


def _nn(a, b, precision=None):
    return jnp.dot(a, b, precision=precision, preferred_element_type=F32)


def _sigmoid(x):
    return 0.5 * jnp.tanh(0.5 * x) + 0.5


def _silu(x):
    return x * _sigmoid(x)


def _log1p_exp_neg_abs(x):
    return jnp.log1p(jnp.exp(-jnp.abs(x)))


def _hi_lo(x):
    hi = x.astype(BF16)
    return hi, (x - hi.astype(F32)).astype(BF16)


def _split_cols(a):
    hi = a.astype(BF16).astype(F32)
    lo = a - hi
    return jnp.concatenate([hi, lo, hi, lo], axis=1).astype(BF16)


def _split_rows(b):
    hi, lo = _hi_lo(b)
    return jnp.concatenate([hi, hi, lo, lo], axis=0)


def _split_rows_lhs(a):
    hi, lo = _hi_lo(a)
    return jnp.concatenate([hi, lo, hi, lo], axis=0)


def _nt_acc(a, b):
    ah, al = _hi_lo(a)
    bh, bl = _hi_lo(b)
    return (_nt(jnp.concatenate([ah, al], axis=1), jnp.concatenate([bh, bh], axis=1))
            + _nt(ah, bl))


def _rmsnorm_kernel(x_ref, g_ref, o_ref):
    x = x_ref[...]
    ms = jnp.mean(x * x, axis=-1, keepdims=True)
    o_ref[...] = (x * lax.rsqrt(ms + RMS_EPS) * g_ref[...]).astype(o_ref.dtype)


def _rmsnorm(x, gain, out_dtype, tm=512):
    n, d = x.shape
    return pl.pallas_call(
        _rmsnorm_kernel,
        grid=(n // tm,),
        in_specs=[pl.BlockSpec((tm, d), lambda i: (i, 0)),
                  pl.BlockSpec((1, d), lambda i: (0, 0))],
        out_specs=pl.BlockSpec((tm, d), lambda i: (i, 0)),
        out_shape=jax.ShapeDtypeStruct((n, d), out_dtype),
        compiler_params=_cparams(("parallel",)),
        name="rmsnorm",
    )(x, gain.reshape(1, d).astype(F32))


def _mm_kernel(a_ref, w_ref, o_ref, *, act):
    acc = _nn(a_ref[...], w_ref[...])
    if act == "sigmoid":
        acc = _sigmoid(acc)
    o_ref[...] = acc.astype(o_ref.dtype)


def _mm(a, w, out_dtype, act=None, tm=MM_TM, tn=MM_TN, name="mm"):
    n, k = a.shape
    c = w.shape[1]
    tn = min(tn, c)
    return pl.pallas_call(
        functools.partial(_mm_kernel, act=act),
        grid=(n // tm, c // tn),
        in_specs=[pl.BlockSpec((tm, k), lambda i, j: (i, 0)),
                  pl.BlockSpec((k, tn), lambda i, j: (0, j))],
        out_specs=pl.BlockSpec((tm, tn), lambda i, j: (i, j)),
        out_shape=jax.ShapeDtypeStruct((n, c), out_dtype),
        compiler_params=_cparams(("parallel", "parallel")),
        name=name,
    )(a, w)


def _merge_out_kernel(g0, g1, g2, g3, y0, y1, y2, y3, wbr_ref, wout_ref, x_ref, o_ref):
    mixed = g0[...].astype(F32) * _nn(y0[...], wbr_ref[0])
    mixed += g1[...].astype(F32) * _nn(y1[...], wbr_ref[1])
    mixed += g2[...].astype(F32) * _nn(y2[...], wbr_ref[2])
    mixed += g3[...].astype(F32) * _nn(y3[...], wbr_ref[3])
    o_ref[...] = x_ref[...] + _nn(mixed.astype(BF16), wout_ref[...])


def _merge_out(gates, ys, w_br, w_out, x, tm=MERGE_TM):
    n = gates.shape[0]
    gate_specs = [pl.BlockSpec((tm, D_MODEL), functools.partial(lambda i, b: (i, b), b=b))
                  for b in range(4)]
    y_specs = [pl.BlockSpec((tm, BRANCH_W), lambda i: (i, 0)) for _ in range(4)]
    resident = dict(pipeline_mode=pl.Buffered(1))
    return pl.pallas_call(
        _merge_out_kernel,
        grid=(n // tm,),
        in_specs=gate_specs + y_specs + [
            pl.BlockSpec((4, BRANCH_W, D_MODEL), lambda i: (0, 0, 0), **resident),
            pl.BlockSpec((D_MODEL, D_MODEL), lambda i: (0, 0), **resident),
            pl.BlockSpec((tm, D_MODEL), lambda i: (i, 0))],
        out_specs=pl.BlockSpec((tm, D_MODEL), lambda i: (i, 0)),
        out_shape=jax.ShapeDtypeStruct((n, D_MODEL), F32),
        compiler_params=_cparams(("parallel",)),
        name="merge_out",
    )(gates, gates, gates, gates, *ys, w_br, w_out, x)


def _t5_bucket(dist):
    n = jnp.maximum(dist, 0)
    max_exact = NUM_BUCKETS // 2
    nf = jnp.maximum(n, max_exact).astype(F32)
    large = max_exact + (jnp.log(nf / max_exact) / math.log(MAX_DISTANCE / max_exact)
                         * (NUM_BUCKETS - max_exact)).astype(jnp.int32)
    large = jnp.minimum(large, NUM_BUCKETS - 1)
    return jnp.where(n < max_exact, n, large)


def _bucket_lookup(table, bucket):
    tab = table.T.reshape((table.shape[1],) + (1,) * bucket.ndim + (NUM_BUCKETS,))
    out = jnp.zeros((table.shape[1],) + bucket.shape, F32)
    for b in range(NUM_BUCKETS):
        out = jnp.where(bucket[None] == b, tab[..., b], out)
    return out


def _dilated_bias(bias_a):
    qi = jnp.arange(BAND)[:, None]
    kj = jnp.arange(2 * BAND)[None, :]
    rel = qi + BAND - kj
    valid = (rel >= 0) & (rel <= BAND)
    tabs = []
    for dil in DILATIONS:
        b = _bucket_lookup(bias_a, _t5_bucket(rel * dil))
        tabs.append(jnp.where(valid[None], b, NEG))
    return jnp.stack(tabs)


def _diff_bias(bias_b):
    qi = jnp.arange(B_BT)[:, None]
    kj = jnp.arange(B_BT)[None, :]
    tabs = []
    for d in range(-1, B_NEAR + 1):
        rel = d * B_BT + qi - kj
        b = _bucket_lookup(bias_b, _t5_bucket(rel))
        tabs.append(jnp.where((rel >= 0)[None], b, NEG))
    return jnp.stack(tabs, axis=1)


def _dil_attn_kernel(q_ref, kp_ref, kc_ref, vp_ref, vc_ref, gate_ref, bias_ref, o_ref,
                     kcat, vcat, acc_ref, m_ref, l_ref):
    n = pl.program_id(1)
    kcat[0:A_TILE, :] = kp_ref[0]
    kcat[A_TILE:2 * A_TILE, :] = kc_ref[0]
    vcat[0:A_TILE, :] = vp_ref[0]
    vcat[A_TILE:2 * A_TILE, :] = vc_ref[0]
    lane = lax.broadcasted_iota(jnp.int32, (BAND, HEAD_W), 1)
    lo = lane < A_HEAD_DIM
    col = lax.broadcasted_iota(jnp.int32, (BAND, 2 * BAND), 1)
    nblk_tile = A_TILE // BAND

    first = len(DILATIONS) - 1
    for p, r in reversed(list(enumerate(DILATIONS))):
        def rows(start, r=r):
            return pl.ds(start, BAND, stride=r) if r > 1 else pl.ds(pl.multiple_of(start, BAND), BAND)

        def body(it, carry, p=p, r=r, rows=rows):
            starts, no_prevs, vvs = [], [], []
            ss, es, ms, ls, outs = {}, {}, {}, {}, {}
            for g in range(A_GROUP):
                idx = it * A_GROUP + g
                j = idx // r
                start = j * (BAND * r) + idx % r
                starts.append(start)
                no_prevs.append(jnp.logical_and(n == 0, j == 0))
                q = (q_ref[0, rows(start), :] * (A_HEAD_DIM ** -0.5)).astype(BF16)
                kk = jnp.concatenate([kcat[rows(A_TILE + start - BAND * r), :],
                                      kcat[rows(A_TILE + start), :]], axis=0).astype(BF16)
                vvs.append(jnp.concatenate([vcat[rows(A_TILE + start - BAND * r), :],
                                            vcat[rows(A_TILE + start), :]], axis=0).astype(BF16))
                for hh in range(2):
                    qz = jnp.where(lo if hh == 0 else jnp.logical_not(lo), q, jnp.zeros_like(q))
                    ss[g, hh] = _nt(qz, kk)
            for g in range(A_GROUP):
                for hh in range(2):
                    s = ss[g, hh] + bias_ref[p, hh]
                    s = jnp.where(jnp.logical_and(no_prevs[g], col < BAND), NEG, s)
                    m = jnp.max(s, axis=-1, keepdims=True)
                    e = jnp.exp(s - m)
                    ms[g, hh] = jnp.broadcast_to(m, (BAND, HEAD_W))
                    ls[g, hh] = jnp.broadcast_to(jnp.sum(e, axis=-1, keepdims=True), (BAND, HEAD_W))
                    es[g, hh] = e.astype(BF16)
            for g in range(A_GROUP):
                for hh in range(2):
                    outs[g, hh] = _nn(es[g, hh], vvs[g])
            merged = []
            for g in range(A_GROUP):
                o_new = jnp.where(lo, outs[g, 0], outs[g, 1])
                m_new = jnp.where(lo, ms[g, 0], ms[g, 1])
                l_new = jnp.where(lo, ls[g, 0], ls[g, 1])
                if p != first:
                    rws = rows(starts[g])
                    m_old = m_ref[rws, :]
                    m_tot = jnp.maximum(m_old, m_new)
                    a = jnp.exp(m_old - m_tot)
                    b = jnp.exp(m_new - m_tot)
                    o_new = a * acc_ref[rws, :] + b * o_new
                    l_new = a * l_ref[rws, :] + b * l_new
                    m_new = m_tot
                merged.append((o_new, m_new, l_new))
            for g in range(A_GROUP):
                rws = rows(starts[g])
                acc_ref[rws, :], m_ref[rws, :], l_ref[rws, :] = merged[g]
            return carry

        lax.fori_loop(0, nblk_tile // A_GROUP, body, 0)

    o_ref[0] = (acc_ref[...] / l_ref[...] * _silu(gate_ref[0])).astype(o_ref.dtype)


def _dil_attn(pf, bias_tab, bsz, t):
    nt = t // A_TILE
    hp = BRANCH_W // HEAD_W
    blk = (1, A_TILE, HEAD_W)

    def cur(cb):
        return pl.BlockSpec(blk, lambda b, n, h: (b, n, cb * hp + h))

    def prev(cb):
        return pl.BlockSpec(blk, lambda b, n, h: (b, jnp.maximum(n - 1, 0), cb * hp + h))

    return pl.pallas_call(
        _dil_attn_kernel,
        grid=(bsz, nt, hp),
        in_specs=[cur(PF_AQ), prev(PF_AK), cur(PF_AK), prev(PF_AV), cur(PF_AV), cur(PF_AGATE),
                  pl.BlockSpec((3, 2, BAND, 2 * BAND), lambda b, n, h: (0, h, 0, 0))],
        out_specs=pl.BlockSpec(blk, lambda b, n, h: (b, n, h)),
        out_shape=jax.ShapeDtypeStruct((bsz, t, BRANCH_W), BF16),
        scratch_shapes=[pltpu.VMEM((2 * A_TILE, HEAD_W), F32), pltpu.VMEM((2 * A_TILE, HEAD_W), F32),
                        pltpu.VMEM((A_TILE, HEAD_W), F32), pltpu.VMEM((A_TILE, HEAD_W), F32),
                        pltpu.VMEM((A_TILE, HEAD_W), F32)],
        compiler_params=_cparams(("parallel", "parallel", "parallel")),
        name="dilated_attn",
    )(pf, pf, pf, pf, pf, pf, bias_tab)


def _diff_attn_kernel(scal_ref, q_ref, k_ref, v_ref, gate_ref, bias_ref, gain_ref, o_ref,
                      kt_ref, vx_ref, s0_ref, s1_ref, p0_ref, p1_ref, al0_ref, al1_ref,
                      acc_ref, m_ref):
    h = pl.program_id(1)
    qi = pl.program_id(2)
    nkb = kt_ref.shape[0]
    nrh = B_TQ // B_BT
    nsub = B_TK // B_BT
    ncol = B_TK // HEAD_W

    @pl.when(qi == 0)
    def _():
        def tr(kb, c):
            rows = pl.ds(pl.multiple_of(kb * B_TK, B_TK), B_TK)
            kt_ref[kb] = k_ref[0, rows, :].astype(F32).T.astype(BF16)
            vx_ref[rows, 0:HEAD_W] = v_ref[0, rows, :]
            vx_ref[rows, HEAD_W:2 * HEAD_W] = jnp.ones((B_TK, HEAD_W), BF16)
            return c
        lax.fori_loop(0, nkb, tr, 0)

    lane = lax.broadcasted_iota(jnp.int32, (B_TQ, HEAD_W), 1)
    lo = lane < DIFF_QK
    q = q_ref[0] * (DIFF_QK ** -0.5)
    zero = jnp.zeros_like(q)
    qz = (jnp.where(lo, q, zero), jnp.where(lo, zero, q))
    cfar = scal_ref[2 + h]
    nk = ((qi + 1) * nrh - 1) // nsub + 1
    n_far = jnp.maximum((nrh * qi - B_NEAR + 1) // nsub, 0)
    per_sub = B_BT // HEAD_W

    s_bufs, p_bufs, al_bufs = (s0_ref, s1_ref), (p0_ref, p1_ref), (al0_ref, al1_ref)
    acc_ref[...] = jnp.zeros_like(acc_ref)
    m_ref[...] = jnp.full_like(m_ref, NEG)
    al1_ref[...] = jnp.ones_like(al1_ref)
    p1_ref[...] = jnp.zeros_like(p1_ref)

    row_halves = [slice(rh * B_BT, (rh + 1) * B_BT) for rh in range(nrh)]

    def qk(kb, par):
        kt = kt_ref[jnp.minimum(kb, nk - 1)]
        for mi in range(2):
            for rows in row_halves:
                s_bufs[par][mi, rows, :] = _nn(qz[mi][rows], kt)

    def pv(kb, par):
        kb = jnp.clip(kb, 0, nk - 1)
        vx = vx_ref[pl.ds(pl.multiple_of(kb * B_TK, B_TK), B_TK), :]
        for mi in range(2):
            for rows in row_halves:
                a = al_bufs[par][mi, rows]
                upd = _nn(p_bufs[par][mi, rows, :], vx)
                acc_ref[mi, rows] = jnp.concatenate([a, a], axis=1) * acc_ref[mi, rows] + upd

    def col(par, mi, c, kb, near):
        x = s_bufs[par][mi, :, c * HEAD_W:(c + 1) * HEAD_W]
        if near:
            cc = (c % per_sub) * HEAD_W
            tiles = [jnp.clip(nrh * qi + rh - nsub * kb - c // per_sub, -1, B_NEAR) + 1
                     for rh in range(nrh)]
            x = x + jnp.concatenate([bias_ref[0, tl, :, cc:cc + HEAD_W] for tl in tiles], axis=0)
        return x

    def softmax(kb, par, near):
        shift = 0.0 if near else cfar
        m_sub = []
        for mi in range(2):
            mx = col(par, mi, 0, kb, near)
            for c in range(1, ncol):
                mx = jnp.maximum(mx, col(par, mi, c, kb, near))
            m_prev = m_ref[mi]
            m_next = jnp.maximum(m_prev, jnp.max(mx, axis=-1, keepdims=True) + shift)
            al_bufs[par][mi] = jnp.exp(m_prev - m_next)
            m_ref[mi] = m_next
            m_sub.append(m_next - shift)
        for mi in range(2):
            for c in range(ncol):
                e = jnp.exp(col(par, mi, c, kb, near) - m_sub[mi])
                p_bufs[par][mi, :, c * HEAD_W:(c + 1) * HEAD_W] = e.astype(BF16)

    def pair(u, near):
        for par in range(2):
            t = 2 * u + par
            qk(t + 1, 1 - par)
            pv(t - 1, 1 - par)
            softmax(t, par, near)

    def far_body(u, c):
        pair(u, False)
        return c

    def near_body(u, c):
        pair(u, True)
        return c

    far_pairs = n_far // 2
    all_pairs = (nk + 1) // 2
    qk(0, 0)
    lax.fori_loop(0, far_pairs, far_body, 0)
    lax.fori_loop(far_pairs, all_pairs, near_body, 0)
    pv(2 * all_pairs - 1, 1)

    lam = scal_ref[0]
    a1, a2 = acc_ref[0], acc_ref[1]
    o = a1[:, :HEAD_W] / a1[:, HEAD_W:] - lam * (a2[:, :HEAD_W] / a2[:, HEAD_W:])
    ms = jnp.mean(o * o, axis=-1, keepdims=True)
    o = o * lax.rsqrt(ms + RMS_EPS) * gain_ref[...] * scal_ref[1]
    o_ref[0] = (o * _silu(gate_ref[0])).astype(o_ref.dtype)


def _diff_attn(pb, pf, bias_tab, scal, gain, bsz, t):
    nh = BRANCH_W // HEAD_W
    nq = t // B_TQ
    return pl.pallas_call(
        _diff_attn_kernel,
        grid=(bsz, nh, nq),
        in_specs=[pl.BlockSpec(memory_space=pltpu.SMEM),
                  pl.BlockSpec((1, B_TQ, HEAD_W), lambda b, h, i: (b, i, h)),
                  pl.BlockSpec((1, t, HEAD_W), lambda b, h, i: (b, 0, nh + h)),
                  pl.BlockSpec((1, t, HEAD_W), lambda b, h, i: (b, 0, 2 * nh + h)),
                  pl.BlockSpec((1, B_TQ, HEAD_W), lambda b, h, i: (b, i, PF_BGATE * nh + h)),
                  pl.BlockSpec((1, B_NEAR + 2, B_BT, B_BT), lambda b, h, i: (h, 0, 0, 0)),
                  pl.BlockSpec((1, HEAD_W), lambda b, h, i: (0, 0))],
        out_specs=pl.BlockSpec((1, B_TQ, HEAD_W), lambda b, h, i: (b, i, h)),
        out_shape=jax.ShapeDtypeStruct((bsz, t, BRANCH_W), BF16),
        scratch_shapes=[pltpu.VMEM((t // B_TK, HEAD_W, B_TK), BF16),
                        pltpu.VMEM((t, 2 * HEAD_W), BF16),
                        pltpu.VMEM((2, B_TQ, B_TK), F32),
                        pltpu.VMEM((2, B_TQ, B_TK), F32),
                        pltpu.VMEM((2, B_TQ, B_TK), BF16),
                        pltpu.VMEM((2, B_TQ, B_TK), BF16),
                        pltpu.VMEM((2, B_TQ, HEAD_W), F32),
                        pltpu.VMEM((2, B_TQ, HEAD_W), F32),
                        pltpu.VMEM((2, B_TQ, 2 * HEAD_W), F32),
                        pltpu.VMEM((2, B_TQ, HEAD_W), F32)],
        compiler_params=_cparams(("arbitrary", "arbitrary", "arbitrary")),
        name="diff_attn",
    )(scal, pb, pb, pb, pf, bias_tab, gain.reshape(1, HEAD_W).astype(F32))


def _delta_kernel(qkv_ref, z_ref, small_ref, conv_ref, par_ref, gain_ref, o_ref,
                  xe_ref, s_ref):
    tc = qkv_ref.shape[1]
    nchunk = tc // CHUNK

    @pl.when(pl.program_id(1) == 0)
    def _():
        xe_ref[0:8, :] = jnp.zeros((8, xe_ref.shape[1]), F32)
        s_ref[...] = jnp.zeros_like(s_ref)

    xe_ref[8:8 + tc, :] = qkv_ref[0]
    conv = conv_ref[0:1, :] * xe_ref[pl.ds(8 - 3, tc), :]
    for kk in range(1, CONV_K):
        conv = conv + conv_ref[kk:kk + 1, :] * xe_ref[pl.ds(8 - 3 + kk, tc), :]
    xe_ref[0:8, :] = xe_ref[tc:tc + 8, :]
    c = _silu(conv)

    small = small_ref[0]
    beta_all = _sigmoid(small)
    xa = small + par_ref[1:2, :]
    softplus = jnp.maximum(xa, 0.0) + _log1p_exp_neg_abs(xa)
    g_all = -jnp.exp(par_ref[0:1, :]) * softplus

    ri = lax.broadcasted_iota(jnp.int32, (CHUNK, CHUNK), 0)
    ci = lax.broadcasted_iota(jnp.int32, (CHUNK, CHUNK), 1)
    tri = ri >= ci
    strict = ri > ci
    eye = (ri == ci).astype(F32)
    rt = lax.broadcasted_iota(jnp.int32, (tc, tc), 0)
    ct = lax.broadcasted_iota(jnp.int32, (tc, tc), 1)
    tri_blocks = jnp.logical_and(rt >= ct, rt // CHUNK == ct // CHUNK).astype(F32)
    sel = (lax.broadcasted_iota(jnp.int32, (8, HEAD_W), 1)
           == lax.broadcasted_iota(jnp.int32, (8, HEAD_W), 0) + 4).astype(F32)
    nh = BRANCH_W // HEAD_W
    units = [(ch, h) for ch in range(nchunk) for h in range(nh)]

    gcum_all = _nn(tri_blocks, g_all, HI)
    grow_all = _nt(sel, gcum_all, HI)

    qs, ks, vs, bcols, gcols, glasts, gammas, xs = {}, {}, {}, {}, {}, {}, {}, {}
    for h in range(nh):
        qf = c[:, h * HEAD_W:(h + 1) * HEAD_W]
        kf = c[:, BRANCH_W + h * HEAD_W:BRANCH_W + (h + 1) * HEAD_W]
        qf = qf * lax.rsqrt(jnp.sum(qf * qf, axis=-1, keepdims=True) + 1e-6) * (HEAD_W ** -0.5)
        kf = kf * lax.rsqrt(jnp.sum(kf * kf, axis=-1, keepdims=True) + 1e-6)
        for ch in range(nchunk):
            r0 = ch * CHUNK
            u_ = (ch, h)
            qs[u_] = qf[r0:r0 + CHUNK]
            ks[u_] = kf[r0:r0 + CHUNK]
            vs[u_] = c[r0:r0 + CHUNK, 2 * BRANCH_W + h * HEAD_W:2 * BRANCH_W + (h + 1) * HEAD_W]
            bcols[u_] = beta_all[r0:r0 + CHUNK, h:h + 1]
            gcols[u_] = gcum_all[r0:r0 + CHUNK, 4 + h:5 + h]
            glasts[u_] = gcum_all[r0 + CHUNK - 1:r0 + CHUNK, 4 + h:5 + h]
            grow = grow_all[h:h + 1, r0:r0 + CHUNK]
            gammas[u_] = jnp.exp(jnp.where(tri, gcols[u_] - grow, NEG))
    for u_ in units:
        kkt = _nt_acc(ks[u_], ks[u_])
        xs[u_] = jnp.where(strict, -(bcols[u_] * kkt * gammas[u_]), 0.0)

    tinv = {u_: eye + xs[u_] for u_ in units}
    pw = xs
    for _ in range(5):
        pw_split = {u_: _split_rows(pw[u_]) for u_ in units}
        pw = {u_: _nn(_split_cols(pw[u_]), pw_split[u_]) for u_ in units}
        pw_split = {u_: _split_rows(pw[u_]) for u_ in units}
        tinv = {u_: tinv[u_] + _nn(_split_cols(tinv[u_]), pw_split[u_]) for u_ in units}

    us, ws, aqks, qds, kds = {}, {}, {}, {}, {}
    for u_ in units:
        egc = jnp.exp(gcols[u_])
        rhs = jnp.concatenate([vs[u_] * bcols[u_], ks[u_] * (bcols[u_] * egc)], axis=1)
        uw = _nn(_split_cols(tinv[u_]), _split_rows(rhs))
        us[u_], ws[u_] = uw[:, :HEAD_W], uw[:, HEAD_W:]
        aqks[u_] = _nt(qs[u_].astype(BF16), ks[u_].astype(BF16)) * gammas[u_]
        qds[u_] = (qs[u_] * egc).astype(BF16)
        kds[u_] = ks[u_] * jnp.exp(glasts[u_] - gcols[u_])

    for ch in range(nchunk):
        r0 = ch * CHUNK
        for h in range(nh):
            u_ = (ch, h)
            s = s_ref[h]
            s_hi = s.astype(BF16)
            s_lo = (s - s_hi.astype(F32)).astype(BF16)
            w_hi = ws[u_].astype(BF16)
            w_lo = (ws[u_] - w_hi.astype(F32)).astype(BF16)
            ws_prod = (_nn(jnp.concatenate([w_hi, w_lo], axis=1), jnp.concatenate([s_hi, s_hi], axis=0))
                       + _nn(w_hi, s_lo))
            v_new = us[u_] - ws_prod
            o = _nn(qds[u_], s_hi) + _nn(aqks[u_].astype(BF16), v_new.astype(BF16))
            s_ref[h] = jnp.exp(glasts[u_]) * s + _tn(_split_rows_lhs(kds[u_]), _split_rows(v_new))
            ms = jnp.mean(o * o, axis=-1, keepdims=True)
            o = o * lax.rsqrt(ms + RMS_EPS) * gain_ref[...]
            zg = z_ref[0, r0:r0 + CHUNK, h * HEAD_W:(h + 1) * HEAD_W]
            o_ref[0, r0:r0 + CHUNK, h * HEAD_W:(h + 1) * HEAD_W] = (o * _silu(zg)).astype(o_ref.dtype)


def _delta_net(pf, small, conv_w, par, gain, bsz, t, tc=CHUNK):
    cw = 3 * BRANCH_W
    return pl.pallas_call(
        _delta_kernel,
        grid=(bsz, t // tc),
        in_specs=[pl.BlockSpec((1, tc, cw), lambda b, i: (b, i, 0)),
                  pl.BlockSpec((1, tc, BRANCH_W), lambda b, i: (b, i, PF_CZ)),
                  pl.BlockSpec((1, tc, HEAD_W), lambda b, i: (b, i, 0)),
                  pl.BlockSpec((8, cw), lambda b, i: (0, 0)),
                  pl.BlockSpec((8, HEAD_W), lambda b, i: (0, 0)),
                  pl.BlockSpec((1, HEAD_W), lambda b, i: (0, 0))],
        out_specs=pl.BlockSpec((1, tc, BRANCH_W), lambda b, i: (b, i, 0)),
        out_shape=jax.ShapeDtypeStruct((bsz, t, BRANCH_W), BF16),
        scratch_shapes=[pltpu.VMEM((tc + 8, cw), F32),
                        pltpu.VMEM((BRANCH_W // HEAD_W, HEAD_W, HEAD_W), F32)],
        compiler_params=_cparams(("arbitrary", "arbitrary")),
        name="delta_net",
    )(pf, pf, small, conv_w, par, gain.reshape(1, HEAD_W).astype(F32))


def _hgrn_kernel(q_ref, f_ref, i_ref, gate_ref, lb_ref, gain_ref, o_ref, st_ref):
    tc = q_ref.shape[1]
    nchunk = tc // CHUNK
    nh = BRANCH_W // HEAD_W
    nsub = CHUNK // SUB

    @pl.when(pl.program_id(1) == 0)
    def _():
        st_ref[...] = jnp.zeros_like(st_ref)

    lb = lb_ref[...]
    df = f_ref[0]
    log_sig = jnp.minimum(df, 0.0) - _log1p_exp_neg_abs(df)
    a = jnp.log(lb)
    b = jnp.log1p(-lb) + log_sig
    logf_all = jnp.maximum(a, b) + _log1p_exp_neg_abs(a - b)
    k_all = (1.0 - lb) * _sigmoid(-df)

    ri = lax.broadcasted_iota(jnp.int32, (CHUNK, 3 * CHUNK), 0)
    ci = lax.broadcasted_iota(jnp.int32, (CHUNK, 3 * CHUNK), 1)
    tri3 = jnp.where(ri >= ci % CHUNK, 1.0, 0.0).astype(BF16)
    sub_row = lax.broadcasted_iota(jnp.int32, (SUB, HEAD_W), 0)
    units = [(ch, h) for ch in range(nchunk) for h in range(nh)]

    bcs = []
    for ch in range(nchunk):
        lf = logf_all[ch * CHUNK:(ch + 1) * CHUNK]
        hi = lf.astype(BF16)
        r1 = lf - hi.astype(F32)
        mid = r1.astype(BF16)
        lo = (r1 - mid.astype(F32)).astype(BF16)
        bcs.append(_nn(tri3, jnp.concatenate([hi, mid, lo], axis=0)))

    qs, ks, vs, bcu, bls, qes, dstate, intra = {}, {}, {}, {}, {}, {}, {}, {}
    for u_ in units:
        ch, h = u_
        r0 = ch * CHUNK
        cs = slice(h * HEAD_W, (h + 1) * HEAD_W)
        qs[u_] = q_ref[0, r0:r0 + CHUNK, cs]
        vs[u_] = i_ref[0, r0:r0 + CHUNK, cs]
        ks[u_] = k_all[r0:r0 + CHUNK, cs]
        bcu[u_] = bcs[ch][:, cs]
        bls[u_] = bcu[u_][CHUNK - 1:CHUNK, :]
        qes[u_] = (qs[u_] * jnp.exp(bcu[u_])).astype(BF16)
        k_end = ks[u_] * jnp.exp(bls[u_] - bcu[u_])
        dstate[u_] = _tn(_split_rows_lhs(vs[u_]), _split_rows(k_end))

    for u_ in units:
        q, k, v, bc = qs[u_], ks[u_], vs[u_], bcu[u_]
        pieces = []
        for si in range(nsub):
            rs = slice(si * SUB, (si + 1) * SUB)
            q_s, bc_s = q[rs], bc[rs]
            if si > 0:
                ref_row = bc[si * SUB - 1:si * SUB, :]
                q_dec = (q_s * jnp.exp(bc_s - ref_row)).astype(BF16)
                k_dec = (k[:si * SUB] * jnp.exp(ref_row - bc[:si * SUB])).astype(BF16)
                o_s = _nn(_nt(q_dec, k_dec).astype(BF16), v[:si * SUB].astype(BF16))
            else:
                o_s = jnp.zeros((SUB, HEAD_W), F32)
            for j in range(SUB):
                jj = si * SUB + j
                e = jnp.exp(jnp.where(sub_row >= j, bc_s - bc[jj:jj + 1, :], NEG))
                a_col = jnp.sum(q_s * k[jj:jj + 1, :] * e, axis=-1, keepdims=True)
                o_s = o_s + a_col * v[jj:jj + 1, :]
            pieces.append(o_s)
        intra[u_] = jnp.concatenate(pieces, axis=0)

    for u_ in units:
        ch, h = u_
        r0 = ch * CHUNK
        cs = slice(h * HEAD_W, (h + 1) * HEAD_W)
        st = st_ref[h]
        o = _nt(qes[u_], st.astype(BF16)) + intra[u_]
        st_ref[h] = st * jnp.exp(bls[u_]) + dstate[u_]
        ms = jnp.mean(o * o, axis=-1, keepdims=True)
        o = o * lax.rsqrt(ms + RMS_EPS) * gain_ref[...]
        o_ref[0, r0:r0 + CHUNK, cs] = (o * _silu(gate_ref[0, r0:r0 + CHUNK, cs])).astype(o_ref.dtype)


def _hgrn(pf, lb, gain, bsz, t, tc=CHUNK):
    def spec(cb):
        return pl.BlockSpec((1, tc, BRANCH_W), lambda b, i: (b, i, cb))

    return pl.pallas_call(
        _hgrn_kernel,
        grid=(bsz, t // tc),
        in_specs=[spec(PF_DQ), spec(PF_DF), spec(PF_DI), spec(PF_DGATE),
                  pl.BlockSpec((1, BRANCH_W), lambda b, i: (0, 0)),
                  pl.BlockSpec((1, HEAD_W), lambda b, i: (0, 0))],
        out_specs=pl.BlockSpec((1, tc, BRANCH_W), lambda b, i: (b, i, 0)),
        out_shape=jax.ShapeDtypeStruct((bsz, t, BRANCH_W), BF16),
        scratch_shapes=[pltpu.VMEM((BRANCH_W // HEAD_W, HEAD_W, HEAD_W), F32)],
        compiler_params=_cparams(("arbitrary", "arbitrary")),
        name="hgrn2",
    )(pf, pf, pf, pf, lb.reshape(1, BRANCH_W), gain.reshape(1, HEAD_W).astype(F32))


def _split_w_in(w):
    o = 0
    parts = {}
    for name, width in (("a_q", 512), ("a_k", 512), ("a_v", 512), ("a_gate", 512),
                        ("b_q", 512), ("b_k", 512), ("b_v", 512), ("b_gate", 512),
                        ("c_qkv", 1536), ("c_z", 512), ("c_beta", 4), ("c_a", 4),
                        ("d_q", 512), ("d_f", 512), ("d_i", 512), ("d_gate", 512),
                        ("merge", 4 * D_MODEL)):
        parts[name] = w[:, o:o + width]
        o += width
    w_f = jnp.concatenate([parts[k] for k in ("c_qkv", "c_z", "a_gate", "b_gate", "d_q", "d_f",
                                              "d_i", "d_gate", "a_q", "a_k", "a_v")], axis=1)
    w_b = jnp.concatenate([parts[k] for k in ("b_q", "b_k", "b_v")], axis=1)
    w_s = jnp.concatenate([parts["c_beta"], parts["c_a"],
                           jnp.zeros((w.shape[0], HEAD_W - 8), w.dtype)], axis=1)
    return w_f.astype(BF16), w_b.astype(BF16), w_s.astype(BF16), parts["merge"].astype(BF16)


def kernel(x, norm_gain, w_in, rel_bias, diff_lambda, diff_subln_gain, dn_conv, dn_a_log, dn_dt_bias,
           dn_norm_gain, hg_lb_logits, hg_norm_gain, w_branch, w_out, final_gain):
    bsz, t, d = x.shape
    n = bsz * t
    depth = w_in.shape[0]
    lb_p = jax.nn.softmax(hg_lb_logits.astype(F32), axis=0)
    hg_lb = jnp.clip(jnp.cumsum(lb_p, axis=0) - lb_p[0], 0.0, 1.0)
    bias_a = _dilated_bias(rel_bias[:, :8].astype(F32))
    bias_b = _diff_bias(rel_bias[:, 8:].astype(F32))
    cfar = rel_bias[NUM_BUCKETS - 1, 8:].astype(F32)

    xf = x.reshape(n, d).astype(F32)
    for layer in range(depth):
        w_f, w_b, w_s, w_g = _split_w_in(w_in[layer])
        h = _rmsnorm(xf, norm_gain[layer], BF16)
        pf = _mm(h, w_f, F32, name="proj_f32").reshape(bsz, t, PF_COLS)
        pb = _mm(h, w_b, BF16, name="proj_bf16").reshape(bsz, t, 3 * BRANCH_W)
        ps = _mm(h, w_s, F32, name="proj_small").reshape(bsz, t, HEAD_W)
        gates = _mm(h, w_g, BF16, act="sigmoid", name="proj_gates")

        y_a = _dil_attn(pf, bias_a, bsz, t)

        lam_init = 0.8 - 0.6 * math.exp(-0.3 * layer)
        lq1, lk1, lq2, lk2 = diff_lambda[layer].astype(F32)
        lam = jnp.exp(jnp.sum(lq1 * lk1)) - jnp.exp(jnp.sum(lq2 * lk2)) + lam_init
        scal = jnp.concatenate([jnp.stack([lam, jnp.asarray(1.0 - lam_init, F32)]), cfar,
                                jnp.zeros((2,), F32)])
        y_b = _diff_attn(pb, pf, bias_b, scal, diff_subln_gain[layer], bsz, t)

        conv_w = jnp.concatenate([dn_conv[layer].astype(F32),
                                  jnp.zeros((8 - CONV_K, 3 * BRANCH_W), F32)], axis=0)
        par = jnp.zeros((8, HEAD_W), F32)
        par = par.at[0, 4:8].set(dn_a_log[layer].astype(F32)).at[1, 4:8].set(dn_dt_bias[layer].astype(F32))
        y_c = _delta_net(pf, ps, conv_w, par, dn_norm_gain[layer], bsz, t, tc=DELTA_TILE)

        y_d = _hgrn(pf, hg_lb[layer], hg_norm_gain[layer], bsz, t, tc=HGRN_TILE)

        ys = [y.reshape(n, BRANCH_W) for y in (y_a, y_b, y_c, y_d)]
        xf = _merge_out(gates, ys, w_branch[layer].astype(BF16), w_out[layer].astype(BF16), xf)
    out = _rmsnorm(xf, final_gain, F32)
    return out.reshape(bsz, t, d).astype(x.dtype)
```

```python
import functools
import math

import jax
import jax.numpy as jnp
import numpy as np
from jax import lax
from jax.experimental import pallas as pl
from jax.experimental.pallas import tpu as pltpu

F32 = jnp.float32
BF16 = jnp.bfloat16
HI = lax.Precision.HIGHEST

D_MODEL = 2048
BRANCH_W = 512
HEAD_W = 128
A_HEAD_DIM = 64
DILATIONS = (1, 4, 16)
BAND = 128
A_TILE = BAND * DILATIONS[-1]
A_GROUP = 4
DIFF_QK = 64
B_BT = 256
B_TQ = 512
B_TK = 512
B_NEAR = 2048 // B_BT + 1
CHUNK = 64
DELTA_TILE = 256
HGRN_TILE = 256
SUB = 16
CONV_K = 4
NUM_BUCKETS = 32
MAX_DISTANCE = 2048
RMS_EPS = 1e-6
NEG = -1e30
VMEM_LIMIT = 56 * 1024 * 1024
MM_TM, MM_TN = 2048, 512
MERGE_TM = 256

PF_CQKV, PF_CZ, PF_AGATE, PF_BGATE, PF_DQ, PF_DF, PF_DI, PF_DGATE, PF_AQ, PF_AK, PF_AV = 0, 3, 4, 5, 6, 7, 8, 9, 10, 11, 12
PF_COLS = 13 * 512


def _cparams(sem):
    return pltpu.CompilerParams(dimension_semantics=sem, vmem_limit_bytes=VMEM_LIMIT)


def _nt(a, b, precision=None):
    return lax.dot_general(a, b, (((1,), (1,)), ((), ())), precision=precision,
                           preferred_element_type=F32)


def _tn(a, b, precision=None):
    return lax.dot_general(a, b, (((0,), (0,)), ((), ())), precision=precision,
                           preferred_element_type=F32)


def _nn(a, b, precision=None):
    return jnp.dot(a, b, precision=precision, preferred_element_type=F32)


def _sigmoid(x):
    return 0.5 * jnp.tanh(0.5 * x) + 0.5


def _silu(x):
    return x * _sigmoid(x)


def _log1p_exp_neg_abs(x):
    return jnp.log1p(jnp.exp(-jnp.abs(x)))


def _hi_lo(x):
    hi = x.astype(BF16)
    return hi, (x - hi.astype(F32)).astype(BF16)


def _split_cols(a):
    hi = a.astype(BF16).astype(F32)
    hl = jnp.concatenate([hi, a - hi], axis=1).astype(BF16)
    return jnp.concatenate([hl, hl], axis=1)


def _split_rows(b):
    hi, lo = _hi_lo(b)
    return jnp.concatenate([hi, hi, lo, lo], axis=0)


def _split_rows_lhs(a):
    hi, lo = _hi_lo(a)
    return jnp.concatenate([hi, lo, hi, lo], axis=0)


def _nt_acc(a, b):
    ah, al = _hi_lo(a)
    bh, bl = _hi_lo(b)
    return (_nt(jnp.concatenate([ah, al], axis=1), jnp.concatenate([bh, bh], axis=1))
            + _nt(ah, bl))


def _rmsnorm_kernel(x_ref, g_ref, o_ref):
    x = x_ref[...]
    ms = jnp.mean(x * x, axis=-1, keepdims=True)
    o_ref[...] = (x * lax.rsqrt(ms + RMS_EPS) * g_ref[...]).astype(o_ref.dtype)


def _rmsnorm(x, gain, out_dtype, tm=512):
    n, d = x.shape
    return pl.pallas_call(
        _rmsnorm_kernel,
        grid=(n // tm,),
        in_specs=[pl.BlockSpec((tm, d), lambda i: (i, 0)),
                  pl.BlockSpec((1, d), lambda i: (0, 0))],
        out_specs=pl.BlockSpec((tm, d), lambda i: (i, 0)),
        out_shape=jax.ShapeDtypeStruct((n, d), out_dtype),
        compiler_params=_cparams(("parallel",)),
        name="rmsnorm",
    )(x, gain.reshape(1, d).astype(F32))


def _mm_kernel(a_ref, w_ref, o_ref, *, act):
    acc = _nn(a_ref[...], w_ref[...])
    if act == "sigmoid":
        acc = _sigmoid(acc)
    o_ref[...] = acc.astype(o_ref.dtype)


def _mm(a, w, out_dtype, act=None, tm=MM_TM, tn=MM_TN, name="mm"):
    n, k = a.shape
    c = w.shape[1]
    tn = min(tn, c)
    return pl.pallas_call(
        functools.partial(_mm_kernel, act=act),
        grid=(n // tm, c // tn),
        in_specs=[pl.BlockSpec((tm, k), lambda i, j: (i, 0)),
                  pl.BlockSpec((k, tn), lambda i, j: (0, j))],
        out_specs=pl.BlockSpec((tm, tn), lambda i, j: (i, j)),
        out_shape=jax.ShapeDtypeStruct((n, c), out_dtype),
        compiler_params=_cparams(("parallel", "parallel")),
        name=name,
    )(a, w)


def _merge_out_kernel(g0, g1, g2, g3, y0, y1, y2, y3, wbr_ref, wout_ref, x_ref, gain_ref, *out_refs,
                      last):
    mixed = g0[...].astype(F32) * _nn(y0[...], wbr_ref[0])
    mixed += g1[...].astype(F32) * _nn(y1[...], wbr_ref[1])
    mixed += g2[...].astype(F32) * _nn(y2[...], wbr_ref[2])
    mixed += g3[...].astype(F32) * _nn(y3[...], wbr_ref[3])
    x = x_ref[...] + _nn(mixed.astype(BF16), wout_ref[...])
    ms = jnp.mean(x * x, axis=-1, keepdims=True)
    normed = x * lax.rsqrt(ms + RMS_EPS) * gain_ref[...]
    if last:
        out_refs[0][...] = normed
    else:
        out_refs[0][...] = x
        out_refs[1][...] = normed.astype(BF16)


def _merge_out(gates, ys, w_br, w_out, x, next_gain, last, tm=MERGE_TM):
    n = gates.shape[0]
    gate_specs = [pl.BlockSpec((tm, D_MODEL), functools.partial(lambda i, b: (i, b), b=b))
                  for b in range(4)]
    y_specs = [pl.BlockSpec((tm, BRANCH_W), lambda i: (i, 0)) for _ in range(4)]
    resident = dict(pipeline_mode=pl.Buffered(1))
    row_spec = pl.BlockSpec((tm, D_MODEL), lambda i: (i, 0))
    if last:
        out_specs, out_shape = row_spec, jax.ShapeDtypeStruct((n, D_MODEL), F32)
    else:
        out_specs = (row_spec, row_spec)
        out_shape = (jax.ShapeDtypeStruct((n, D_MODEL), F32), jax.ShapeDtypeStruct((n, D_MODEL), BF16))
    return pl.pallas_call(
        functools.partial(_merge_out_kernel, last=last),
        grid=(n // tm,),
        in_specs=gate_specs + y_specs + [
            pl.BlockSpec((4, BRANCH_W, D_MODEL), lambda i: (0, 0, 0), **resident),
            pl.BlockSpec((D_MODEL, D_MODEL), lambda i: (0, 0), **resident),
            row_spec,
            pl.BlockSpec((1, D_MODEL), lambda i: (0, 0))],
        out_specs=out_specs,
        out_shape=out_shape,
        compiler_params=_cparams(("parallel",)),
        name="merge_out",
    )(gates, gates, gates, gates, *ys, w_br, w_out, x, next_gain.reshape(1, D_MODEL).astype(F32))


def _t5_bucket(dist):
    n = jnp.maximum(dist, 0)
    max_exact = NUM_BUCKETS // 2
    nf = jnp.maximum(n, max_exact).astype(F32)
    large = max_exact + (jnp.log(nf / max_exact) / math.log(MAX_DISTANCE / max_exact)
                         * (NUM_BUCKETS - max_exact)).astype(jnp.int32)
    large = jnp.minimum(large, NUM_BUCKETS - 1)
    return jnp.where(n < max_exact, n, large)


def _bucket_lookup(table, bucket):
    tab = table.T.reshape((table.shape[1],) + (1,) * bucket.ndim + (NUM_BUCKETS,))
    out = jnp.zeros((table.shape[1],) + bucket.shape, F32)
    for b in range(NUM_BUCKETS):
        out = jnp.where(bucket[None] == b, tab[..., b], out)
    return out


def _dilated_bias(bias_a):
    qi = jnp.arange(BAND)[:, None]
    kj = jnp.arange(2 * BAND)[None, :]
    rel = qi + BAND - kj
    valid = (rel >= 0) & (rel <= BAND)
    tabs = []
    for dil in DILATIONS:
        b = _bucket_lookup(bias_a, _t5_bucket(rel * dil))
        tabs.append(jnp.where(valid[None], b, NEG))
    return jnp.stack(tabs)


def _diff_bias(bias_b):
    qi = jnp.arange(B_BT)[:, None]
    kj = jnp.arange(B_BT)[None, :]
    tabs = []
    for d in range(-1, B_NEAR + 1):
        rel = d * B_BT + qi - kj
        b = _bucket_lookup(bias_b, _t5_bucket(rel))
        tabs.append(jnp.where((rel >= 0)[None], b, NEG))
    return jnp.stack(tabs, axis=1)


def _dil_attn_kernel(q_ref, kp_ref, kc_ref, vp_ref, vc_ref, gate_ref, bias_ref, o_ref,
                     kcat, vcat, acc_ref, m_ref, l_ref):
    n = pl.program_id(1)
    kcat[0:A_TILE, :] = kp_ref[0]
    kcat[A_TILE:2 * A_TILE, :] = kc_ref[0]
    vcat[0:A_TILE, :] = vp_ref[0]
    vcat[A_TILE:2 * A_TILE, :] = vc_ref[0]
    lane = lax.broadcasted_iota(jnp.int32, (BAND, HEAD_W), 1)
    lo = lane < A_HEAD_DIM
    col = lax.broadcasted_iota(jnp.int32, (BAND, 2 * BAND), 1)
    nblk_tile = A_TILE // BAND

    first = len(DILATIONS) - 1
    for p, r in reversed(list(enumerate(DILATIONS))):
        def rows(start, r=r):
            return pl.ds(start, BAND, stride=r) if r > 1 else pl.ds(pl.multiple_of(start, BAND), BAND)

        def body(it, carry, p=p, r=r, rows=rows):
            starts, no_prevs, vvs = [], [], []
            ss, es, ms, ls, outs = {}, {}, {}, {}, {}
            for g in range(A_GROUP):
                idx = it * A_GROUP + g
                j = idx // r
                start = j * (BAND * r) + idx % r
                starts.append(start)
                no_prevs.append(jnp.logical_and(n == 0, j == 0))
                q = (q_ref[0, rows(start), :] * (A_HEAD_DIM ** -0.5)).astype(BF16)
                kk = jnp.concatenate([kcat[rows(A_TILE + start - BAND * r), :],
                                      kcat[rows(A_TILE + start), :]], axis=0).astype(BF16)
                vvs.append(jnp.concatenate([vcat[rows(A_TILE + start - BAND * r), :],
                                            vcat[rows(A_TILE + start), :]], axis=0).astype(BF16))
                for hh in range(2):
                    qz = jnp.where(lo if hh == 0 else jnp.logical_not(lo), q, jnp.zeros_like(q))
                    ss[g, hh] = _nt(qz, kk)
            for g in range(A_GROUP):
                for hh in range(2):
                    s = ss[g, hh] + bias_ref[p, hh]
                    s = jnp.where(jnp.logical_and(no_prevs[g], col < BAND), NEG, s)
                    m = jnp.max(s, axis=-1, keepdims=True)
                    e = jnp.exp(s - m)
                    ms[g, hh] = jnp.broadcast_to(m, (BAND, HEAD_W))
                    ls[g, hh] = jnp.broadcast_to(jnp.sum(e, axis=-1, keepdims=True), (BAND, HEAD_W))
                    es[g, hh] = e.astype(BF16)
            for g in range(A_GROUP):
                for hh in range(2):
                    outs[g, hh] = _nn(es[g, hh], vvs[g])
            merged = []
            for g in range(A_GROUP):
                o_new = jnp.where(lo, outs[g, 0], outs[g, 1])
                m_new = jnp.where(lo, ms[g, 0], ms[g, 1])
                l_new = jnp.where(lo, ls[g, 0], ls[g, 1])
                if p != first:
                    rws = rows(starts[g])
                    m_old = m_ref[rws, :]
                    m_tot = jnp.maximum(m_old, m_new)
                    a = jnp.exp(m_old - m_tot)
                    b = jnp.exp(m_new - m_tot)
                    o_new = a * acc_ref[rws, :] + b * o_new
                    l_new = a * l_ref[rws, :] + b * l_new
                    m_new = m_tot
                merged.append((o_new, m_new, l_new))
            for g in range(A_GROUP):
                rws = rows(starts[g])
                acc_ref[rws, :], m_ref[rws, :], l_ref[rws, :] = merged[g]
            return carry

        lax.fori_loop(0, nblk_tile // A_GROUP, body, 0)

    o_ref[0] = (acc_ref[...] / l_ref[...] * _silu(gate_ref[0])).astype(o_ref.dtype)


def _dil_attn(pf, bias_tab, bsz, t):
    nt = t // A_TILE
    hp = BRANCH_W // HEAD_W
    blk = (1, A_TILE, HEAD_W)

    def cur(cb):
        return pl.BlockSpec(blk, lambda b, n, h: (b, n, cb * hp + h))

    def prev(cb):
        return pl.BlockSpec(blk, lambda b, n, h: (b, jnp.maximum(n - 1, 0), cb * hp + h))

    return pl.pallas_call(
        _dil_attn_kernel,
        grid=(bsz, nt, hp),
        in_specs=[cur(PF_AQ), prev(PF_AK), cur(PF_AK), prev(PF_AV), cur(PF_AV), cur(PF_AGATE),
                  pl.BlockSpec((3, 2, BAND, 2 * BAND), lambda b, n, h: (0, h, 0, 0))],
        out_specs=pl.BlockSpec(blk, lambda b, n, h: (b, n, h)),
        out_shape=jax.ShapeDtypeStruct((bsz, t, BRANCH_W), BF16),
        scratch_shapes=[pltpu.VMEM((2 * A_TILE, HEAD_W), F32), pltpu.VMEM((2 * A_TILE, HEAD_W), F32),
                        pltpu.VMEM((A_TILE, HEAD_W), F32), pltpu.VMEM((A_TILE, HEAD_W), F32),
                        pltpu.VMEM((A_TILE, HEAD_W), F32)],
        compiler_params=_cparams(("parallel", "parallel", "parallel")),
        name="dilated_attn",
    )(pf, pf, pf, pf, pf, pf, bias_tab)


def _diff_attn_kernel(scal_ref, q_ref, k_ref, v_ref, gate_ref, bias_ref, gain_ref, o_ref,
                      kt_ref, vx_ref, s0_ref, s1_ref, p0_ref, p1_ref, al0_ref, al1_ref,
                      acc_ref, m_ref):
    h = pl.program_id(1)
    qi = pl.program_id(2)
    nkb = kt_ref.shape[0]
    nrh = B_TQ // B_BT
    nsub = B_TK // B_BT
    ncol = B_TK // HEAD_W

    @pl.when(qi == 0)
    def _():
        def tr(kb, c):
            rows = pl.ds(pl.multiple_of(kb * B_TK, B_TK), B_TK)
            kt_ref[kb] = k_ref[0, rows, :].astype(F32).T.astype(BF16)
            vx_ref[rows, 0:HEAD_W] = v_ref[0, rows, :]
            vx_ref[rows, HEAD_W:2 * HEAD_W] = jnp.ones((B_TK, HEAD_W), BF16)
            return c
        lax.fori_loop(0, nkb, tr, 0)

    lane = lax.broadcasted_iota(jnp.int32, (B_TQ, HEAD_W), 1)
    lo = lane < DIFF_QK
    q = q_ref[0] * (DIFF_QK ** -0.5)
    zero = jnp.zeros_like(q)
    qz = (jnp.where(lo, q, zero), jnp.where(lo, zero, q))
    cfar = scal_ref[2 + h]
    nk = ((qi + 1) * nrh - 1) // nsub + 1
    n_far = jnp.maximum((nrh * qi - B_NEAR + 1) // nsub, 0)
    per_sub = B_BT // HEAD_W

    s_bufs, p_bufs, al_bufs = (s0_ref, s1_ref), (p0_ref, p1_ref), (al0_ref, al1_ref)
    acc_ref[...] = jnp.zeros_like(acc_ref)
    m_ref[...] = jnp.full_like(m_ref, NEG)
    al1_ref[...] = jnp.ones_like(al1_ref)
    p1_ref[...] = jnp.zeros_like(p1_ref)

    row_halves = [slice(rh * B_BT, (rh + 1) * B_BT) for rh in range(nrh)]

    def qk(kb, par):
        kt = kt_ref[jnp.minimum(kb, nk - 1)]
        for mi in range(2):
            for rows in row_halves:
                s_bufs[par][mi, rows, :] = _nn(qz[mi][rows], kt)

    def pv(kb, par):
        kb = jnp.clip(kb, 0, nk - 1)
        vx = vx_ref[pl.ds(pl.multiple_of(kb * B_TK, B_TK), B_TK), :]
        for mi in range(2):
            for rows in row_halves:
                a = al_bufs[par][mi, rows]
                upd = _nn(p_bufs[par][mi, rows, :], vx)
                acc_ref[mi, rows] = jnp.concatenate([a, a], axis=1) * acc_ref[mi, rows] + upd

    def col(par, mi, c, kb, near):
        x = s_bufs[par][mi, :, c * HEAD_W:(c + 1) * HEAD_W]
        if near:
            cc = (c % per_sub) * HEAD_W
            tiles = [jnp.clip(nrh * qi + rh - nsub * kb - c // per_sub, -1, B_NEAR) + 1
                     for rh in range(nrh)]
            x = x + jnp.concatenate([bias_ref[0, tl, :, cc:cc + HEAD_W] for tl in tiles], axis=0)
        return x

    def softmax(kb, par, near):
        shift = 0.0 if near else cfar
        m_sub = []
        for mi in range(2):
            mx = col(par, mi, 0, kb, near)
            for c in range(1, ncol):
                mx = jnp.maximum(mx, col(par, mi, c, kb, near))
            m_prev = m_ref[mi]
            m_next = jnp.maximum(m_prev, jnp.max(mx, axis=-1, keepdims=True) + shift)
            al_bufs[par][mi] = jnp.exp(m_prev - m_next)
            m_ref[mi] = m_next
            m_sub.append(m_next - shift)
        for mi in range(2):
            for c in range(ncol):
                e = jnp.exp(col(par, mi, c, kb, near) - m_sub[mi])
                p_bufs[par][mi, :, c * HEAD_W:(c + 1) * HEAD_W] = e.astype(BF16)

    def pair(u, near):
        for par in range(2):
            t = 2 * u + par
            qk(t + 1, 1 - par)
            pv(t - 1, 1 - par)
            softmax(t, par, near)

    def far_body(u, c):
        pair(u, False)
        return c

    def near_body(u, c):
        pair(u, True)
        return c

    far_pairs = n_far // 2
    all_pairs = (nk + 1) // 2
    qk(0, 0)
    lax.fori_loop(0, far_pairs, far_body, 0)
    lax.fori_loop(far_pairs, all_pairs, near_body, 0)
    pv(2 * all_pairs - 1, 1)

    lam = scal_ref[0]
    a1, a2 = acc_ref[0], acc_ref[1]
    o = a1[:, :HEAD_W] / a1[:, HEAD_W:] - lam * (a2[:, :HEAD_W] / a2[:, HEAD_W:])
    ms = jnp.mean(o * o, axis=-1, keepdims=True)
    o = o * lax.rsqrt(ms + RMS_EPS) * gain_ref[...] * scal_ref[1]
    o_ref[0] = (o * _silu(gate_ref[0])).astype(o_ref.dtype)


def _diff_attn(pb, pf, bias_tab, scal, gain, bsz, t):
    nh = BRANCH_W // HEAD_W
    nq = t // B_TQ
    return pl.pallas_call(
        _diff_attn_kernel,
        grid=(bsz, nh, nq),
        in_specs=[pl.BlockSpec(memory_space=pltpu.SMEM),
                  pl.BlockSpec((1, B_TQ, HEAD_W), lambda b, h, i: (b, i, h)),
                  pl.BlockSpec((1, t, HEAD_W), lambda b, h, i: (b, 0, nh + h)),
                  pl.BlockSpec((1, t, HEAD_W), lambda b, h, i: (b, 0, 2 * nh + h)),
                  pl.BlockSpec((1, B_TQ, HEAD_W), lambda b, h, i: (b, i, PF_BGATE * nh + h)),
                  pl.BlockSpec((1, B_NEAR + 2, B_BT, B_BT), lambda b, h, i: (h, 0, 0, 0)),
                  pl.BlockSpec((1, HEAD_W), lambda b, h, i: (0, 0))],
        out_specs=pl.BlockSpec((1, B_TQ, HEAD_W), lambda b, h, i: (b, i, h)),
        out_shape=jax.ShapeDtypeStruct((bsz, t, BRANCH_W), BF16),
        scratch_shapes=[pltpu.VMEM((t // B_TK, HEAD_W, B_TK), BF16),
                        pltpu.VMEM((t, 2 * HEAD_W), BF16),
                        pltpu.VMEM((2, B_TQ, B_TK), F32),
                        pltpu.VMEM((2, B_TQ, B_TK), F32),
                        pltpu.VMEM((2, B_TQ, B_TK), BF16),
                        pltpu.VMEM((2, B_TQ, B_TK), BF16),
                        pltpu.VMEM((2, B_TQ, HEAD_W), F32),
                        pltpu.VMEM((2, B_TQ, HEAD_W), F32),
                        pltpu.VMEM((2, B_TQ, 2 * HEAD_W), F32),
                        pltpu.VMEM((2, B_TQ, HEAD_W), F32)],
        compiler_params=_cparams(("arbitrary", "arbitrary", "arbitrary")),
        name="diff_attn",
    )(scal, pb, pb, pb, pf, bias_tab, gain.reshape(1, HEAD_W).astype(F32))


def _delta_kernel(qkv_ref, z_ref, small_ref, conv_ref, par_ref, gain_ref, o_ref,
                  xe_ref, s_ref):
    tc = qkv_ref.shape[1]
    nchunk = tc // CHUNK

    @pl.when(pl.program_id(1) == 0)
    def _():
        xe_ref[0:8, :] = jnp.zeros((8, xe_ref.shape[1]), F32)
        s_ref[...] = jnp.zeros_like(s_ref)

    xe_ref[8:8 + tc, :] = qkv_ref[0]
    conv = conv_ref[0:1, :] * xe_ref[pl.ds(8 - 3, tc), :]
    for kk in range(1, CONV_K):
        conv = conv + conv_ref[kk:kk + 1, :] * xe_ref[pl.ds(8 - 3 + kk, tc), :]
    xe_ref[0:8, :] = xe_ref[tc:tc + 8, :]
    c = _silu(conv)

    small = small_ref[0]
    beta_all = _sigmoid(small)
    xa = small + par_ref[1:2, :]
    softplus = jnp.maximum(xa, 0.0) + _log1p_exp_neg_abs(xa)
    g_all = -jnp.exp(par_ref[0:1, :]) * softplus

    ri = lax.broadcasted_iota(jnp.int32, (CHUNK, CHUNK), 0)
    ci = lax.broadcasted_iota(jnp.int32, (CHUNK, CHUNK), 1)
    tri = ri >= ci
    strict = ri > ci
    eye = (ri == ci).astype(F32)
    rt = lax.broadcasted_iota(jnp.int32, (tc, tc), 0)
    ct = lax.broadcasted_iota(jnp.int32, (tc, tc), 1)
    tri_blocks = jnp.logical_and(rt >= ct, rt // CHUNK == ct // CHUNK).astype(F32)
    sel = (lax.broadcasted_iota(jnp.int32, (8, HEAD_W), 1)
           == lax.broadcasted_iota(jnp.int32, (8, HEAD_W), 0) + 4).astype(F32)
    nh = BRANCH_W // HEAD_W
    units = [(ch, h) for ch in range(nchunk) for h in range(nh)]

    gcum_all = _nn(tri_blocks, g_all, HI)
    grow_all = _nt(sel, gcum_all, HI)

    qs, ks, vs, bcols, gcols, glasts, gammas, xs = {}, {}, {}, {}, {}, {}, {}, {}
    for h in range(nh):
        qf = c[:, h * HEAD_W:(h + 1) * HEAD_W]
        kf = c[:, BRANCH_W + h * HEAD_W:BRANCH_W + (h + 1) * HEAD_W]
        qf = qf * lax.rsqrt(jnp.sum(qf * qf, axis=-1, keepdims=True) + 1e-6) * (HEAD_W ** -0.5)
        kf = kf * lax.rsqrt(jnp.sum(kf * kf, axis=-1, keepdims=True) + 1e-6)
        for ch in range(nchunk):
            r0 = ch * CHUNK
            u_ = (ch, h)
            qs[u_] = qf[r0:r0 + CHUNK]
            ks[u_] = kf[r0:r0 + CHUNK]
            vs[u_] = c[r0:r0 + CHUNK, 2 * BRANCH_W + h * HEAD_W:2 * BRANCH_W + (h + 1) * HEAD_W]
            bcols[u_] = beta_all[r0:r0 + CHUNK, h:h + 1]
            gcols[u_] = gcum_all[r0:r0 + CHUNK, 4 + h:5 + h]
            glasts[u_] = gcum_all[r0 + CHUNK - 1:r0 + CHUNK, 4 + h:5 + h]
            grow = grow_all[h:h + 1, r0:r0 + CHUNK]
            gammas[u_] = jnp.exp(jnp.where(tri, gcols[u_] - grow, NEG))
    for u_ in units:
        kkt = _nt_acc(ks[u_], ks[u_])
        xs[u_] = jnp.where(strict, -(bcols[u_] * kkt * gammas[u_]), 0.0)

    tinv = {u_: eye + xs[u_] for u_ in units}
    pw = xs
    pw_rows = {u_: _split_rows(pw[u_]) for u_ in units}
    for _ in range(5):
        pw = {u_: _nn(_split_cols(pw[u_]), pw_rows[u_]) for u_ in units}
        pw_rows = {u_: _split_rows(pw[u_]) for u_ in units}
        tinv = {u_: tinv[u_] + _nn(_split_cols(tinv[u_]), pw_rows[u_]) for u_ in units}

    us, ws, aqks, qds, kds = {}, {}, {}, {}, {}
    for u_ in units:
        egc = jnp.exp(gcols[u_])
        rhs = jnp.concatenate([vs[u_] * bcols[u_], ks[u_] * (bcols[u_] * egc)], axis=1)
        uw = _nn(_split_cols(tinv[u_]), _split_rows(rhs))
        us[u_], ws[u_] = uw[:, :HEAD_W], uw[:, HEAD_W:]
        aqks[u_] = _nt(qs[u_].astype(BF16), ks[u_].astype(BF16)) * gammas[u_]
        qds[u_] = (qs[u_] * egc).astype(BF16)
        kds[u_] = ks[u_] * jnp.exp(glasts[u_] - gcols[u_])

    for ch in range(nchunk):
        r0 = ch * CHUNK
        for h in range(nh):
            u_ = (ch, h)
            s = s_ref[h]
            s_hi = s.astype(BF16)
            s_lo = (s - s_hi.astype(F32)).astype(BF16)
            w_hi = ws[u_].astype(BF16)
            w_lo = (ws[u_] - w_hi.astype(F32)).astype(BF16)
            ws_prod = (_nn(jnp.concatenate([w_hi, w_lo], axis=1), jnp.concatenate([s_hi, s_hi], axis=0))
                       + _nn(w_hi, s_lo))
            v_new = us[u_] - ws_prod
            o = _nn(qds[u_], s_hi) + _nn(aqks[u_].astype(BF16), v_new.astype(BF16))
            s_ref[h] = jnp.exp(glasts[u_]) * s + _tn(_split_rows_lhs(kds[u_]), _split_rows(v_new))
            ms = jnp.mean(o * o, axis=-1, keepdims=True)
            o = o * lax.rsqrt(ms + RMS_EPS) * gain_ref[...]
            zg = z_ref[0, r0:r0 + CHUNK, h * HEAD_W:(h + 1) * HEAD_W]
            o_ref[0, r0:r0 + CHUNK, h * HEAD_W:(h + 1) * HEAD_W] = (o * _silu(zg)).astype(o_ref.dtype)


def _delta_net(pf, small, conv_w, par, gain, bsz, t, tc=CHUNK):
    cw = 3 * BRANCH_W
    return pl.pallas_call(
        _delta_kernel,
        grid=(bsz, t // tc),
        in_specs=[pl.BlockSpec((1, tc, cw), lambda b, i: (b, i, 0)),
                  pl.BlockSpec((1, tc, BRANCH_W), lambda b, i: (b, i, PF_CZ)),
                  pl.BlockSpec((1, tc, HEAD_W), lambda b, i: (b, i, 0)),
                  pl.BlockSpec((8, cw), lambda b, i: (0, 0)),
                  pl.BlockSpec((8, HEAD_W), lambda b, i: (0, 0)),
                  pl.BlockSpec((1, HEAD_W), lambda b, i: (0, 0))],
        out_specs=pl.BlockSpec((1, tc, BRANCH_W), lambda b, i: (b, i, 0)),
        out_shape=jax.ShapeDtypeStruct((bsz, t, BRANCH_W), BF16),
        scratch_shapes=[pltpu.VMEM((tc + 8, cw), F32),
                        pltpu.VMEM((BRANCH_W // HEAD_W, HEAD_W, HEAD_W), F32)],
        compiler_params=_cparams(("arbitrary", "arbitrary")),
        name="delta_net",
    )(pf, pf, small, conv_w, par, gain.reshape(1, HEAD_W).astype(F32))


def _hgrn_kernel(q_ref, f_ref, i_ref, gate_ref, lb_ref, gain_ref, o_ref, st_ref):
    tc = q_ref.shape[1]
    nchunk = tc // CHUNK
    nh = BRANCH_W // HEAD_W
    nsub = CHUNK // SUB

    @pl.when(pl.program_id(1) == 0)
    def _():
        st_ref[...] = jnp.zeros_like(st_ref)

    lb = lb_ref[...]
    df = f_ref[0]
    log_sig = jnp.minimum(df, 0.0) - _log1p_exp_neg_abs(df)
    a = jnp.log(lb)
    b = jnp.log1p(-lb) + log_sig
    logf_all = jnp.maximum(a, b) + _log1p_exp_neg_abs(a - b)
    k_all = (1.0 - lb) * _sigmoid(-df)

    ri = lax.broadcasted_iota(jnp.int32, (CHUNK, 3 * CHUNK), 0)
    ci = lax.broadcasted_iota(jnp.int32, (CHUNK, 3 * CHUNK), 1)
    tri3 = jnp.where(ri >= ci % CHUNK, 1.0, 0.0).astype(BF16)
    sub_row = lax.broadcasted_iota(jnp.int32, (SUB, HEAD_W), 0)
    units = [(ch, h) for ch in range(nchunk) for h in range(nh)]

    bcs = []
    for ch in range(nchunk):
        lf = logf_all[ch * CHUNK:(ch + 1) * CHUNK]
        hi = lf.astype(BF16)
        r1 = lf - hi.astype(F32)
        mid = r1.astype(BF16)
        lo = (r1 - mid.astype(F32)).astype(BF16)
        bcs.append(_nn(tri3, jnp.concatenate([hi, mid, lo], axis=0)))

    qs, ks, vs, bcu, bls, qes, dstate, intra = {}, {}, {}, {}, {}, {}, {}, {}
    for u_ in units:
        ch, h = u_
        r0 = ch * CHUNK
        cs = slice(h * HEAD_W, (h + 1) * HEAD_W)
        qs[u_] = q_ref[0, r0:r0 + CHUNK, cs]
        vs[u_] = i_ref[0, r0:r0 + CHUNK, cs]
        ks[u_] = k_all[r0:r0 + CHUNK, cs]
        bcu[u_] = bcs[ch][:, cs]
        bls[u_] = bcu[u_][CHUNK - 1:CHUNK, :]
        qes[u_] = (qs[u_] * jnp.exp(bcu[u_])).astype(BF16)
        k_end = ks[u_] * jnp.exp(bls[u_] - bcu[u_])
        dstate[u_] = _tn(_split_rows_lhs(vs[u_]), _split_rows(k_end))

    for u_ in units:
        q, k, v, bc = qs[u_], ks[u_], vs[u_], bcu[u_]
        pieces = []
        for si in range(nsub):
            rs = slice(si * SUB, (si + 1) * SUB)
            q_s, bc_s = q[rs], bc[rs]
            if si > 0:
                ref_row = bc[si * SUB - 1:si * SUB, :]
                q_dec = (q_s * jnp.exp(bc_s - ref_row)).astype(BF16)
                k_dec = (k[:si * SUB] * jnp.exp(ref_row - bc[:si * SUB])).astype(BF16)
                o_s = _nn(_nt(q_dec, k_dec).astype(BF16), v[:si * SUB].astype(BF16))
            else:
                o_s = jnp.zeros((SUB, HEAD_W), F32)
            for j in range(SUB):
                jj = si * SUB + j
                e = jnp.exp(jnp.where(sub_row >= j, bc_s - bc[jj:jj + 1, :], NEG))
                a_col = jnp.sum(q_s * k[jj:jj + 1, :] * e, axis=-1, keepdims=True)
                o_s = o_s + a_col * v[jj:jj + 1, :]
            pieces.append(o_s)
        intra[u_] = jnp.concatenate(pieces, axis=0)

    for u_ in units:
        ch, h = u_
        r0 = ch * CHUNK
        cs = slice(h * HEAD_W, (h + 1) * HEAD_W)
        st = st_ref[h]
        o = _nt(qes[u_], st.astype(BF16)) + intra[u_]
        st_ref[h] = st * jnp.exp(bls[u_]) + dstate[u_]
        ms = jnp.mean(o * o, axis=-1, keepdims=True)
        o = o * lax.rsqrt(ms + RMS_EPS) * gain_ref[...]
        o_ref[0, r0:r0 + CHUNK, cs] = (o * _silu(gate_ref[0, r0:r0 + CHUNK, cs])).astype(o_ref.dtype)


def _hgrn(pf, lb, gain, bsz, t, tc=CHUNK):
    def spec(cb):
        return pl.BlockSpec((1, tc, BRANCH_W), lambda b, i: (b, i, cb))

    return pl.pallas_call(
        _hgrn_kernel,
        grid=(bsz, t // tc),
        in_specs=[spec(PF_DQ), spec(PF_DF), spec(PF_DI), spec(PF_DGATE),
                  pl.BlockSpec((1, BRANCH_W), lambda b, i: (0, 0)),
                  pl.BlockSpec((1, HEAD_W), lambda b, i: (0, 0))],
        out_specs=pl.BlockSpec((1, tc, BRANCH_W), lambda b, i: (b, i, 0)),
        out_shape=jax.ShapeDtypeStruct((bsz, t, BRANCH_W), BF16),
        scratch_shapes=[pltpu.VMEM((BRANCH_W // HEAD_W, HEAD_W, HEAD_W), F32)],
        compiler_params=_cparams(("arbitrary", "arbitrary")),
        name="hgrn2",
    )(pf, pf, pf, pf, lb.reshape(1, BRANCH_W), gain.reshape(1, HEAD_W).astype(F32))


def _split_w_in(w):
    o = 0
    parts = {}
    for name, width in (("a_q", 512), ("a_k", 512), ("a_v", 512), ("a_gate", 512),
                        ("b_q", 512), ("b_k", 512), ("b_v", 512), ("b_gate", 512),
                        ("c_qkv", 1536), ("c_z", 512), ("c_beta", 4), ("c_a", 4),
                        ("d_q", 512), ("d_f", 512), ("d_i", 512), ("d_gate", 512),
                        ("merge", 4 * D_MODEL)):
        parts[name] = w[:, o:o + width]
        o += width
    w_f = jnp.concatenate([parts[k] for k in ("c_qkv", "c_z", "a_gate", "b_gate", "d_q", "d_f",
                                              "d_i", "d_gate", "a_q", "a_k", "a_v")], axis=1)
    w_b = jnp.concatenate([parts[k] for k in ("b_q", "b_k", "b_v")], axis=1)
    w_s = jnp.concatenate([parts["c_beta"], parts["c_a"],
                           jnp.zeros((w.shape[0], HEAD_W - 8), w.dtype)], axis=1)
    return w_f.astype(BF16), w_b.astype(BF16), w_s.astype(BF16), parts["merge"].astype(BF16)


def kernel(x, norm_gain, w_in, rel_bias, diff_lambda, diff_subln_gain, dn_conv, dn_a_log, dn_dt_bias,
           dn_norm_gain, hg_lb_logits, hg_norm_gain, w_branch, w_out, final_gain):
    bsz, t, d = x.shape
    n = bsz * t
    depth = w_in.shape[0]
    lb_p = jax.nn.softmax(hg_lb_logits.astype(F32), axis=0)
    hg_lb = jnp.clip(jnp.cumsum(lb_p, axis=0) - lb_p[0], 0.0, 1.0)
    bias_a = _dilated_bias(rel_bias[:, :8].astype(F32))
    bias_b = _diff_bias(rel_bias[:, 8:].astype(F32))
    cfar = rel_bias[NUM_BUCKETS - 1, 8:].astype(F32)

    xf = x.reshape(n, d).astype(F32)
    h = _rmsnorm(xf, norm_gain[0], BF16)
    for layer in range(depth):
        w_f, w_b, w_s, w_g = _split_w_in(w_in[layer])
        pf = _mm(h, w_f, F32, name="proj_f32").reshape(bsz, t, PF_COLS)
        pb = _mm(h, w_b, BF16, name="proj_bf16").reshape(bsz, t, 3 * BRANCH_W)
        ps = _mm(h, w_s, F32, name="proj_small").reshape(bsz, t, HEAD_W)
        gates = _mm(h, w_g, BF16, act="sigmoid", name="proj_gates")

        y_a = _dil_attn(pf, bias_a, bsz, t)

        lam_init = 0.8 - 0.6 * math.exp(-0.3 * layer)
        lq1, lk1, lq2, lk2 = diff_lambda[layer].astype(F32)
        lam = jnp.exp(jnp.sum(lq1 * lk1)) - jnp.exp(jnp.sum(lq2 * lk2)) + lam_init
        scal = jnp.concatenate([jnp.stack([lam, jnp.asarray(1.0 - lam_init, F32)]), cfar,
                                jnp.zeros((2,), F32)])
        y_b = _diff_attn(pb, pf, bias_b, scal, diff_subln_gain[layer], bsz, t)

        conv_w = jnp.concatenate([dn_conv[layer].astype(F32),
                                  jnp.zeros((8 - CONV_K, 3 * BRANCH_W), F32)], axis=0)
        par = jnp.zeros((8, HEAD_W), F32)
        par = par.at[0, 4:8].set(dn_a_log[layer].astype(F32)).at[1, 4:8].set(dn_dt_bias[layer].astype(F32))
        y_c = _delta_net(pf, ps, conv_w, par, dn_norm_gain[layer], bsz, t, tc=DELTA_TILE)

        y_d = _hgrn(pf, hg_lb[layer], hg_norm_gain[layer], bsz, t, tc=HGRN_TILE)

        ys = [y.reshape(n, BRANCH_W) for y in (y_a, y_b, y_c, y_d)]
        last = layer == depth - 1
        res = _merge_out(gates, ys, w_branch[layer].astype(BF16), w_out[layer].astype(BF16), xf,
                         final_gain if last else norm_gain[layer + 1], last)
        if last:
            out = res
        else:
            xf, h = res
    return out.reshape(bsz, t, d).astype(x.dtype)
```

```python
import functools
import math

import jax
import jax.numpy as jnp
import numpy as np
from jax import lax
from jax.experimental import pallas as pl
from jax.experimental.pallas import tpu as pltpu

F32 = jnp.float32
BF16 = jnp.bfloat16
HI = lax.Precision.HIGHEST

D_MODEL = 2048
BRANCH_W = 512
HEAD_W = 128
A_HEAD_DIM = 64
DILATIONS = (1, 4, 16)
BAND = 128
A_TILE = BAND * DILATIONS[-1]
A_GROUP = 4
DIFF_QK = 64
B_BT = 256
B_TQ = 512
B_TK = 512
B_NEAR = 2048 // B_BT + 1
CHUNK = 64
DELTA_TILE = 256
HGRN_TILE = 256
SUB = 16
HALF = SUB // 2
CONV_K = 4
NUM_BUCKETS = 32
MAX_DISTANCE = 2048
RMS_EPS = 1e-6
NEG = -1e30
VMEM_LIMIT = 56 * 1024 * 1024
MM_TM, MM_TN = 2048, 512
MERGE_TM = 256

PF_CQKV, PF_CZ, PF_AGATE, PF_BGATE, PF_DQ, PF_DF, PF_DI, PF_DGATE, PF_AQ, PF_AK, PF_AV = 0, 3, 4, 5, 6, 7, 8, 9, 10, 11, 12
PF_COLS = 13 * 512


def _cparams(sem):
    return pltpu.CompilerParams(dimension_semantics=sem, vmem_limit_bytes=VMEM_LIMIT)


def _nt(a, b, precision=None):
    return lax.dot_general(a, b, (((1,), (1,)), ((), ())), precision=precision,
                           preferred_element_type=F32)


def _tn(a, b, precision=None):
    return lax.dot_general(a, b, (((0,), (0,)), ((), ())), precision=precision,
                           preferred_element_type=F32)


def _nn(a, b, precision=None):
    return jnp.dot(a, b, precision=precision, preferred_element_type=F32)


def _sigmoid(x):
    return 0.5 * jnp.tanh(0.5 * x) + 0.5


def _silu(x):
    return x * _sigmoid(x)


def _log1p_exp_neg_abs(x):
    return jnp.log1p(jnp.exp(-jnp.abs(x)))


def _hi_lo(x):
    hi = x.astype(BF16)
    return hi, (x - hi.astype(F32)).astype(BF16)


def _split_cols(a):
    hi = a.astype(BF16).astype(F32)
    hl = jnp.concatenate([hi, a - hi], axis=1).astype(BF16)
    return jnp.concatenate([hl, hl], axis=1)


def _split_rows(b):
    hi, lo = _hi_lo(b)
    return jnp.concatenate([hi, hi, lo, lo], axis=0)


def _split_rows_lhs(a):
    hi, lo = _hi_lo(a)
    return jnp.concatenate([hi, lo, hi, lo], axis=0)


def _nt_acc(a, b):
    ah, al = _hi_lo(a)
    bh, bl = _hi_lo(b)
    return (_nt(jnp.concatenate([ah, al], axis=1), jnp.concatenate([bh, bh], axis=1))
            + _nt(ah, bl))


def _rmsnorm_kernel(x_ref, g_ref, o_ref):
    x = x_ref[...]
    ms = jnp.mean(x * x, axis=-1, keepdims=True)
    o_ref[...] = (x * lax.rsqrt(ms + RMS_EPS) * g_ref[...]).astype(o_ref.dtype)


def _rmsnorm(x, gain, out_dtype, tm=512):
    n, d = x.shape
    return pl.pallas_call(
        _rmsnorm_kernel,
        grid=(n // tm,),
        in_specs=[pl.BlockSpec((tm, d), lambda i: (i, 0)),
                  pl.BlockSpec((1, d), lambda i: (0, 0))],
        out_specs=pl.BlockSpec((tm, d), lambda i: (i, 0)),
        out_shape=jax.ShapeDtypeStruct((n, d), out_dtype),
        compiler_params=_cparams(("parallel",)),
        name="rmsnorm",
    )(x, gain.reshape(1, d).astype(F32))


def _mm_kernel(a_ref, w_ref, o_ref, *, act):
    acc = _nn(a_ref[...], w_ref[...])
    if act == "sigmoid":
        acc = _sigmoid(acc)
    o_ref[...] = acc.astype(o_ref.dtype)


def _mm(a, w, out_dtype, act=None, tm=MM_TM, tn=MM_TN, name="mm"):
    n, k = a.shape
    c = w.shape[1]
    tn = min(tn, c)
    return pl.pallas_call(
        functools.partial(_mm_kernel, act=act),
        grid=(n // tm, c // tn),
        in_specs=[pl.BlockSpec((tm, k), lambda i, j: (i, 0)),
                  pl.BlockSpec((k, tn), lambda i, j: (0, j))],
        out_specs=pl.BlockSpec((tm, tn), lambda i, j: (i, j)),
        out_shape=jax.ShapeDtypeStruct((n, c), out_dtype),
        compiler_params=_cparams(("parallel", "parallel")),
        name=name,
    )(a, w)


def _merge_out_kernel(g0, g1, g2, g3, y0, y1, y2, y3, wbr_ref, wout_ref, x_ref, gain_ref, *out_refs,
                      last):
    mixed = g0[...].astype(F32) * _nn(y0[...], wbr_ref[0])
    mixed += g1[...].astype(F32) * _nn(y1[...], wbr_ref[1])
    mixed += g2[...].astype(F32) * _nn(y2[...], wbr_ref[2])
    mixed += g3[...].astype(F32) * _nn(y3[...], wbr_ref[3])
    x = x_ref[...] + _nn(mixed.astype(BF16), wout_ref[...])
    ms = jnp.mean(x * x, axis=-1, keepdims=True)
    normed = x * lax.rsqrt(ms + RMS_EPS) * gain_ref[...]
    if last:
        out_refs[0][...] = normed
    else:
        out_refs[0][...] = x
        out_refs[1][...] = normed.astype(BF16)


def _merge_out(gates, ys, w_br, w_out, x, next_gain, last, tm=MERGE_TM):
    n = gates.shape[0]
    gate_specs = [pl.BlockSpec((tm, D_MODEL), functools.partial(lambda i, b: (i, b), b=b))
                  for b in range(4)]
    y_specs = [pl.BlockSpec((tm, BRANCH_W), lambda i: (i, 0)) for _ in range(4)]
    resident = dict(pipeline_mode=pl.Buffered(1))
    row_spec = pl.BlockSpec((tm, D_MODEL), lambda i: (i, 0))
    if last:
        out_specs, out_shape = row_spec, jax.ShapeDtypeStruct((n, D_MODEL), F32)
    else:
        out_specs = (row_spec, row_spec)
        out_shape = (jax.ShapeDtypeStruct((n, D_MODEL), F32), jax.ShapeDtypeStruct((n, D_MODEL), BF16))
    return pl.pallas_call(
        functools.partial(_merge_out_kernel, last=last),
        grid=(n // tm,),
        in_specs=gate_specs + y_specs + [
            pl.BlockSpec((4, BRANCH_W, D_MODEL), lambda i: (0, 0, 0), **resident),
            pl.BlockSpec((D_MODEL, D_MODEL), lambda i: (0, 0), **resident),
            row_spec,
            pl.BlockSpec((1, D_MODEL), lambda i: (0, 0))],
        out_specs=out_specs,
        out_shape=out_shape,
        compiler_params=_cparams(("parallel",)),
        name="merge_out",
    )(gates, gates, gates, gates, *ys, w_br, w_out, x, next_gain.reshape(1, D_MODEL).astype(F32))


def _t5_bucket(dist):
    n = jnp.maximum(dist, 0)
    max_exact = NUM_BUCKETS // 2
    nf = jnp.maximum(n, max_exact).astype(F32)
    large = max_exact + (jnp.log(nf / max_exact) / math.log(MAX_DISTANCE / max_exact)
                         * (NUM_BUCKETS - max_exact)).astype(jnp.int32)
    large = jnp.minimum(large, NUM_BUCKETS - 1)
    return jnp.where(n < max_exact, n, large)


def _bucket_lookup(table, bucket):
    tab = table.T.reshape((table.shape[1],) + (1,) * bucket.ndim + (NUM_BUCKETS,))
    out = jnp.zeros((table.shape[1],) + bucket.shape, F32)
    for b in range(NUM_BUCKETS):
        out = jnp.where(bucket[None] == b, tab[..., b], out)
    return out


def _dilated_bias(bias_a):
    qi = jnp.arange(BAND)[:, None]
    kj = jnp.arange(2 * BAND)[None, :]
    rel = qi + BAND - kj
    valid = (rel >= 0) & (rel <= BAND)
    tabs = []
    for dil in DILATIONS:
        b = _bucket_lookup(bias_a, _t5_bucket(rel * dil))
        tabs.append(jnp.where(valid[None], b, NEG))
    return jnp.stack(tabs)


def _diff_bias(bias_b):
    qi = jnp.arange(B_BT)[:, None]
    kj = jnp.arange(B_BT)[None, :]
    tabs = []
    for d in range(-1, B_NEAR + 1):
        rel = d * B_BT + qi - kj
        b = _bucket_lookup(bias_b, _t5_bucket(rel))
        tabs.append(jnp.where((rel >= 0)[None], b, NEG))
    return jnp.stack(tabs, axis=1)


def _dil_attn_kernel(q_ref, kp_ref, kc_ref, vp_ref, vc_ref, gate_ref, bias_ref, o_ref,
                     kcat, vcat, acc_ref, m_ref, l_ref):
    n = pl.program_id(1)
    kcat[0:A_TILE, :] = kp_ref[0]
    kcat[A_TILE:2 * A_TILE, :] = kc_ref[0]
    vcat[0:A_TILE, :] = vp_ref[0]
    vcat[A_TILE:2 * A_TILE, :] = vc_ref[0]
    lane = lax.broadcasted_iota(jnp.int32, (BAND, HEAD_W), 1)
    lo = lane < A_HEAD_DIM
    col = lax.broadcasted_iota(jnp.int32, (BAND, 2 * BAND), 1)
    nblk_tile = A_TILE // BAND

    first = len(DILATIONS) - 1
    for p, r in reversed(list(enumerate(DILATIONS))):
        def rows(start, r=r):
            return pl.ds(start, BAND, stride=r) if r > 1 else pl.ds(pl.multiple_of(start, BAND), BAND)

        def body(it, carry, p=p, r=r, rows=rows):
            starts, no_prevs, vvs = [], [], []
            ss, es, ms, ls, outs = {}, {}, {}, {}, {}
            for g in range(A_GROUP):
                idx = it * A_GROUP + g
                j = idx // r
                start = j * (BAND * r) + idx % r
                starts.append(start)
                no_prevs.append(jnp.logical_and(n == 0, j == 0))
                q = (q_ref[0, rows(start), :] * (A_HEAD_DIM ** -0.5)).astype(BF16)
                kk = jnp.concatenate([kcat[rows(A_TILE + start - BAND * r), :],
                                      kcat[rows(A_TILE + start), :]], axis=0).astype(BF16)
                vvs.append(jnp.concatenate([vcat[rows(A_TILE + start - BAND * r), :],
                                            vcat[rows(A_TILE + start), :]], axis=0).astype(BF16))
                for hh in range(2):
                    qz = jnp.where(lo if hh == 0 else jnp.logical_not(lo), q, jnp.zeros_like(q))
                    ss[g, hh] = _nt(qz, kk)
            for g in range(A_GROUP):
                for hh in range(2):
                    s = ss[g, hh] + bias_ref[p, hh]
                    s = jnp.where(jnp.logical_and(no_prevs[g], col < BAND), NEG, s)
                    m = jnp.max(s, axis=-1, keepdims=True)
                    e = jnp.exp(s - m)
                    ms[g, hh] = jnp.broadcast_to(m, (BAND, HEAD_W))
                    ls[g, hh] = jnp.broadcast_to(jnp.sum(e, axis=-1, keepdims=True), (BAND, HEAD_W))
                    es[g, hh] = e.astype(BF16)
            for g in range(A_GROUP):
                for hh in range(2):
                    outs[g, hh] = _nn(es[g, hh], vvs[g])
            merged = []
            for g in range(A_GROUP):
                o_new = jnp.where(lo, outs[g, 0], outs[g, 1])
                m_new = jnp.where(lo, ms[g, 0], ms[g, 1])
                l_new = jnp.where(lo, ls[g, 0], ls[g, 1])
                if p != first:
                    rws = rows(starts[g])
                    m_old = m_ref[rws, :]
                    m_tot = jnp.maximum(m_old, m_new)
                    a = jnp.exp(m_old - m_tot)
                    b = jnp.exp(m_new - m_tot)
                    o_new = a * acc_ref[rws, :] + b * o_new
                    l_new = a * l_ref[rws, :] + b * l_new
                    m_new = m_tot
                merged.append((o_new, m_new, l_new))
            for g in range(A_GROUP):
                rws = rows(starts[g])
                acc_ref[rws, :], m_ref[rws, :], l_ref[rws, :] = merged[g]
            return carry

        lax.fori_loop(0, nblk_tile // A_GROUP, body, 0)

    o_ref[0] = (acc_ref[...] / l_ref[...] * _silu(gate_ref[0])).astype(o_ref.dtype)


def _dil_attn(pf, bias_tab, bsz, t):
    nt = t // A_TILE
    hp = BRANCH_W // HEAD_W
    blk = (1, A_TILE, HEAD_W)

    def cur(cb):
        return pl.BlockSpec(blk, lambda b, n, h: (b, n, cb * hp + h))

    def prev(cb):
        return pl.BlockSpec(blk, lambda b, n, h: (b, jnp.maximum(n - 1, 0), cb * hp + h))

    return pl.pallas_call(
        _dil_attn_kernel,
        grid=(bsz, nt, hp),
        in_specs=[cur(PF_AQ), prev(PF_AK), cur(PF_AK), prev(PF_AV), cur(PF_AV), cur(PF_AGATE),
                  pl.BlockSpec((3, 2, BAND, 2 * BAND), lambda b, n, h: (0, h, 0, 0))],
        out_specs=pl.BlockSpec(blk, lambda b, n, h: (b, n, h)),
        out_shape=jax.ShapeDtypeStruct((bsz, t, BRANCH_W), BF16),
        scratch_shapes=[pltpu.VMEM((2 * A_TILE, HEAD_W), F32), pltpu.VMEM((2 * A_TILE, HEAD_W), F32),
                        pltpu.VMEM((A_TILE, HEAD_W), F32), pltpu.VMEM((A_TILE, HEAD_W), F32),
                        pltpu.VMEM((A_TILE, HEAD_W), F32)],
        compiler_params=_cparams(("parallel", "parallel", "parallel")),
        name="dilated_attn",
    )(pf, pf, pf, pf, pf, pf, bias_tab)


def _diff_attn_kernel(scal_ref, q_ref, k_ref, v_ref, gate_ref, bias_ref, gain_ref, o_ref,
                      kt_ref, vx_ref, s0_ref, s1_ref, p0_ref, p1_ref, al0_ref, al1_ref,
                      acc_ref, m_ref):
    h = pl.program_id(1)
    qi = pl.program_id(2)
    nkb = kt_ref.shape[0]
    nrh = B_TQ // B_BT
    nsub = B_TK // B_BT
    ncol = B_TK // HEAD_W

    @pl.when(qi == 0)
    def _():
        def tr(kb, c):
            rows = pl.ds(pl.multiple_of(kb * B_TK, B_TK), B_TK)
            kt_ref[kb] = k_ref[0, rows, :].astype(F32).T.astype(BF16)
            vx_ref[rows, 0:HEAD_W] = v_ref[0, rows, :]
            vx_ref[rows, HEAD_W:2 * HEAD_W] = jnp.ones((B_TK, HEAD_W), BF16)
            return c
        lax.fori_loop(0, nkb, tr, 0)

    lane = lax.broadcasted_iota(jnp.int32, (B_TQ, HEAD_W), 1)
    lo = lane < DIFF_QK
    q = q_ref[0] * (DIFF_QK ** -0.5)
    zero = jnp.zeros_like(q)
    qz = (jnp.where(lo, q, zero), jnp.where(lo, zero, q))
    cfar = scal_ref[2 + h]
    nk = ((qi + 1) * nrh - 1) // nsub + 1
    n_far = jnp.maximum((nrh * qi - B_NEAR + 1) // nsub, 0)
    per_sub = B_BT // HEAD_W

    s_bufs, p_bufs, al_bufs = (s0_ref, s1_ref), (p0_ref, p1_ref), (al0_ref, al1_ref)
    acc_ref[...] = jnp.zeros_like(acc_ref)
    m_ref[...] = jnp.full_like(m_ref, NEG)
    al1_ref[...] = jnp.ones_like(al1_ref)
    p1_ref[...] = jnp.zeros_like(p1_ref)

    row_halves = [slice(rh * B_BT, (rh + 1) * B_BT) for rh in range(nrh)]

    def qk(kb, par):
        kt = kt_ref[jnp.minimum(kb, nk - 1)]
        for mi in range(2):
            for rows in row_halves:
                s_bufs[par][mi, rows, :] = _nn(qz[mi][rows], kt)

    def pv(kb, par):
        kb = jnp.clip(kb, 0, nk - 1)
        vx = vx_ref[pl.ds(pl.multiple_of(kb * B_TK, B_TK), B_TK), :]
        for mi in range(2):
            for rows in row_halves:
                a = al_bufs[par][mi, rows]
                upd = _nn(p_bufs[par][mi, rows, :], vx)
                acc_ref[mi, rows] = jnp.concatenate([a, a], axis=1) * acc_ref[mi, rows] + upd

    def col(par, mi, c, kb, near):
        x = s_bufs[par][mi, :, c * HEAD_W:(c + 1) * HEAD_W]
        if near:
            cc = (c % per_sub) * HEAD_W
            tiles = [jnp.clip(nrh * qi + rh - nsub * kb - c // per_sub, -1, B_NEAR) + 1
                     for rh in range(nrh)]
            x = x + jnp.concatenate([bias_ref[0, tl, :, cc:cc + HEAD_W] for tl in tiles], axis=0)
        return x

    def softmax(kb, par, near):
        shift = 0.0 if near else cfar
        m_sub = []
        for mi in range(2):
            mx = col(par, mi, 0, kb, near)
            for c in range(1, ncol):
                mx = jnp.maximum(mx, col(par, mi, c, kb, near))
            m_prev = m_ref[mi]
            m_next = jnp.maximum(m_prev, jnp.max(mx, axis=-1, keepdims=True) + shift)
            al_bufs[par][mi] = jnp.exp(m_prev - m_next)
            m_ref[mi] = m_next
            m_sub.append(m_next - shift)
        for mi in range(2):
            for c in range(ncol):
                e = jnp.exp(col(par, mi, c, kb, near) - m_sub[mi])
                p_bufs[par][mi, :, c * HEAD_W:(c + 1) * HEAD_W] = e.astype(BF16)

    def pair(u, near):
        for par in range(2):
            t = 2 * u + par
            qk(t + 1, 1 - par)
            pv(t - 1, 1 - par)
            softmax(t, par, near)

    def far_body(u, c):
        pair(u, False)
        return c

    def near_body(u, c):
        pair(u, True)
        return c

    far_pairs = n_far // 2
    all_pairs = (nk + 1) // 2
    qk(0, 0)
    lax.fori_loop(0, far_pairs, far_body, 0)
    lax.fori_loop(far_pairs, all_pairs, near_body, 0)
    pv(2 * all_pairs - 1, 1)

    lam = scal_ref[0]
    a1, a2 = acc_ref[0], acc_ref[1]
    o = a1[:, :HEAD_W] / a1[:, HEAD_W:] - lam * (a2[:, :HEAD_W] / a2[:, HEAD_W:])
    ms = jnp.mean(o * o, axis=-1, keepdims=True)
    o = o * lax.rsqrt(ms + RMS_EPS) * gain_ref[...] * scal_ref[1]
    o_ref[0] = (o * _silu(gate_ref[0])).astype(o_ref.dtype)


def _diff_attn(pb, pf, bias_tab, scal, gain, bsz, t):
    nh = BRANCH_W // HEAD_W
    nq = t // B_TQ
    return pl.pallas_call(
        _diff_attn_kernel,
        grid=(bsz, nh, nq),
        in_specs=[pl.BlockSpec(memory_space=pltpu.SMEM),
                  pl.BlockSpec((1, B_TQ, HEAD_W), lambda b, h, i: (b, i, h)),
                  pl.BlockSpec((1, t, HEAD_W), lambda b, h, i: (b, 0, nh + h)),
                  pl.BlockSpec((1, t, HEAD_W), lambda b, h, i: (b, 0, 2 * nh + h)),
                  pl.BlockSpec((1, B_TQ, HEAD_W), lambda b, h, i: (b, i, PF_BGATE * nh + h)),
                  pl.BlockSpec((1, B_NEAR + 2, B_BT, B_BT), lambda b, h, i: (h, 0, 0, 0)),
                  pl.BlockSpec((1, HEAD_W), lambda b, h, i: (0, 0))],
        out_specs=pl.BlockSpec((1, B_TQ, HEAD_W), lambda b, h, i: (b, i, h)),
        out_shape=jax.ShapeDtypeStruct((bsz, t, BRANCH_W), BF16),
        scratch_shapes=[pltpu.VMEM((t // B_TK, HEAD_W, B_TK), BF16),
                        pltpu.VMEM((t, 2 * HEAD_W), BF16),
                        pltpu.VMEM((2, B_TQ, B_TK), F32),
                        pltpu.VMEM((2, B_TQ, B_TK), F32),
                        pltpu.VMEM((2, B_TQ, B_TK), BF16),
                        pltpu.VMEM((2, B_TQ, B_TK), BF16),
                        pltpu.VMEM((2, B_TQ, HEAD_W), F32),
                        pltpu.VMEM((2, B_TQ, HEAD_W), F32),
                        pltpu.VMEM((2, B_TQ, 2 * HEAD_W), F32),
                        pltpu.VMEM((2, B_TQ, HEAD_W), F32)],
        compiler_params=_cparams(("arbitrary", "arbitrary", "arbitrary")),
        name="diff_attn",
    )(scal, pb, pb, pb, pf, bias_tab, gain.reshape(1, HEAD_W).astype(F32))


def _delta_kernel(qkv_ref, z_ref, small_ref, conv_ref, par_ref, gain_ref, o_ref,
                  xe_ref, s_ref):
    tc = qkv_ref.shape[1]
    nchunk = tc // CHUNK

    @pl.when(pl.program_id(1) == 0)
    def _():
        xe_ref[0:8, :] = jnp.zeros((8, xe_ref.shape[1]), F32)
        s_ref[...] = jnp.zeros_like(s_ref)

    xe_ref[8:8 + tc, :] = qkv_ref[0]
    conv = conv_ref[0:1, :] * xe_ref[pl.ds(8 - 3, tc), :]
    for kk in range(1, CONV_K):
        conv = conv + conv_ref[kk:kk + 1, :] * xe_ref[pl.ds(8 - 3 + kk, tc), :]
    xe_ref[0:8, :] = xe_ref[tc:tc + 8, :]
    c = _silu(conv)

    small = small_ref[0]
    beta_all = _sigmoid(small)
    xa = small + par_ref[1:2, :]
    softplus = jnp.maximum(xa, 0.0) + _log1p_exp_neg_abs(xa)
    g_all = -jnp.exp(par_ref[0:1, :]) * softplus

    ri = lax.broadcasted_iota(jnp.int32, (CHUNK, CHUNK), 0)
    ci = lax.broadcasted_iota(jnp.int32, (CHUNK, CHUNK), 1)
    tri = ri >= ci
    strict = ri > ci
    eye = (ri == ci).astype(F32)
    rt = lax.broadcasted_iota(jnp.int32, (tc, tc), 0)
    ct = lax.broadcasted_iota(jnp.int32, (tc, tc), 1)
    tri_blocks = jnp.logical_and(rt >= ct, rt // CHUNK == ct // CHUNK).astype(F32)
    sel = (lax.broadcasted_iota(jnp.int32, (8, HEAD_W), 1)
           == lax.broadcasted_iota(jnp.int32, (8, HEAD_W), 0) + 4).astype(F32)
    nh = BRANCH_W // HEAD_W
    units = [(ch, h) for ch in range(nchunk) for h in range(nh)]

    gcum_all = _nn(tri_blocks, g_all, HI)
    grow_all = _nt(sel, gcum_all, HI)

    qs, ks, vs, bcols, gcols, glasts, gammas, xs = {}, {}, {}, {}, {}, {}, {}, {}
    for h in range(nh):
        qf = c[:, h * HEAD_W:(h + 1) * HEAD_W]
        kf = c[:, BRANCH_W + h * HEAD_W:BRANCH_W + (h + 1) * HEAD_W]
        qf = qf * lax.rsqrt(jnp.sum(qf * qf, axis=-1, keepdims=True) + 1e-6) * (HEAD_W ** -0.5)
        kf = kf * lax.rsqrt(jnp.sum(kf * kf, axis=-1, keepdims=True) + 1e-6)
        for ch in range(nchunk):
            r0 = ch * CHUNK
            u_ = (ch, h)
            qs[u_] = qf[r0:r0 + CHUNK]
            ks[u_] = kf[r0:r0 + CHUNK]
            vs[u_] = c[r0:r0 + CHUNK, 2 * BRANCH_W + h * HEAD_W:2 * BRANCH_W + (h + 1) * HEAD_W]
            bcols[u_] = beta_all[r0:r0 + CHUNK, h:h + 1]
            gcols[u_] = gcum_all[r0:r0 + CHUNK, 4 + h:5 + h]
            glasts[u_] = gcum_all[r0 + CHUNK - 1:r0 + CHUNK, 4 + h:5 + h]
            grow = grow_all[h:h + 1, r0:r0 + CHUNK]
            gammas[u_] = jnp.exp(jnp.where(tri, gcols[u_] - grow, NEG))
    for u_ in units:
        kkt = _nt_acc(ks[u_], ks[u_])
        xs[u_] = jnp.where(strict, -(bcols[u_] * kkt * gammas[u_]), 0.0)

    tinv = {u_: eye + xs[u_] for u_ in units}
    pw = xs
    pw_rows = {u_: _split_rows(pw[u_]) for u_ in units}
    for _ in range(5):
        pw = {u_: _nn(_split_cols(pw[u_]), pw_rows[u_]) for u_ in units}
        pw_rows = {u_: _split_rows(pw[u_]) for u_ in units}
        tinv = {u_: tinv[u_] + _nn(_split_cols(tinv[u_]), pw_rows[u_]) for u_ in units}

    us, ws, aqks, qds, kds = {}, {}, {}, {}, {}
    for u_ in units:
        egc = jnp.exp(gcols[u_])
        rhs = jnp.concatenate([vs[u_] * bcols[u_], ks[u_] * (bcols[u_] * egc)], axis=1)
        uw = _nn(_split_cols(tinv[u_]), _split_rows(rhs))
        us[u_], ws[u_] = uw[:, :HEAD_W], uw[:, HEAD_W:]
        aqks[u_] = _nt(qs[u_].astype(BF16), ks[u_].astype(BF16)) * gammas[u_]
        qds[u_] = (qs[u_] * egc).astype(BF16)
        kds[u_] = ks[u_] * jnp.exp(glasts[u_] - gcols[u_])

    states = [s_ref[h] for h in range(nh)]
    for ch in range(nchunk):
        r0 = ch * CHUNK
        v_news, o_inters = {}, {}
        for h in range(nh):
            u_ = (ch, h)
            s = states[h]
            s_hi = s.astype(BF16)
            s_lo = (s - s_hi.astype(F32)).astype(BF16)
            w_hi = ws[u_].astype(BF16)
            w_lo = (ws[u_] - w_hi.astype(F32)).astype(BF16)
            ws_prod = (_nn(jnp.concatenate([w_hi, w_lo], axis=1), jnp.concatenate([s_hi, s_hi], axis=0))
                       + _nn(w_hi, s_lo))
            v_news[h] = us[u_] - ws_prod
            o_inters[h] = _nn(qds[u_], s_hi)
        for h in range(nh):
            u_ = (ch, h)
            v_new = v_news[h]
            o = o_inters[h] + _nn(aqks[u_].astype(BF16), v_new.astype(BF16))
            states[h] = (jnp.exp(glasts[u_]) * states[h]
                         + _tn(_split_rows_lhs(kds[u_]), _split_rows(v_new)))
            if ch == nchunk - 1:
                s_ref[h] = states[h]
            ms = jnp.mean(o * o, axis=-1, keepdims=True)
            o = o * lax.rsqrt(ms + RMS_EPS) * gain_ref[...]
            zg = z_ref[0, r0:r0 + CHUNK, h * HEAD_W:(h + 1) * HEAD_W]
            o_ref[0, r0:r0 + CHUNK, h * HEAD_W:(h + 1) * HEAD_W] = (o * _silu(zg)).astype(o_ref.dtype)


def _delta_net(pf, small, conv_w, par, gain, bsz, t, tc=CHUNK):
    cw = 3 * BRANCH_W
    return pl.pallas_call(
        _delta_kernel,
        grid=(bsz, t // tc),
        in_specs=[pl.BlockSpec((1, tc, cw), lambda b, i: (b, i, 0)),
                  pl.BlockSpec((1, tc, BRANCH_W), lambda b, i: (b, i, PF_CZ)),
                  pl.BlockSpec((1, tc, HEAD_W), lambda b, i: (b, i, 0)),
                  pl.BlockSpec((8, cw), lambda b, i: (0, 0)),
                  pl.BlockSpec((8, HEAD_W), lambda b, i: (0, 0)),
                  pl.BlockSpec((1, HEAD_W), lambda b, i: (0, 0))],
        out_specs=pl.BlockSpec((1, tc, BRANCH_W), lambda b, i: (b, i, 0)),
        out_shape=jax.ShapeDtypeStruct((bsz, t, BRANCH_W), BF16),
        scratch_shapes=[pltpu.VMEM((tc + 8, cw), F32),
                        pltpu.VMEM((BRANCH_W // HEAD_W, HEAD_W, HEAD_W), F32)],
        compiler_params=_cparams(("arbitrary", "arbitrary")),
        name="delta_net",
    )(pf, pf, small, conv_w, par, gain.reshape(1, HEAD_W).astype(F32))


def _hgrn_kernel(q_ref, f_ref, i_ref, gate_ref, lb_ref, gain_ref, o_ref, st_ref):
    tc = q_ref.shape[1]
    nchunk = tc // CHUNK
    nh = BRANCH_W // HEAD_W
    nsub = CHUNK // SUB

    @pl.when(pl.program_id(1) == 0)
    def _():
        st_ref[...] = jnp.zeros_like(st_ref)

    lb = lb_ref[...]
    df = f_ref[0]
    log_sig = jnp.minimum(df, 0.0) - _log1p_exp_neg_abs(df)
    a = jnp.log(lb)
    b = jnp.log1p(-lb) + log_sig
    logf_all = jnp.maximum(a, b) + _log1p_exp_neg_abs(a - b)
    k_all = (1.0 - lb) * _sigmoid(-df)

    ri = lax.broadcasted_iota(jnp.int32, (CHUNK, 3 * CHUNK), 0)
    ci = lax.broadcasted_iota(jnp.int32, (CHUNK, 3 * CHUNK), 1)
    tri3 = jnp.where(ri >= ci % CHUNK, 1.0, 0.0).astype(BF16)
    sub_row = lax.broadcasted_iota(jnp.int32, (HALF, HEAD_W), 0)
    units = [(ch, h) for ch in range(nchunk) for h in range(nh)]

    bcs = []
    for ch in range(nchunk):
        lf = logf_all[ch * CHUNK:(ch + 1) * CHUNK]
        hi = lf.astype(BF16)
        r1 = lf - hi.astype(F32)
        mid = r1.astype(BF16)
        lo = (r1 - mid.astype(F32)).astype(BF16)
        bcs.append(_nn(tri3, jnp.concatenate([hi, mid, lo], axis=0)))

    qs, ks, vs, bcu, bls, qes, dstate, intra = {}, {}, {}, {}, {}, {}, {}, {}
    for u_ in units:
        ch, h = u_
        r0 = ch * CHUNK
        cs = slice(h * HEAD_W, (h + 1) * HEAD_W)
        qs[u_] = q_ref[0, r0:r0 + CHUNK, cs]
        vs[u_] = i_ref[0, r0:r0 + CHUNK, cs]
        ks[u_] = k_all[r0:r0 + CHUNK, cs]
        bcu[u_] = bcs[ch][:, cs]
        bls[u_] = bcu[u_][CHUNK - 1:CHUNK, :]
        qes[u_] = (qs[u_] * jnp.exp(bcu[u_])).astype(BF16)
        k_end = ks[u_] * jnp.exp(bls[u_] - bcu[u_])
        dstate[u_] = _tn(_split_rows_lhs(vs[u_]), _split_rows(k_end))

    row_c = lax.broadcasted_iota(jnp.int32, (CHUNK, HEAD_W), 0)
    second_half = (row_c // HALF) % 2 == 1
    rr = lax.broadcasted_iota(jnp.int32, (CHUNK, CHUNK), 0)
    cc = lax.broadcasted_iota(jnp.int32, (CHUNK, CHUNK), 1)
    same_block = rr // SUB == cc // SUB
    logits, v_bfs = {}, {}
    for u_ in units:
        q, k, bc = qs[u_], ks[u_], bcu[u_]
        v_bfs[u_] = vs[u_].astype(BF16)
        for si in range(1, nsub):
            rs = slice(si * SUB, (si + 1) * SUB)
            ref_row = bc[si * SUB - 1:si * SUB, :]
            q_dec = (q[rs] * jnp.exp(bc[rs] - ref_row)).astype(BF16)
            k_dec = (k[:si * SUB] * jnp.exp(ref_row - bc[:si * SUB])).astype(BF16)
            logits[u_, si] = _nt(q_dec, k_dec)
        mid_rows = jnp.concatenate(
            [jnp.broadcast_to(bc[si * SUB + HALF - 1:si * SUB + HALF, :], (SUB, HEAD_W))
             for si in range(nsub)], axis=0)
        e_mid = jnp.exp(-jnp.abs(bc - mid_rows))
        q_mid = jnp.where(second_half, q * e_mid, 0.0).astype(BF16)
        k_mid = jnp.where(second_half, 0.0, k * e_mid).astype(BF16)
        logits[u_, "mid"] = jnp.where(same_block, _nt(q_mid, k_mid), 0.0)
    for u_ in units:
        pieces = [jnp.zeros((SUB, HEAD_W), F32)]
        for si in range(1, nsub):
            pieces.append(_nn(logits[u_, si].astype(BF16), v_bfs[u_][:si * SUB]))
        o_b = _nn(logits[u_, "mid"].astype(BF16), v_bfs[u_])
        intra[u_] = jnp.concatenate(pieces, axis=0) + o_b

    diag = {u_: [] for u_ in units}
    for sb in range(CHUNK // HALF):
        rs = slice(sb * HALF, (sb + 1) * HALF)
        o_s = {u_: jnp.zeros((HALF, HEAD_W), F32) for u_ in units}
        for j in range(HALF):
            jj = sb * HALF + j
            for u_ in units:
                q, k, v, bc = qs[u_], ks[u_], vs[u_], bcu[u_]
                e = jnp.exp(jnp.where(sub_row >= j, bc[rs] - bc[jj:jj + 1, :], NEG))
                a_col = jnp.sum(q[rs] * k[jj:jj + 1, :] * e, axis=-1, keepdims=True)
                o_s[u_] = o_s[u_] + a_col * v[jj:jj + 1, :]
        for u_ in units:
            diag[u_].append(o_s[u_])
    for u_ in units:
        intra[u_] = intra[u_] + jnp.concatenate(diag[u_], axis=0)

    states = [st_ref[h] for h in range(nh)]
    for u_ in units:
        ch, h = u_
        r0 = ch * CHUNK
        cs = slice(h * HEAD_W, (h + 1) * HEAD_W)
        o = _nt(qes[u_], states[h].astype(BF16)) + intra[u_]
        states[h] = states[h] * jnp.exp(bls[u_]) + dstate[u_]
        if ch == nchunk - 1:
            st_ref[h] = states[h]
        ms = jnp.mean(o * o, axis=-1, keepdims=True)
        o = o * lax.rsqrt(ms + RMS_EPS) * gain_ref[...]
        o_ref[0, r0:r0 + CHUNK, cs] = (o * _silu(gate_ref[0, r0:r0 + CHUNK, cs])).astype(o_ref.dtype)


def _hgrn(pf, lb, gain, bsz, t, tc=CHUNK):
    def spec(cb):
        return pl.BlockSpec((1, tc, BRANCH_W), lambda b, i: (b, i, cb))

    return pl.pallas_call(
        _hgrn_kernel,
        grid=(bsz, t // tc),
        in_specs=[spec(PF_DQ), spec(PF_DF), spec(PF_DI), spec(PF_DGATE),
                  pl.BlockSpec((1, BRANCH_W), lambda b, i: (0, 0)),
                  pl.BlockSpec((1, HEAD_W), lambda b, i: (0, 0))],
        out_specs=pl.BlockSpec((1, tc, BRANCH_W), lambda b, i: (b, i, 0)),
        out_shape=jax.ShapeDtypeStruct((bsz, t, BRANCH_W), BF16),
        scratch_shapes=[pltpu.VMEM((BRANCH_W // HEAD_W, HEAD_W, HEAD_W), F32)],
        compiler_params=_cparams(("arbitrary", "arbitrary")),
        name="hgrn2",
    )(pf, pf, pf, pf, lb.reshape(1, BRANCH_W), gain.reshape(1, HEAD_W).astype(F32))


def _split_w_in(w):
    o = 0
    parts = {}
    for name, width in (("a_q", 512), ("a_k", 512), ("a_v", 512), ("a_gate", 512),
                        ("b_q", 512), ("b_k", 512), ("b_v", 512), ("b_gate", 512),
                        ("c_qkv", 1536), ("c_z", 512), ("c_beta", 4), ("c_a", 4),
                        ("d_q", 512), ("d_f", 512), ("d_i", 512), ("d_gate", 512),
                        ("merge", 4 * D_MODEL)):
        parts[name] = w[:, o:o + width]
        o += width
    w_f = jnp.concatenate([parts[k] for k in ("c_qkv", "c_z", "a_gate", "b_gate", "d_q", "d_f",
                                              "d_i", "d_gate", "a_q", "a_k", "a_v")], axis=1)
    w_b = jnp.concatenate([parts[k] for k in ("b_q", "b_k", "b_v")], axis=1)
    w_s = jnp.concatenate([parts["c_beta"], parts["c_a"],
                           jnp.zeros((w.shape[0], HEAD_W - 8), w.dtype)], axis=1)
    return w_f.astype(BF16), w_b.astype(BF16), w_s.astype(BF16), parts["merge"].astype(BF16)


def kernel(x, norm_gain, w_in, rel_bias, diff_lambda, diff_subln_gain, dn_conv, dn_a_log, dn_dt_bias,
           dn_norm_gain, hg_lb_logits, hg_norm_gain, w_branch, w_out, final_gain):
    bsz, t, d = x.shape
    n = bsz * t
    depth = w_in.shape[0]
    lb_p = jax.nn.softmax(hg_lb_logits.astype(F32), axis=0)
    hg_lb = jnp.clip(jnp.cumsum(lb_p, axis=0) - lb_p[0], 0.0, 1.0)
    bias_a = _dilated_bias(rel_bias[:, :8].astype(F32))
    bias_b = _diff_bias(rel_bias[:, 8:].astype(F32))
    cfar = rel_bias[NUM_BUCKETS - 1, 8:].astype(F32)

    xf = x.reshape(n, d).astype(F32)
    h = _rmsnorm(xf, norm_gain[0], BF16)
    for layer in range(depth):
        w_f, w_b, w_s, w_g = _split_w_in(w_in[layer])
        pf = _mm(h, w_f, F32, name="proj_f32").reshape(bsz, t, PF_COLS)
        pb = _mm(h, w_b, BF16, name="proj_bf16").reshape(bsz, t, 3 * BRANCH_W)
        ps = _mm(h, w_s, F32, name="proj_small").reshape(bsz, t, HEAD_W)
        gates = _mm(h, w_g, BF16, act="sigmoid", name="proj_gates")

        y_a = _dil_attn(pf, bias_a, bsz, t)

        lam_init = 0.8 - 0.6 * math.exp(-0.3 * layer)
        lq1, lk1, lq2, lk2 = diff_lambda[layer].astype(F32)
        lam = jnp.exp(jnp.sum(lq1 * lk1)) - jnp.exp(jnp.sum(lq2 * lk2)) + lam_init
        scal = jnp.concatenate([jnp.stack([lam, jnp.asarray(1.0 - lam_init, F32)]), cfar,
                                jnp.zeros((2,), F32)])
        y_b = _diff_attn(pb, pf, bias_b, scal, diff_subln_gain[layer], bsz, t)

        conv_w = jnp.concatenate([dn_conv[layer].astype(F32),
                                  jnp.zeros((8 - CONV_K, 3 * BRANCH_W), F32)], axis=0)
        par = jnp.zeros((8, HEAD_W), F32)
        par = par.at[0, 4:8].set(dn_a_log[layer].astype(F32)).at[1, 4:8].set(dn_dt_bias[layer].astype(F32))
        y_c = _delta_net(pf, ps, conv_w, par, dn_norm_gain[layer], bsz, t, tc=DELTA_TILE)

        y_d = _hgrn(pf, hg_lb[layer], hg_norm_gain[layer], bsz, t, tc=HGRN_TILE)

        ys = [y.reshape(n, BRANCH_W) for y in (y_a, y_b, y_c, y_d)]
        last = layer == depth - 1
        res = _merge_out(gates, ys, w_branch[layer].astype(BF16), w_out[layer].astype(BF16), xf,
                         final_gain if last else norm_gain[layer + 1], last)
        if last:
            out = res
        else:
            xf, h = res
    return out.reshape(bsz, t, d).astype(x.dtype)
```

```python
import functools
import math

import jax
import jax.numpy as jnp
import numpy as np
from jax import lax
from jax.experimental import pallas as pl
from jax.experimental.pallas import tpu as pltpu

F32 = jnp.float32
BF16 = jnp.bfloat16
HI = lax.Precision.HIGHEST

D_MODEL = 2048
BRANCH_W = 512
HEAD_W = 128
A_HEAD_DIM = 64
DILATIONS = (1, 4, 16)
BAND = 128
A_TILE = BAND * DILATIONS[-1]
A_GROUP = 4
DIFF_QK = 64
B_BT = 256
B_TQ = 512
B_TK = 512
B_NEAR = 2048 // B_BT + 1
CHUNK = 64
DELTA_TILE = 256
HGRN_TILE = 256
SUB = 16
HALF = SUB // 2
CONV_K = 4
NUM_BUCKETS = 32
MAX_DISTANCE = 2048
RMS_EPS = 1e-6
NEG = -1e30
VMEM_LIMIT = 56 * 1024 * 1024
MM_TM, MM_TN = 2048, 512
MERGE_TM = 256

PF_CQKV, PF_CZ, PF_AGATE, PF_BGATE, PF_DQ, PF_DF, PF_DI, PF_DGATE, PF_AQ, PF_AK, PF_AV = 0, 3, 4, 5, 6, 7, 8, 9, 10, 11, 12
PF_COLS = 13 * 512


def _cparams(sem):
    return pltpu.CompilerParams(dimension_semantics=sem, vmem_limit_bytes=VMEM_LIMIT)


def _nt(a, b, precision=None):
    return lax.dot_general(a, b, (((1,), (1,)), ((), ())), precision=precision,
                           preferred_element_type=F32)


def _tn(a, b, precision=None):
    return lax.dot_general(a, b, (((0,), (0,)), ((), ())), precision=precision,
                           preferred_element_type=F32)


def _nn(a, b, precision=None):
    return jnp.dot(a, b, precision=precision, preferred_element_type=F32)


def _sigmoid(x):
    return 0.5 * jnp.tanh(0.5 * x) + 0.5


def _silu(x):
    return x * _sigmoid(x)


def _log1p_exp_neg_abs(x):
    return jnp.log1p(jnp.exp(-jnp.abs(x)))


def _hi_lo(x):
    bits = lax.bitcast_convert_type(x, jnp.int32)
    hi = lax.bitcast_convert_type(bits & jnp.int32(-65536), F32)
    return hi, x - hi


def _split_cols(a):
    hl = jnp.concatenate(_hi_lo(a), axis=1)
    return jnp.concatenate([hl, hl], axis=1)


def _split_rows(b):
    hi, lo = _hi_lo(b)
    return jnp.concatenate([hi, hi, lo, lo], axis=0)


def _split_rows_lhs(a):
    hi, lo = _hi_lo(a)
    return jnp.concatenate([hi, lo, hi, lo], axis=0)


def _nt_acc(a, b):
    ah, al = _hi_lo(a)
    bh, bl = _hi_lo(b)
    return (_nt(jnp.concatenate([ah, al], axis=1), jnp.concatenate([bh, bh], axis=1))
            + _nt(ah, bl))


def _rmsnorm_kernel(x_ref, g_ref, o_ref):
    x = x_ref[...]
    ms = jnp.mean(x * x, axis=-1, keepdims=True)
    o_ref[...] = (x * lax.rsqrt(ms + RMS_EPS) * g_ref[...]).astype(o_ref.dtype)


def _rmsnorm(x, gain, out_dtype, tm=512):
    n, d = x.shape
    return pl.pallas_call(
        _rmsnorm_kernel,
        grid=(n // tm,),
        in_specs=[pl.BlockSpec((tm, d), lambda i: (i, 0)),
                  pl.BlockSpec((1, d), lambda i: (0, 0))],
        out_specs=pl.BlockSpec((tm, d), lambda i: (i, 0)),
        out_shape=jax.ShapeDtypeStruct((n, d), out_dtype),
        compiler_params=_cparams(("parallel",)),
        name="rmsnorm",
    )(x, gain.reshape(1, d).astype(F32))


def _mm_kernel(a_ref, w_ref, o_ref, *, act):
    acc = _nn(a_ref[...], w_ref[...])
    if act == "sigmoid":
        acc = _sigmoid(acc)
    o_ref[...] = acc.astype(o_ref.dtype)


def _mm(a, w, out_dtype, act=None, tm=MM_TM, tn=MM_TN, name="mm"):
    n, k = a.shape
    c = w.shape[1]
    tn = min(tn, c)
    return pl.pallas_call(
        functools.partial(_mm_kernel, act=act),
        grid=(n // tm, c // tn),
        in_specs=[pl.BlockSpec((tm, k), lambda i, j: (i, 0)),
                  pl.BlockSpec((k, tn), lambda i, j: (0, j))],
        out_specs=pl.BlockSpec((tm, tn), lambda i, j: (i, j)),
        out_shape=jax.ShapeDtypeStruct((n, c), out_dtype),
        compiler_params=_cparams(("parallel", "parallel")),
        name=name,
    )(a, w)


def _merge_out_kernel(g0, g1, g2, g3, y0, y1, y2, y3, wbr_ref, wout_ref, x_ref, gain_ref, *out_refs,
                      last):
    mixed = g0[...].astype(F32) * _nn(y0[...], wbr_ref[0])
    mixed += g1[...].astype(F32) * _nn(y1[...], wbr_ref[1])
    mixed += g2[...].astype(F32) * _nn(y2[...], wbr_ref[2])
    mixed += g3[...].astype(F32) * _nn(y3[...], wbr_ref[3])
    x = x_ref[...] + _nn(mixed.astype(BF16), wout_ref[...])
    ms = jnp.mean(x * x, axis=-1, keepdims=True)
    normed = x * lax.rsqrt(ms + RMS_EPS) * gain_ref[...]
    if last:
        out_refs[0][...] = normed
    else:
        out_refs[0][...] = x
        out_refs[1][...] = normed.astype(BF16)


def _merge_out(gates, ys, w_br, w_out, x, next_gain, last, tm=MERGE_TM):
    n = gates.shape[0]
    gate_specs = [pl.BlockSpec((tm, D_MODEL), functools.partial(lambda i, b: (i, b), b=b))
                  for b in range(4)]
    y_specs = [pl.BlockSpec((tm, BRANCH_W), lambda i: (i, 0)) for _ in range(4)]
    resident = dict(pipeline_mode=pl.Buffered(1))
    row_spec = pl.BlockSpec((tm, D_MODEL), lambda i: (i, 0))
    if last:
        out_specs, out_shape = row_spec, jax.ShapeDtypeStruct((n, D_MODEL), F32)
    else:
        out_specs = (row_spec, row_spec)
        out_shape = (jax.ShapeDtypeStruct((n, D_MODEL), F32), jax.ShapeDtypeStruct((n, D_MODEL), BF16))
    return pl.pallas_call(
        functools.partial(_merge_out_kernel, last=last),
        grid=(n // tm,),
        in_specs=gate_specs + y_specs + [
            pl.BlockSpec((4, BRANCH_W, D_MODEL), lambda i: (0, 0, 0), **resident),
            pl.BlockSpec((D_MODEL, D_MODEL), lambda i: (0, 0), **resident),
            row_spec,
            pl.BlockSpec((1, D_MODEL), lambda i: (0, 0))],
        out_specs=out_specs,
        out_shape=out_shape,
        compiler_params=_cparams(("parallel",)),
        name="merge_out",
    )(gates, gates, gates, gates, *ys, w_br, w_out, x, next_gain.reshape(1, D_MODEL).astype(F32))


def _t5_bucket(dist):
    n = jnp.maximum(dist, 0)
    max_exact = NUM_BUCKETS // 2
    nf = jnp.maximum(n, max_exact).astype(F32)
    large = max_exact + (jnp.log(nf / max_exact) / math.log(MAX_DISTANCE / max_exact)
                         * (NUM_BUCKETS - max_exact)).astype(jnp.int32)
    large = jnp.minimum(large, NUM_BUCKETS - 1)
    return jnp.where(n < max_exact, n, large)


def _bucket_lookup(table, bucket):
    tab = table.T.reshape((table.shape[1],) + (1,) * bucket.ndim + (NUM_BUCKETS,))
    out = jnp.zeros((table.shape[1],) + bucket.shape, F32)
    for b in range(NUM_BUCKETS):
        out = jnp.where(bucket[None] == b, tab[..., b], out)
    return out


def _dilated_bias(bias_a):
    qi = jnp.arange(BAND)[:, None]
    kj = jnp.arange(2 * BAND)[None, :]
    rel = qi + BAND - kj
    valid = (rel >= 0) & (rel <= BAND)
    tabs = []
    for dil in DILATIONS:
        b = _bucket_lookup(bias_a, _t5_bucket(rel * dil))
        tabs.append(jnp.where(valid[None], b, NEG))
    return jnp.stack(tabs)


def _diff_bias(bias_b):
    qi = jnp.arange(B_BT)[:, None]
    kj = jnp.arange(B_BT)[None, :]
    tabs = []
    for d in range(-1, B_NEAR + 1):
        rel = d * B_BT + qi - kj
        b = _bucket_lookup(bias_b, _t5_bucket(rel))
        tabs.append(jnp.where((rel >= 0)[None], b, NEG))
    return jnp.stack(tabs, axis=1)


def _dil_attn_kernel(q_ref, kp_ref, kc_ref, vp_ref, vc_ref, gate_ref, bias_ref, o_ref,
                     kcat, vcat, acc_ref, m_ref, l_ref):
    n = pl.program_id(1)
    kcat[0:A_TILE, :] = kp_ref[0]
    kcat[A_TILE:2 * A_TILE, :] = kc_ref[0]
    vcat[0:A_TILE, :] = vp_ref[0]
    vcat[A_TILE:2 * A_TILE, :] = vc_ref[0]
    lane = lax.broadcasted_iota(jnp.int32, (BAND, HEAD_W), 1)
    lo = lane < A_HEAD_DIM
    col = lax.broadcasted_iota(jnp.int32, (BAND, 2 * BAND), 1)
    nblk_tile = A_TILE // BAND

    def rows(start, r):
        return pl.ds(start, BAND, stride=r) if r > 1 else pl.ds(start, BAND)

    trips = [(p, r, it) for p, r in reversed(list(enumerate(DILATIONS)))
             for it in range(nblk_tile // A_GROUP)]
    first_p = trips[0][0]

    def logits_stage(p, r, it):
        starts, first_blk, vvs, ss = [], [], [], {}
        for g in range(A_GROUP):
            idx = it * A_GROUP + g
            j = idx // r
            start = j * (BAND * r) + idx % r
            starts.append(start)
            first_blk.append(j == 0)
            q = (q_ref[0, rows(start, r), :] * (A_HEAD_DIM ** -0.5)).astype(BF16)
            kk = jnp.concatenate([kcat[rows(A_TILE + start - BAND * r, r), :],
                                  kcat[rows(A_TILE + start, r), :]], axis=0).astype(BF16)
            vvs.append(jnp.concatenate([vcat[rows(A_TILE + start - BAND * r, r), :],
                                        vcat[rows(A_TILE + start, r), :]], axis=0).astype(BF16))
            for hh in range(2):
                qz = jnp.where(lo if hh == 0 else jnp.logical_not(lo), q, jnp.zeros_like(q))
                ss[g, hh] = _nt(qz, kk)
        return starts, first_blk, vvs, ss

    def finish_stage(p, r, staged):
        starts, first_blk, vvs, ss = staged
        es, ms, ls, outs = {}, {}, {}, {}
        for g in range(A_GROUP):
            for hh in range(2):
                s = ss[g, hh] + bias_ref[p, hh]
                if first_blk[g]:
                    s = jnp.where(jnp.logical_and(n == 0, col < BAND), NEG, s)
                m = jnp.max(s, axis=-1, keepdims=True)
                e = jnp.exp(s - m)
                ms[g, hh] = jnp.broadcast_to(m, (BAND, HEAD_W))
                ls[g, hh] = jnp.broadcast_to(jnp.sum(e, axis=-1, keepdims=True), (BAND, HEAD_W))
                es[g, hh] = e.astype(BF16)
        for g in range(A_GROUP):
            for hh in range(2):
                outs[g, hh] = _nn(es[g, hh], vvs[g])
        merged = []
        for g in range(A_GROUP):
            o_new = jnp.where(lo, outs[g, 0], outs[g, 1])
            m_new = jnp.where(lo, ms[g, 0], ms[g, 1])
            l_new = jnp.where(lo, ls[g, 0], ls[g, 1])
            if p != first_p:
                rws = rows(starts[g], r)
                m_old = m_ref[rws, :]
                m_tot = jnp.maximum(m_old, m_new)
                a = jnp.exp(m_old - m_tot)
                b = jnp.exp(m_new - m_tot)
                o_new = a * acc_ref[rws, :] + b * o_new
                l_new = a * l_ref[rws, :] + b * l_new
                m_new = m_tot
            merged.append((o_new, m_new, l_new))
        for g in range(A_GROUP):
            rws = rows(starts[g], r)
            acc_ref[rws, :], m_ref[rws, :], l_ref[rws, :] = merged[g]

    staged = logits_stage(*trips[0])
    for i, (p, r, _) in enumerate(trips):
        nxt = logits_stage(*trips[i + 1]) if i + 1 < len(trips) else None
        finish_stage(p, r, staged)
        staged = nxt

    o_ref[0] = (acc_ref[...] / l_ref[...] * _silu(gate_ref[0])).astype(o_ref.dtype)


def _dil_attn(pf, bias_tab, bsz, t):
    nt = t // A_TILE
    hp = BRANCH_W // HEAD_W
    blk = (1, A_TILE, HEAD_W)

    def cur(cb):
        return pl.BlockSpec(blk, lambda b, n, h: (b, n, cb * hp + h))

    def prev(cb):
        return pl.BlockSpec(blk, lambda b, n, h: (b, jnp.maximum(n - 1, 0), cb * hp + h))

    return pl.pallas_call(
        _dil_attn_kernel,
        grid=(bsz, nt, hp),
        in_specs=[cur(PF_AQ), prev(PF_AK), cur(PF_AK), prev(PF_AV), cur(PF_AV), cur(PF_AGATE),
                  pl.BlockSpec((3, 2, BAND, 2 * BAND), lambda b, n, h: (0, h, 0, 0))],
        out_specs=pl.BlockSpec(blk, lambda b, n, h: (b, n, h)),
        out_shape=jax.ShapeDtypeStruct((bsz, t, BRANCH_W), BF16),
        scratch_shapes=[pltpu.VMEM((2 * A_TILE, HEAD_W), F32), pltpu.VMEM((2 * A_TILE, HEAD_W), F32),
                        pltpu.VMEM((A_TILE, HEAD_W), F32), pltpu.VMEM((A_TILE, HEAD_W), F32),
                        pltpu.VMEM((A_TILE, HEAD_W), F32)],
        compiler_params=_cparams(("parallel", "parallel", "parallel")),
        name="dilated_attn",
    )(pf, pf, pf, pf, pf, pf, bias_tab)


def _diff_attn_kernel(scal_ref, q_ref, k_ref, v_ref, gate_ref, bias_ref, gain_ref, o_ref,
                      kt_ref, vx_ref, s0_ref, s1_ref, p0_ref, p1_ref, al0_ref, al1_ref,
                      acc_ref, m_ref):
    h = pl.program_id(1)
    qi = pl.program_id(2)
    nkb = kt_ref.shape[0]
    nrh = B_TQ // B_BT
    nsub = B_TK // B_BT
    ncol = B_TK // HEAD_W

    @pl.when(qi == 0)
    def _():
        def tr(kb, c):
            rows = pl.ds(pl.multiple_of(kb * B_TK, B_TK), B_TK)
            kt_ref[kb] = k_ref[0, rows, :].astype(F32).T.astype(BF16)
            vx_ref[rows, 0:HEAD_W] = v_ref[0, rows, :]
            vx_ref[rows, HEAD_W:2 * HEAD_W] = jnp.ones((B_TK, HEAD_W), BF16)
            return c
        lax.fori_loop(0, nkb, tr, 0)

    lane = lax.broadcasted_iota(jnp.int32, (B_TQ, HEAD_W), 1)
    lo = lane < DIFF_QK
    q = q_ref[0] * (DIFF_QK ** -0.5)
    zero = jnp.zeros_like(q)
    qz = (jnp.where(lo, q, zero), jnp.where(lo, zero, q))
    cfar = scal_ref[2 + h]
    nk = ((qi + 1) * nrh - 1) // nsub + 1
    n_far = jnp.maximum((nrh * qi - B_NEAR + 1) // nsub, 0)
    per_sub = B_BT // HEAD_W

    s_bufs, p_bufs, al_bufs = (s0_ref, s1_ref), (p0_ref, p1_ref), (al0_ref, al1_ref)
    acc_ref[...] = jnp.zeros_like(acc_ref)
    m_ref[...] = jnp.full_like(m_ref, NEG)
    al1_ref[...] = jnp.ones_like(al1_ref)
    p1_ref[...] = jnp.zeros_like(p1_ref)

    row_halves = [slice(rh * B_BT, (rh + 1) * B_BT) for rh in range(nrh)]

    def qk(kb, par):
        kt = kt_ref[jnp.minimum(kb, nk - 1)]
        for mi in range(2):
            for rows in row_halves:
                s_bufs[par][mi, rows, :] = _nn(qz[mi][rows], kt)

    def pv(kb, par):
        kb = jnp.clip(kb, 0, nk - 1)
        vx = vx_ref[pl.ds(pl.multiple_of(kb * B_TK, B_TK), B_TK), :]
        for mi in range(2):
            for rows in row_halves:
                a = al_bufs[par][mi, rows]
                upd = _nn(p_bufs[par][mi, rows, :], vx)
                acc_ref[mi, rows] = jnp.concatenate([a, a], axis=1) * acc_ref[mi, rows] + upd

    def col(par, mi, c, kb, near):
        x = s_bufs[par][mi, :, c * HEAD_W:(c + 1) * HEAD_W]
        if near:
            cc = (c % per_sub) * HEAD_W
            tiles = [jnp.clip(nrh * qi + rh - nsub * kb - c // per_sub, -1, B_NEAR) + 1
                     for rh in range(nrh)]
            x = x + jnp.concatenate([bias_ref[0, tl, :, cc:cc + HEAD_W] for tl in tiles], axis=0)
        return x

    def softmax(kb, par, near):
        shift = 0.0 if near else cfar
        m_sub = []
        for mi in range(2):
            mx = col(par, mi, 0, kb, near)
            for c in range(1, ncol):
                mx = jnp.maximum(mx, col(par, mi, c, kb, near))
            m_prev = m_ref[mi]
            m_next = jnp.maximum(m_prev, jnp.max(mx, axis=-1, keepdims=True) + shift)
            al_bufs[par][mi] = jnp.exp(m_prev - m_next)
            m_ref[mi] = m_next
            m_sub.append(m_next - shift)
        for mi in range(2):
            for c in range(ncol):
                e = jnp.exp(col(par, mi, c, kb, near) - m_sub[mi])
                p_bufs[par][mi, :, c * HEAD_W:(c + 1) * HEAD_W] = e.astype(BF16)

    def pair(u, near):
        for par in range(2):
            t = 2 * u + par
            qk(t + 1, 1 - par)
            pv(t - 1, 1 - par)
            softmax(t, par, near)

    def far_body(u, c):
        pair(u, False)
        return c

    def near_body(u, c):
        pair(u, True)
        return c

    far_pairs = n_far // 2
    all_pairs = (nk + 1) // 2
    qk(0, 0)
    lax.fori_loop(0, far_pairs, far_body, 0)
    lax.fori_loop(far_pairs, all_pairs, near_body, 0)
    pv(2 * all_pairs - 1, 1)

    lam = scal_ref[0]
    a1, a2 = acc_ref[0], acc_ref[1]
    o = a1[:, :HEAD_W] / a1[:, HEAD_W:] - lam * (a2[:, :HEAD_W] / a2[:, HEAD_W:])
    ms = jnp.mean(o * o, axis=-1, keepdims=True)
    o = o * lax.rsqrt(ms + RMS_EPS) * gain_ref[...] * scal_ref[1]
    o_ref[0] = (o * _silu(gate_ref[0])).astype(o_ref.dtype)


def _diff_attn(pb, pf, bias_tab, scal, gain, bsz, t):
    nh = BRANCH_W // HEAD_W
    nq = t // B_TQ
    return pl.pallas_call(
        _diff_attn_kernel,
        grid=(bsz, nh, nq),
        in_specs=[pl.BlockSpec(memory_space=pltpu.SMEM),
                  pl.BlockSpec((1, B_TQ, HEAD_W), lambda b, h, i: (b, i, h)),
                  pl.BlockSpec((1, t, HEAD_W), lambda b, h, i: (b, 0, nh + h)),
                  pl.BlockSpec((1, t, HEAD_W), lambda b, h, i: (b, 0, 2 * nh + h)),
                  pl.BlockSpec((1, B_TQ, HEAD_W), lambda b, h, i: (b, i, PF_BGATE * nh + h)),
                  pl.BlockSpec((1, B_NEAR + 2, B_BT, B_BT), lambda b, h, i: (h, 0, 0, 0)),
                  pl.BlockSpec((1, HEAD_W), lambda b, h, i: (0, 0))],
        out_specs=pl.BlockSpec((1, B_TQ, HEAD_W), lambda b, h, i: (b, i, h)),
        out_shape=jax.ShapeDtypeStruct((bsz, t, BRANCH_W), BF16),
        scratch_shapes=[pltpu.VMEM((t // B_TK, HEAD_W, B_TK), BF16),
                        pltpu.VMEM((t, 2 * HEAD_W), BF16),
                        pltpu.VMEM((2, B_TQ, B_TK), F32),
                        pltpu.VMEM((2, B_TQ, B_TK), F32),
                        pltpu.VMEM((2, B_TQ, B_TK), BF16),
                        pltpu.VMEM((2, B_TQ, B_TK), BF16),
                        pltpu.VMEM((2, B_TQ, HEAD_W), F32),
                        pltpu.VMEM((2, B_TQ, HEAD_W), F32),
                        pltpu.VMEM((2, B_TQ, 2 * HEAD_W), F32),
                        pltpu.VMEM((2, B_TQ, HEAD_W), F32)],
        compiler_params=_cparams(("arbitrary", "arbitrary", "arbitrary")),
        name="diff_attn",
    )(scal, pb, pb, pb, pf, bias_tab, gain.reshape(1, HEAD_W).astype(F32))


def _delta_kernel(qkv_ref, z_ref, small_ref, conv_ref, par_ref, gain_ref, o_ref,
                  xe_ref, s_ref):
    tc = qkv_ref.shape[1]
    nchunk = tc // CHUNK

    @pl.when(pl.program_id(1) == 0)
    def _():
        xe_ref[0:8, :] = jnp.zeros((8, xe_ref.shape[1]), F32)
        s_ref[...] = jnp.zeros_like(s_ref)

    xe_ref[8:8 + tc, :] = qkv_ref[0]
    conv = conv_ref[0:1, :] * xe_ref[pl.ds(8 - 3, tc), :]
    for kk in range(1, CONV_K):
        conv = conv + conv_ref[kk:kk + 1, :] * xe_ref[pl.ds(8 - 3 + kk, tc), :]
    xe_ref[0:8, :] = xe_ref[tc:tc + 8, :]
    c = _silu(conv)

    small = small_ref[0]
    beta_all = _sigmoid(small)
    xa = small + par_ref[1:2, :]
    softplus = jnp.maximum(xa, 0.0) + _log1p_exp_neg_abs(xa)
    g_all = -jnp.exp(par_ref[0:1, :]) * softplus

    ri = lax.broadcasted_iota(jnp.int32, (CHUNK, CHUNK), 0)
    ci = lax.broadcasted_iota(jnp.int32, (CHUNK, CHUNK), 1)
    tri = ri >= ci
    strict = ri > ci
    eye = (ri == ci).astype(F32)
    blocks = [ri // size == ci // size for size in (8, 16, 32, CHUNK)]
    rt = lax.broadcasted_iota(jnp.int32, (tc, tc), 0)
    ct = lax.broadcasted_iota(jnp.int32, (tc, tc), 1)
    tri_blocks = jnp.logical_and(rt >= ct, rt // CHUNK == ct // CHUNK).astype(F32)
    sel = (lax.broadcasted_iota(jnp.int32, (8, HEAD_W), 1)
           == lax.broadcasted_iota(jnp.int32, (8, HEAD_W), 0) + 4).astype(F32)
    nh = BRANCH_W // HEAD_W
    units = [(ch, h) for ch in range(nchunk) for h in range(nh)]

    gcum_all = _nn(tri_blocks, g_all, HI)
    grow_all = _nt(sel, gcum_all, HI)

    qs, ks, vs, bcols, gcols, glasts, gammas, ms_ = {}, {}, {}, {}, {}, {}, {}, {}
    for h in range(nh):
        qf = c[:, h * HEAD_W:(h + 1) * HEAD_W]
        kf = c[:, BRANCH_W + h * HEAD_W:BRANCH_W + (h + 1) * HEAD_W]
        qf = qf * lax.rsqrt(jnp.sum(qf * qf, axis=-1, keepdims=True) + 1e-6) * (HEAD_W ** -0.5)
        kf = kf * lax.rsqrt(jnp.sum(kf * kf, axis=-1, keepdims=True) + 1e-6)
        for ch in range(nchunk):
            r0 = ch * CHUNK
            u_ = (ch, h)
            qs[u_] = qf[r0:r0 + CHUNK]
            ks[u_] = kf[r0:r0 + CHUNK]
            vs[u_] = c[r0:r0 + CHUNK, 2 * BRANCH_W + h * HEAD_W:2 * BRANCH_W + (h + 1) * HEAD_W]
            bcols[u_] = beta_all[r0:r0 + CHUNK, h:h + 1]
            gcols[u_] = gcum_all[r0:r0 + CHUNK, 4 + h:5 + h]
            glasts[u_] = gcum_all[r0 + CHUNK - 1:r0 + CHUNK, 4 + h:5 + h]
            grow = grow_all[h:h + 1, r0:r0 + CHUNK]
            gammas[u_] = jnp.exp(jnp.where(tri, gcols[u_] - grow, NEG))
    for u_ in units:
        kkt = _nt_acc(ks[u_], ks[u_])
        ms_[u_] = jnp.where(strict, bcols[u_] * kkt * gammas[u_], 0.0)

    def prod(a, b):
        return {u_: _nn(_split_cols(a[u_]), _split_rows(b[u_])) for u_ in units}

    pw = {u_: jnp.where(blocks[0], -ms_[u_], 0.0) for u_ in units}
    tinv = {u_: eye + pw[u_] for u_ in units}
    for _ in range(2):
        pw = prod(pw, pw)
        step = prod(tinv, pw)
        tinv = {u_: tinv[u_] + step[u_] for u_ in units}
    for inner, outer in zip(blocks[:-1], blocks[1:]):
        coupling = {u_: jnp.where(jnp.logical_and(outer, jnp.logical_not(inner)), ms_[u_], 0.0)
                    for u_ in units}
        step = prod(tinv, prod(coupling, tinv))
        tinv = {u_: tinv[u_] - step[u_] for u_ in units}

    us, ws, aqks, qds, kds = {}, {}, {}, {}, {}
    for u_ in units:
        egc = jnp.exp(gcols[u_])
        rhs = jnp.concatenate([vs[u_] * bcols[u_], ks[u_] * (bcols[u_] * egc)], axis=1)
        uw = _nn(_split_cols(tinv[u_]), _split_rows(rhs))
        us[u_], ws[u_] = uw[:, :HEAD_W], uw[:, HEAD_W:]
        aqks[u_] = _nt(qs[u_], ks[u_]) * gammas[u_]
        qds[u_] = qs[u_] * egc
        kds[u_] = ks[u_] * jnp.exp(glasts[u_] - gcols[u_])

    states = [s_ref[h] for h in range(nh)]
    for ch in range(nchunk):
        r0 = ch * CHUNK
        v_news, o_inters = {}, {}
        for h in range(nh):
            u_ = (ch, h)
            s = states[h]
            s_hi, s_lo = _hi_lo(s)
            w_hi, w_lo = _hi_lo(ws[u_])
            ws_prod = (_nn(jnp.concatenate([w_hi, w_lo], axis=1), jnp.concatenate([s_hi, s_hi], axis=0))
                       + _nn(w_hi, s_lo))
            v_news[h] = us[u_] - ws_prod
            o_inters[h] = _nn(qds[u_], s)
        for h in range(nh):
            u_ = (ch, h)
            v_new = v_news[h]
            o = o_inters[h] + _nn(aqks[u_], v_new)
            states[h] = (jnp.exp(glasts[u_]) * states[h]
                         + _tn(_split_rows_lhs(kds[u_]), _split_rows(v_new)))
            if ch == nchunk - 1:
                s_ref[h] = states[h]
            ms = jnp.mean(o * o, axis=-1, keepdims=True)
            o = o * lax.rsqrt(ms + RMS_EPS) * gain_ref[...]
            zg = z_ref[0, r0:r0 + CHUNK, h * HEAD_W:(h + 1) * HEAD_W]
            o_ref[0, r0:r0 + CHUNK, h * HEAD_W:(h + 1) * HEAD_W] = (o * _silu(zg)).astype(o_ref.dtype)


def _delta_net(pf, small, conv_w, par, gain, bsz, t, tc=CHUNK):
    cw = 3 * BRANCH_W
    return pl.pallas_call(
        _delta_kernel,
        grid=(bsz, t // tc),
        in_specs=[pl.BlockSpec((1, tc, cw), lambda b, i: (b, i, 0)),
                  pl.BlockSpec((1, tc, BRANCH_W), lambda b, i: (b, i, PF_CZ)),
                  pl.BlockSpec((1, tc, HEAD_W), lambda b, i: (b, i, 0)),
                  pl.BlockSpec((8, cw), lambda b, i: (0, 0)),
                  pl.BlockSpec((8, HEAD_W), lambda b, i: (0, 0)),
                  pl.BlockSpec((1, HEAD_W), lambda b, i: (0, 0))],
        out_specs=pl.BlockSpec((1, tc, BRANCH_W), lambda b, i: (b, i, 0)),
        out_shape=jax.ShapeDtypeStruct((bsz, t, BRANCH_W), BF16),
        scratch_shapes=[pltpu.VMEM((tc + 8, cw), F32),
                        pltpu.VMEM((BRANCH_W // HEAD_W, HEAD_W, HEAD_W), F32)],
        compiler_params=_cparams(("arbitrary", "arbitrary")),
        name="delta_net",
    )(pf, pf, small, conv_w, par, gain.reshape(1, HEAD_W).astype(F32))


def _hgrn_kernel(q_ref, f_ref, i_ref, gate_ref, lb_ref, gain_ref, o_ref, st_ref):
    tc = q_ref.shape[1]
    nchunk = tc // CHUNK
    nh = BRANCH_W // HEAD_W
    nsub = CHUNK // SUB

    @pl.when(pl.program_id(1) == 0)
    def _():
        st_ref[...] = jnp.zeros_like(st_ref)

    lb = lb_ref[...]
    df = f_ref[0]
    log_sig = jnp.minimum(df, 0.0) - _log1p_exp_neg_abs(df)
    a = jnp.log(lb)
    b = jnp.log1p(-lb) + log_sig
    logf_all = jnp.maximum(a, b) + _log1p_exp_neg_abs(a - b)
    k_all = (1.0 - lb) * _sigmoid(-df)

    ri = lax.broadcasted_iota(jnp.int32, (CHUNK, 3 * CHUNK), 0)
    ci = lax.broadcasted_iota(jnp.int32, (CHUNK, 3 * CHUNK), 1)
    tri3 = jnp.where(ri >= ci % CHUNK, 1.0, 0.0).astype(BF16)
    sub_row = lax.broadcasted_iota(jnp.int32, (HALF, HEAD_W), 0)
    units = [(ch, h) for ch in range(nchunk) for h in range(nh)]

    bcs = []
    for ch in range(nchunk):
        lf = logf_all[ch * CHUNK:(ch + 1) * CHUNK]
        hi = lf.astype(BF16)
        r1 = lf - hi.astype(F32)
        mid = r1.astype(BF16)
        lo = (r1 - mid.astype(F32)).astype(BF16)
        bcs.append(_nn(tri3, jnp.concatenate([hi, mid, lo], axis=0)))

    qs, ks, vs, bcu, bls, qes, dstate, intra = {}, {}, {}, {}, {}, {}, {}, {}
    for u_ in units:
        ch, h = u_
        r0 = ch * CHUNK
        cs = slice(h * HEAD_W, (h + 1) * HEAD_W)
        qs[u_] = q_ref[0, r0:r0 + CHUNK, cs]
        vs[u_] = i_ref[0, r0:r0 + CHUNK, cs]
        ks[u_] = k_all[r0:r0 + CHUNK, cs]
        bcu[u_] = bcs[ch][:, cs]
        bls[u_] = bcu[u_][CHUNK - 1:CHUNK, :]
        qes[u_] = (qs[u_] * jnp.exp(bcu[u_])).astype(BF16)
        k_end = ks[u_] * jnp.exp(bls[u_] - bcu[u_])
        dstate[u_] = _tn(_split_rows_lhs(vs[u_]), _split_rows(k_end))

    row_c = lax.broadcasted_iota(jnp.int32, (CHUNK, HEAD_W), 0)
    second_half = (row_c // HALF) % 2 == 1
    rr = lax.broadcasted_iota(jnp.int32, (CHUNK, CHUNK), 0)
    cc = lax.broadcasted_iota(jnp.int32, (CHUNK, CHUNK), 1)
    same_block = rr // SUB == cc // SUB
    logits, v_bfs = {}, {}
    for u_ in units:
        q, k, bc = qs[u_], ks[u_], bcu[u_]
        v_bfs[u_] = vs[u_].astype(BF16)
        for si in range(1, nsub):
            rs = slice(si * SUB, (si + 1) * SUB)
            ref_row = bc[si * SUB - 1:si * SUB, :]
            q_dec = (q[rs] * jnp.exp(bc[rs] - ref_row)).astype(BF16)
            k_dec = (k[:si * SUB] * jnp.exp(ref_row - bc[:si * SUB])).astype(BF16)
            logits[u_, si] = _nt(q_dec, k_dec)
        mid_rows = jnp.concatenate(
            [jnp.broadcast_to(bc[si * SUB + HALF - 1:si * SUB + HALF, :], (SUB, HEAD_W))
             for si in range(nsub)], axis=0)
        e_mid = jnp.exp(-jnp.abs(bc - mid_rows))
        q_mid = jnp.where(second_half, q * e_mid, 0.0).astype(BF16)
        k_mid = jnp.where(second_half, 0.0, k * e_mid).astype(BF16)
        logits[u_, "mid"] = jnp.where(same_block, _nt(q_mid, k_mid), 0.0)
    for u_ in units:
        pieces = [jnp.zeros((SUB, HEAD_W), F32)]
        for si in range(1, nsub):
            pieces.append(_nn(logits[u_, si].astype(BF16), v_bfs[u_][:si * SUB]))
        o_b = _nn(logits[u_, "mid"].astype(BF16), v_bfs[u_])
        intra[u_] = jnp.concatenate(pieces, axis=0) + o_b

    diag = {u_: [] for u_ in units}
    for sb in range(CHUNK // HALF):
        rs = slice(sb * HALF, (sb + 1) * HALF)
        o_s = {u_: jnp.zeros((HALF, HEAD_W), F32) for u_ in units}
        for j in range(HALF):
            jj = sb * HALF + j
            for u_ in units:
                q, k, v, bc = qs[u_], ks[u_], vs[u_], bcu[u_]
                e = jnp.exp(jnp.where(sub_row >= j, bc[rs] - bc[jj:jj + 1, :], NEG))
                a_col = jnp.sum(q[rs] * k[jj:jj + 1, :] * e, axis=-1, keepdims=True)
                o_s[u_] = o_s[u_] + a_col * v[jj:jj + 1, :]
        for u_ in units:
            diag[u_].append(o_s[u_])
    for u_ in units:
        intra[u_] = intra[u_] + jnp.concatenate(diag[u_], axis=0)

    states = [st_ref[h] for h in range(nh)]
    for u_ in units:
        ch, h = u_
        r0 = ch * CHUNK
        cs = slice(h * HEAD_W, (h + 1) * HEAD_W)
        o = _nt(qes[u_], states[h].astype(BF16)) + intra[u_]
        states[h] = states[h] * jnp.exp(bls[u_]) + dstate[u_]
        if ch == nchunk - 1:
            st_ref[h] = states[h]
        ms = jnp.mean(o * o, axis=-1, keepdims=True)
        o = o * lax.rsqrt(ms + RMS_EPS) * gain_ref[...]
        o_ref[0, r0:r0 + CHUNK, cs] = (o * _silu(gate_ref[0, r0:r0 + CHUNK, cs])).astype(o_ref.dtype)


def _hgrn(pf, lb, gain, bsz, t, tc=CHUNK):
    def spec(cb):
        return pl.BlockSpec((1, tc, BRANCH_W), lambda b, i: (b, i, cb))

    return pl.pallas_call(
        _hgrn_kernel,
        grid=(bsz, t // tc),
        in_specs=[spec(PF_DQ), spec(PF_DF), spec(PF_DI), spec(PF_DGATE),
                  pl.BlockSpec((1, BRANCH_W), lambda b, i: (0, 0)),
                  pl.BlockSpec((1, HEAD_W), lambda b, i: (0, 0))],
        out_specs=pl.BlockSpec((1, tc, BRANCH_W), lambda b, i: (b, i, 0)),
        out_shape=jax.ShapeDtypeStruct((bsz, t, BRANCH_W), BF16),
        scratch_shapes=[pltpu.VMEM((BRANCH_W // HEAD_W, HEAD_W, HEAD_W), F32)],
        compiler_params=_cparams(("arbitrary", "arbitrary")),
        name="hgrn2",
    )(pf, pf, pf, pf, lb.reshape(1, BRANCH_W), gain.reshape(1, HEAD_W).astype(F32))


def _split_w_in(w):
    o = 0
    parts = {}
    for name, width in (("a_q", 512), ("a_k", 512), ("a_v", 512), ("a_gate", 512),
                        ("b_q", 512), ("b_k", 512), ("b_v", 512), ("b_gate", 512),
                        ("c_qkv", 1536), ("c_z", 512), ("c_beta", 4), ("c_a", 4),
                        ("d_q", 512), ("d_f", 512), ("d_i", 512), ("d_gate", 512),
                        ("merge", 4 * D_MODEL)):
        parts[name] = w[:, o:o + width]
        o += width
    w_f = jnp.concatenate([parts[k] for k in ("c_qkv", "c_z", "a_gate", "b_gate", "d_q", "d_f",
                                              "d_i", "d_gate", "a_q", "a_k", "a_v")], axis=1)
    w_b = jnp.concatenate([parts[k] for k in ("b_q", "b_k", "b_v")], axis=1)
    w_s = jnp.concatenate([parts["c_beta"], parts["c_a"],
                           jnp.zeros((w.shape[0], HEAD_W - 8), w.dtype)], axis=1)
    return w_f.astype(BF16), w_b.astype(BF16), w_s.astype(BF16), parts["merge"].astype(BF16)


def kernel(x, norm_gain, w_in, rel_bias, diff_lambda, diff_subln_gain, dn_conv, dn_a_log, dn_dt_bias,
           dn_norm_gain, hg_lb_logits, hg_norm_gain, w_branch, w_out, final_gain):
    bsz, t, d = x.shape
    n = bsz * t
    depth = w_in.shape[0]
    lb_p = jax.nn.softmax(hg_lb_logits.astype(F32), axis=0)
    hg_lb = jnp.clip(jnp.cumsum(lb_p, axis=0) - lb_p[0], 0.0, 1.0)
    bias_a = _dilated_bias(rel_bias[:, :8].astype(F32))
    bias_b = _diff_bias(rel_bias[:, 8:].astype(F32))
    cfar = rel_bias[NUM_BUCKETS - 1, 8:].astype(F32)

    xf = x.reshape(n, d).astype(F32)
    h = _rmsnorm(xf, norm_gain[0], BF16)
    for layer in range(depth):
        w_f, w_b, w_s, w_g = _split_w_in(w_in[layer])
        pf = _mm(h, w_f, F32, name="proj_f32").reshape(bsz, t, PF_COLS)
        pb = _mm(h, w_b, BF16, name="proj_bf16").reshape(bsz, t, 3 * BRANCH_W)
        ps = _mm(h, w_s, F32, name="proj_small").reshape(bsz, t, HEAD_W)
        gates = _mm(h, w_g, BF16, act="sigmoid", name="proj_gates")

        y_a = _dil_attn(pf, bias_a, bsz, t)

        lam_init = 0.8 - 0.6 * math.exp(-0.3 * layer)
        lq1, lk1, lq2, lk2 = diff_lambda[layer].astype(F32)
        lam = jnp.exp(jnp.sum(lq1 * lk1)) - jnp.exp(jnp.sum(lq2 * lk2)) + lam_init
        scal = jnp.concatenate([jnp.stack([lam, jnp.asarray(1.0 - lam_init, F32)]), cfar,
                                jnp.zeros((2,), F32)])
        y_b = _diff_attn(pb, pf, bias_b, scal, diff_subln_gain[layer], bsz, t)

        conv_w = jnp.concatenate([dn_conv[layer].astype(F32),
                                  jnp.zeros((8 - CONV_K, 3 * BRANCH_W), F32)], axis=0)
        par = jnp.zeros((8, HEAD_W), F32)
        par = par.at[0, 4:8].set(dn_a_log[layer].astype(F32)).at[1, 4:8].set(dn_dt_bias[layer].astype(F32))
        y_c = _delta_net(pf, ps, conv_w, par, dn_norm_gain[layer], bsz, t, tc=DELTA_TILE)

        y_d = _hgrn(pf, hg_lb[layer], hg_norm_gain[layer], bsz, t, tc=HGRN_TILE)

        ys = [y.reshape(n, BRANCH_W) for y in (y_a, y_b, y_c, y_d)]
        last = layer == depth - 1
        res = _merge_out(gates, ys, w_branch[layer].astype(BF16), w_out[layer].astype(BF16), xf,
                         final_gain if last else norm_gain[layer + 1], last)
        if last:
            out = res
        else:
            xf, h = res
    return out.reshape(bsz, t, d).astype(x.dtype)
```

```python
import functools
import math

import jax
import jax.numpy as jnp
import numpy as np
from jax import lax
from jax.experimental import pallas as pl
from jax.experimental.pallas import tpu as pltpu

F32 = jnp.float32
BF16 = jnp.bfloat16
HI = lax.Precision.HIGHEST

D_MODEL = 2048
BRANCH_W = 512
HEAD_W = 128
A_HEAD_DIM = 64
DILATIONS = (1, 4, 16)
BAND = 128
A_TILE = BAND * DILATIONS[-1]
A_GROUP = 4
DIFF_QK = 64
B_BT = 256
B_TQ = 512
B_TK = 512
B_NEAR = 2048 // B_BT + 1
CHUNK = 64
DELTA_TILE = 256
HGRN_TILE = 256
SUB = 16
HALF = SUB // 2
CONV_K = 4
NUM_BUCKETS = 32
MAX_DISTANCE = 2048
RMS_EPS = 1e-6
NEG = -1e30
VMEM_LIMIT = 56 * 1024 * 1024
MM_TM, MM_TN = 2048, 512
MERGE_TM = 256

PF_CQKV, PF_CZ, PF_AGATE, PF_BGATE, PF_DQ, PF_DF, PF_DI, PF_DGATE, PF_AQ, PF_AK, PF_AV = 0, 3, 4, 5, 6, 7, 8, 9, 10, 11, 12
PF_COLS = 13 * 512


def _cparams(sem):
    return pltpu.CompilerParams(dimension_semantics=sem, vmem_limit_bytes=VMEM_LIMIT)


def _nt(a, b, precision=None):
    return lax.dot_general(a, b, (((1,), (1,)), ((), ())), precision=precision,
                           preferred_element_type=F32)


def _tn(a, b, precision=None):
    return lax.dot_general(a, b, (((0,), (0,)), ((), ())), precision=precision,
                           preferred_element_type=F32)


def _nn(a, b, precision=None):
    return jnp.dot(a, b, precision=precision, preferred_element_type=F32)


def _sigmoid(x):
    return 0.5 * jnp.tanh(0.5 * x) + 0.5


def _silu(x):
    return x * _sigmoid(x)


def _log1p_exp_neg_abs(x):
    return jnp.log1p(jnp.exp(-jnp.abs(x)))


def _hi_lo(x):
    bits = lax.bitcast_convert_type(x, jnp.int32)
    hi = lax.bitcast_convert_type(bits & jnp.int32(-65536), F32)
    return hi, x - hi


def _split_cols(a):
    hl = jnp.concatenate(_hi_lo(a), axis=1)
    return jnp.concatenate([hl, hl], axis=1)


def _split_rows(b):
    hi, lo = _hi_lo(b)
    return jnp.concatenate([hi, hi, lo, lo], axis=0)


def _split_rows_lhs(a):
    hi, lo = _hi_lo(a)
    return jnp.concatenate([hi, lo, hi, lo], axis=0)


def _nt_acc(a, b):
    ah, al = _hi_lo(a)
    bh, bl = _hi_lo(b)
    return (_nt(jnp.concatenate([ah, al], axis=1), jnp.concatenate([bh, bh], axis=1))
            + _nt(ah, bl))


def _rmsnorm_kernel(x_ref, g_ref, o_ref):
    x = x_ref[...]
    ms = jnp.mean(x * x, axis=-1, keepdims=True)
    o_ref[...] = (x * lax.rsqrt(ms + RMS_EPS) * g_ref[...]).astype(o_ref.dtype)


def _rmsnorm(x, gain, out_dtype, tm=512):
    n, d = x.shape
    return pl.pallas_call(
        _rmsnorm_kernel,
        grid=(n // tm,),
        in_specs=[pl.BlockSpec((tm, d), lambda i: (i, 0)),
                  pl.BlockSpec((1, d), lambda i: (0, 0))],
        out_specs=pl.BlockSpec((tm, d), lambda i: (i, 0)),
        out_shape=jax.ShapeDtypeStruct((n, d), out_dtype),
        compiler_params=_cparams(("parallel",)),
        name="rmsnorm",
    )(x, gain.reshape(1, d).astype(F32))


def _mm_kernel(a_ref, w_ref, o_ref, *, act):
    acc = _nn(a_ref[...], w_ref[...])
    if act == "sigmoid":
        acc = _sigmoid(acc)
    o_ref[...] = acc.astype(o_ref.dtype)


def _mm(a, w, out_dtype, act=None, tm=MM_TM, tn=MM_TN, name="mm"):
    n, k = a.shape
    c = w.shape[1]
    tn = min(tn, c)
    return pl.pallas_call(
        functools.partial(_mm_kernel, act=act),
        grid=(n // tm, c // tn),
        in_specs=[pl.BlockSpec((tm, k), lambda i, j: (i, 0)),
                  pl.BlockSpec((k, tn), lambda i, j: (0, j))],
        out_specs=pl.BlockSpec((tm, tn), lambda i, j: (i, j)),
        out_shape=jax.ShapeDtypeStruct((n, c), out_dtype),
        compiler_params=_cparams(("parallel", "parallel")),
        name=name,
    )(a, w)


def _merge_out_kernel(g0, g1, g2, g3, y0, y1, y2, y3, wbr_ref, wout_ref, x_ref, gain_ref, *out_refs,
                      last):
    mixed = g0[...].astype(F32) * _nn(y0[...], wbr_ref[0])
    mixed += g1[...].astype(F32) * _nn(y1[...], wbr_ref[1])
    mixed += g2[...].astype(F32) * _nn(y2[...], wbr_ref[2])
    mixed += g3[...].astype(F32) * _nn(y3[...], wbr_ref[3])
    x = x_ref[...] + _nn(mixed.astype(BF16), wout_ref[...])
    ms = jnp.mean(x * x, axis=-1, keepdims=True)
    normed = x * lax.rsqrt(ms + RMS_EPS) * gain_ref[...]
    if last:
        out_refs[0][...] = normed
    else:
        out_refs[0][...] = x
        out_refs[1][...] = normed.astype(BF16)


def _merge_out(gates, ys, w_br, w_out, x, next_gain, last, tm=MERGE_TM):
    n = gates.shape[0]
    gate_specs = [pl.BlockSpec((tm, D_MODEL), functools.partial(lambda i, b: (i, b), b=b))
                  for b in range(4)]
    y_specs = [pl.BlockSpec((tm, BRANCH_W), lambda i: (i, 0)) for _ in range(4)]
    resident = dict(pipeline_mode=pl.Buffered(1))
    row_spec = pl.BlockSpec((tm, D_MODEL), lambda i: (i, 0))
    if last:
        out_specs, out_shape = row_spec, jax.ShapeDtypeStruct((n, D_MODEL), F32)
    else:
        out_specs = (row_spec, row_spec)
        out_shape = (jax.ShapeDtypeStruct((n, D_MODEL), F32), jax.ShapeDtypeStruct((n, D_MODEL), BF16))
    return pl.pallas_call(
        functools.partial(_merge_out_kernel, last=last),
        grid=(n // tm,),
        in_specs=gate_specs + y_specs + [
            pl.BlockSpec((4, BRANCH_W, D_MODEL), lambda i: (0, 0, 0), **resident),
            pl.BlockSpec((D_MODEL, D_MODEL), lambda i: (0, 0), **resident),
            row_spec,
            pl.BlockSpec((1, D_MODEL), lambda i: (0, 0))],
        out_specs=out_specs,
        out_shape=out_shape,
        compiler_params=_cparams(("parallel",)),
        name="merge_out",
    )(gates, gates, gates, gates, *ys, w_br, w_out, x, next_gain.reshape(1, D_MODEL).astype(F32))


def _t5_bucket(dist):
    n = jnp.maximum(dist, 0)
    max_exact = NUM_BUCKETS // 2
    nf = jnp.maximum(n, max_exact).astype(F32)
    large = max_exact + (jnp.log(nf / max_exact) / math.log(MAX_DISTANCE / max_exact)
                         * (NUM_BUCKETS - max_exact)).astype(jnp.int32)
    large = jnp.minimum(large, NUM_BUCKETS - 1)
    return jnp.where(n < max_exact, n, large)


def _bucket_lookup(table, bucket):
    tab = table.T.reshape((table.shape[1],) + (1,) * bucket.ndim + (NUM_BUCKETS,))
    out = jnp.zeros((table.shape[1],) + bucket.shape, F32)
    for b in range(NUM_BUCKETS):
        out = jnp.where(bucket[None] == b, tab[..., b], out)
    return out


def _dilated_bias(bias_a):
    qi = jnp.arange(BAND)[:, None]
    kj = jnp.arange(2 * BAND)[None, :]
    rel = qi + BAND - kj
    valid = (rel >= 0) & (rel <= BAND)
    tabs = []
    for dil in DILATIONS:
        b = _bucket_lookup(bias_a, _t5_bucket(rel * dil))
        tabs.append(jnp.where(valid[None], b, NEG))
    return jnp.stack(tabs)


def _diff_bias(bias_b):
    qi = jnp.arange(B_BT)[:, None]
    kj = jnp.arange(B_BT)[None, :]
    tabs = []
    for d in range(-1, B_NEAR + 1):
        rel = d * B_BT + qi - kj
        b = _bucket_lookup(bias_b, _t5_bucket(rel))
        tabs.append(jnp.where((rel >= 0)[None], b, NEG))
    return jnp.stack(tabs, axis=1)


def _dil_attn_kernel(q_ref, kp_ref, kc_ref, vp_ref, vc_ref, gate_ref, bias_ref, o_ref,
                     kcat, vcat, acc_ref, m_ref, l_ref):
    n = pl.program_id(1)
    kcat[0:A_TILE, :] = kp_ref[0]
    kcat[A_TILE:2 * A_TILE, :] = kc_ref[0]
    vcat[0:A_TILE, :] = vp_ref[0]
    vcat[A_TILE:2 * A_TILE, :] = vc_ref[0]
    lane = lax.broadcasted_iota(jnp.int32, (BAND, HEAD_W), 1)
    lo = lane < A_HEAD_DIM
    col = lax.broadcasted_iota(jnp.int32, (BAND, 2 * BAND), 1)
    nblk_tile = A_TILE // BAND

    def rows(start, r):
        return pl.ds(start, BAND, stride=r) if r > 1 else pl.ds(start, BAND)

    trips = [(p, r, it) for p, r in reversed(list(enumerate(DILATIONS)))
             for it in range(nblk_tile // A_GROUP)]
    first_p = trips[0][0]

    def logits_stage(p, r, it):
        starts, first_blk, vvs, ss = [], [], [], {}
        for g in range(A_GROUP):
            idx = it * A_GROUP + g
            j = idx // r
            start = j * (BAND * r) + idx % r
            starts.append(start)
            first_blk.append(j == 0)
            q = (q_ref[0, rows(start, r), :] * (A_HEAD_DIM ** -0.5)).astype(BF16)
            kk = jnp.concatenate([kcat[rows(A_TILE + start - BAND * r, r), :],
                                  kcat[rows(A_TILE + start, r), :]], axis=0).astype(BF16)
            vvs.append(jnp.concatenate([vcat[rows(A_TILE + start - BAND * r, r), :],
                                        vcat[rows(A_TILE + start, r), :]], axis=0).astype(BF16))
            for hh in range(2):
                qz = jnp.where(lo if hh == 0 else jnp.logical_not(lo), q, jnp.zeros_like(q))
                ss[g, hh] = _nt(qz, kk)
        return starts, first_blk, vvs, ss

    def finish_stage(p, r, staged):
        starts, first_blk, vvs, ss = staged
        es, ms, ls, outs = {}, {}, {}, {}
        for g in range(A_GROUP):
            for hh in range(2):
                s = ss[g, hh] + bias_ref[p, hh]
                if first_blk[g]:
                    s = jnp.where(jnp.logical_and(n == 0, col < BAND), NEG, s)
                m = jnp.max(s, axis=-1, keepdims=True)
                e = jnp.exp(s - m)
                ms[g, hh] = jnp.broadcast_to(m, (BAND, HEAD_W))
                ls[g, hh] = jnp.broadcast_to(jnp.sum(e, axis=-1, keepdims=True), (BAND, HEAD_W))
                es[g, hh] = e.astype(BF16)
        for g in range(A_GROUP):
            for hh in range(2):
                outs[g, hh] = _nn(es[g, hh], vvs[g])
        merged = []
        for g in range(A_GROUP):
            o_new = jnp.where(lo, outs[g, 0], outs[g, 1])
            m_new = jnp.where(lo, ms[g, 0], ms[g, 1])
            l_new = jnp.where(lo, ls[g, 0], ls[g, 1])
            if p != first_p:
                rws = rows(starts[g], r)
                m_old = m_ref[rws, :]
                m_tot = jnp.maximum(m_old, m_new)
                a = jnp.exp(m_old - m_tot)
                b = jnp.exp(m_new - m_tot)
                o_new = a * acc_ref[rws, :] + b * o_new
                l_new = a * l_ref[rws, :] + b * l_new
                m_new = m_tot
            merged.append((o_new, m_new, l_new))
        for g in range(A_GROUP):
            rws = rows(starts[g], r)
            acc_ref[rws, :], m_ref[rws, :], l_ref[rws, :] = merged[g]

    staged = logits_stage(*trips[0])
    for i, (p, r, _) in enumerate(trips):
        nxt = logits_stage(*trips[i + 1]) if i + 1 < len(trips) else None
        finish_stage(p, r, staged)
        staged = nxt

    o_ref[0] = (acc_ref[...] / l_ref[...] * _silu(gate_ref[0])).astype(o_ref.dtype)


def _dil_attn(pf, bias_tab, bsz, t):
    nt = t // A_TILE
    hp = BRANCH_W // HEAD_W
    blk = (1, A_TILE, HEAD_W)

    def cur(cb):
        return pl.BlockSpec(blk, lambda b, n, h: (b, n, cb * hp + h))

    def prev(cb):
        return pl.BlockSpec(blk, lambda b, n, h: (b, jnp.maximum(n - 1, 0), cb * hp + h))

    return pl.pallas_call(
        _dil_attn_kernel,
        grid=(bsz, nt, hp),
        in_specs=[cur(PF_AQ), prev(PF_AK), cur(PF_AK), prev(PF_AV), cur(PF_AV), cur(PF_AGATE),
                  pl.BlockSpec((3, 2, BAND, 2 * BAND), lambda b, n, h: (0, h, 0, 0))],
        out_specs=pl.BlockSpec(blk, lambda b, n, h: (b, n, h)),
        out_shape=jax.ShapeDtypeStruct((bsz, t, BRANCH_W), BF16),
        scratch_shapes=[pltpu.VMEM((2 * A_TILE, HEAD_W), F32), pltpu.VMEM((2 * A_TILE, HEAD_W), F32),
                        pltpu.VMEM((A_TILE, HEAD_W), F32), pltpu.VMEM((A_TILE, HEAD_W), F32),
                        pltpu.VMEM((A_TILE, HEAD_W), F32)],
        compiler_params=_cparams(("parallel", "parallel", "parallel")),
        name="dilated_attn",
    )(pf, pf, pf, pf, pf, pf, bias_tab)


def _diff_attn_kernel(scal_ref, q_ref, k_ref, v_ref, gate_ref, bias_ref, gain_ref, o_ref,
                      kt_ref, vx_ref, s0_ref, s1_ref, p0_ref, p1_ref, al0_ref, al1_ref,
                      acc_ref, m_ref):
    h = pl.program_id(1)
    qi = pl.program_id(2)
    nkb = kt_ref.shape[0]
    nrh = B_TQ // B_BT
    nsub = B_TK // B_BT
    ncol = B_TK // HEAD_W

    @pl.when(qi == 0)
    def _():
        def tr(kb, c):
            rows = pl.ds(pl.multiple_of(kb * B_TK, B_TK), B_TK)
            kt_ref[kb] = k_ref[0, rows, :].astype(F32).T.astype(BF16)
            vx_ref[rows, 0:HEAD_W] = v_ref[0, rows, :]
            vx_ref[rows, HEAD_W:2 * HEAD_W] = jnp.ones((B_TK, HEAD_W), BF16)
            return c
        lax.fori_loop(0, nkb, tr, 0)

    lane = lax.broadcasted_iota(jnp.int32, (B_TQ, HEAD_W), 1)
    lo = lane < DIFF_QK
    q = q_ref[0] * (DIFF_QK ** -0.5)
    zero = jnp.zeros_like(q)
    qz = (jnp.where(lo, q, zero), jnp.where(lo, zero, q))
    cfar = scal_ref[2 + h]
    nk = ((qi + 1) * nrh - 1) // nsub + 1
    n_far = jnp.maximum((nrh * qi - B_NEAR + 1) // nsub, 0)
    per_sub = B_BT // HEAD_W

    s_bufs, p_bufs, al_bufs = (s0_ref, s1_ref), (p0_ref, p1_ref), (al0_ref, al1_ref)
    acc_ref[...] = jnp.zeros_like(acc_ref)
    m_ref[...] = jnp.full_like(m_ref, NEG)
    al1_ref[...] = jnp.ones_like(al1_ref)
    p1_ref[...] = jnp.zeros_like(p1_ref)

    row_halves = [slice(rh * B_BT, (rh + 1) * B_BT) for rh in range(nrh)]

    def qk(kb, par):
        kt = kt_ref[jnp.minimum(kb, nk - 1)]
        for mi in range(2):
            for rows in row_halves:
                s_bufs[par][mi, rows, :] = _nn(qz[mi][rows], kt)

    def pv(kb, par):
        kb = jnp.clip(kb, 0, nk - 1)
        vx = vx_ref[pl.ds(pl.multiple_of(kb * B_TK, B_TK), B_TK), :]
        for mi in range(2):
            for rows in row_halves:
                a = al_bufs[par][mi, rows]
                upd = _nn(p_bufs[par][mi, rows, :], vx)
                acc_ref[mi, rows] = jnp.concatenate([a, a], axis=1) * acc_ref[mi, rows] + upd

    def col(par, mi, c, kb, near):
        x = s_bufs[par][mi, :, c * HEAD_W:(c + 1) * HEAD_W]
        if near:
            cc = (c % per_sub) * HEAD_W
            tiles = [jnp.clip(nrh * qi + rh - nsub * kb - c // per_sub, -1, B_NEAR) + 1
                     for rh in range(nrh)]
            x = x + jnp.concatenate([bias_ref[0, tl, :, cc:cc + HEAD_W] for tl in tiles], axis=0)
        return x

    def softmax(kb, par, near):
        shift = 0.0 if near else cfar
        m_sub = []
        for mi in range(2):
            mx = col(par, mi, 0, kb, near)
            for c in range(1, ncol):
                mx = jnp.maximum(mx, col(par, mi, c, kb, near))
            m_prev = m_ref[mi]
            m_next = jnp.maximum(m_prev, jnp.max(mx, axis=-1, keepdims=True) + shift)
            al_bufs[par][mi] = jnp.exp(m_prev - m_next)
            m_ref[mi] = m_next
            m_sub.append(m_next - shift)
        for mi in range(2):
            for c in range(ncol):
                e = jnp.exp(col(par, mi, c, kb, near) - m_sub[mi])
                p_bufs[par][mi, :, c * HEAD_W:(c + 1) * HEAD_W] = e.astype(BF16)

    def pair(u, near):
        for par in range(2):
            t = 2 * u + par
            qk(t + 1, 1 - par)
            pv(t - 1, 1 - par)
            softmax(t, par, near)

    def far_body(u, c):
        pair(u, False)
        return c

    def near_body(u, c):
        pair(u, True)
        return c

    far_pairs = n_far // 2
    full_pairs = nk // 2
    qk(0, 0)
    lax.fori_loop(0, far_pairs, far_body, 0)
    lax.fori_loop(far_pairs, full_pairs, near_body, 0)

    @pl.when(nk % 2 == 1)
    def _():
        pv(nk - 2, 1)
        softmax(nk - 1, 0, True)
        pv(nk - 1, 0)

    @pl.when(nk % 2 == 0)
    def _():
        pv(nk - 1, 1)

    lam = scal_ref[0]
    a1, a2 = acc_ref[0], acc_ref[1]
    o = a1[:, :HEAD_W] / a1[:, HEAD_W:] - lam * (a2[:, :HEAD_W] / a2[:, HEAD_W:])
    ms = jnp.mean(o * o, axis=-1, keepdims=True)
    o = o * lax.rsqrt(ms + RMS_EPS) * gain_ref[...] * scal_ref[1]
    o_ref[0] = (o * _silu(gate_ref[0])).astype(o_ref.dtype)


def _diff_attn(pb, pf, bias_tab, scal, gain, bsz, t):
    nh = BRANCH_W // HEAD_W
    nq = t // B_TQ
    return pl.pallas_call(
        _diff_attn_kernel,
        grid=(bsz, nh, nq),
        in_specs=[pl.BlockSpec(memory_space=pltpu.SMEM),
                  pl.BlockSpec((1, B_TQ, HEAD_W), lambda b, h, i: (b, i, h)),
                  pl.BlockSpec((1, t, HEAD_W), lambda b, h, i: (b, 0, nh + h)),
                  pl.BlockSpec((1, t, HEAD_W), lambda b, h, i: (b, 0, 2 * nh + h)),
                  pl.BlockSpec((1, B_TQ, HEAD_W), lambda b, h, i: (b, i, PF_BGATE * nh + h)),
                  pl.BlockSpec((1, B_NEAR + 2, B_BT, B_BT), lambda b, h, i: (h, 0, 0, 0)),
                  pl.BlockSpec((1, HEAD_W), lambda b, h, i: (0, 0))],
        out_specs=pl.BlockSpec((1, B_TQ, HEAD_W), lambda b, h, i: (b, i, h)),
        out_shape=jax.ShapeDtypeStruct((bsz, t, BRANCH_W), BF16),
        scratch_shapes=[pltpu.VMEM((t // B_TK, HEAD_W, B_TK), BF16),
                        pltpu.VMEM((t, 2 * HEAD_W), BF16),
                        pltpu.VMEM((2, B_TQ, B_TK), F32),
                        pltpu.VMEM((2, B_TQ, B_TK), F32),
                        pltpu.VMEM((2, B_TQ, B_TK), BF16),
                        pltpu.VMEM((2, B_TQ, B_TK), BF16),
                        pltpu.VMEM((2, B_TQ, HEAD_W), F32),
                        pltpu.VMEM((2, B_TQ, HEAD_W), F32),
                        pltpu.VMEM((2, B_TQ, 2 * HEAD_W), F32),
                        pltpu.VMEM((2, B_TQ, HEAD_W), F32)],
        compiler_params=_cparams(("arbitrary", "arbitrary", "arbitrary")),
        name="diff_attn",
    )(scal, pb, pb, pb, pf, bias_tab, gain.reshape(1, HEAD_W).astype(F32))


def _delta_kernel(qkv_ref, z_ref, small_ref, conv_ref, par_ref, gain_ref, o_ref,
                  xe_ref, s_ref):
    tc = qkv_ref.shape[1]
    nchunk = tc // CHUNK

    @pl.when(pl.program_id(1) == 0)
    def _():
        xe_ref[0:8, :] = jnp.zeros((8, xe_ref.shape[1]), F32)
        s_ref[...] = jnp.zeros_like(s_ref)

    xe_ref[8:8 + tc, :] = qkv_ref[0]
    conv = conv_ref[0:1, :] * xe_ref[pl.ds(8 - 3, tc), :]
    for kk in range(1, CONV_K):
        conv = conv + conv_ref[kk:kk + 1, :] * xe_ref[pl.ds(8 - 3 + kk, tc), :]
    xe_ref[0:8, :] = xe_ref[tc:tc + 8, :]
    c = _silu(conv)

    small = small_ref[0]
    beta_all = _sigmoid(small)
    xa = small + par_ref[1:2, :]
    softplus = jnp.maximum(xa, 0.0) + _log1p_exp_neg_abs(xa)
    g_all = -jnp.exp(par_ref[0:1, :]) * softplus

    ri = lax.broadcasted_iota(jnp.int32, (CHUNK, CHUNK), 0)
    ci = lax.broadcasted_iota(jnp.int32, (CHUNK, CHUNK), 1)
    tri = ri >= ci
    strict = ri > ci
    eye = (ri == ci).astype(F32)
    blocks = [ri // size == ci // size for size in (8, 16, 32, CHUNK)]
    rt = lax.broadcasted_iota(jnp.int32, (tc, tc), 0)
    ct = lax.broadcasted_iota(jnp.int32, (tc, tc), 1)
    tri_blocks = jnp.logical_and(rt >= ct, rt // CHUNK == ct // CHUNK).astype(F32)
    sel = (lax.broadcasted_iota(jnp.int32, (8, HEAD_W), 1)
           == lax.broadcasted_iota(jnp.int32, (8, HEAD_W), 0) + 4).astype(F32)
    nh = BRANCH_W // HEAD_W
    units = [(ch, h) for ch in range(nchunk) for h in range(nh)]

    gcum_all = _nn(tri_blocks, g_all, HI)
    grow_all = _nt(sel, gcum_all, HI)

    qs, ks, vs, bcols, gcols, glasts, gammas, ms_ = {}, {}, {}, {}, {}, {}, {}, {}
    for h in range(nh):
        qf = c[:, h * HEAD_W:(h + 1) * HEAD_W]
        kf = c[:, BRANCH_W + h * HEAD_W:BRANCH_W + (h + 1) * HEAD_W]
        qf = qf * lax.rsqrt(jnp.sum(qf * qf, axis=-1, keepdims=True) + 1e-6) * (HEAD_W ** -0.5)
        kf = kf * lax.rsqrt(jnp.sum(kf * kf, axis=-1, keepdims=True) + 1e-6)
        for ch in range(nchunk):
            r0 = ch * CHUNK
            u_ = (ch, h)
            qs[u_] = qf[r0:r0 + CHUNK]
            ks[u_] = kf[r0:r0 + CHUNK]
            vs[u_] = c[r0:r0 + CHUNK, 2 * BRANCH_W + h * HEAD_W:2 * BRANCH_W + (h + 1) * HEAD_W]
            bcols[u_] = beta_all[r0:r0 + CHUNK, h:h + 1]
            gcols[u_] = gcum_all[r0:r0 + CHUNK, 4 + h:5 + h]
            glasts[u_] = gcum_all[r0 + CHUNK - 1:r0 + CHUNK, 4 + h:5 + h]
            grow = grow_all[h:h + 1, r0:r0 + CHUNK]
            gammas[u_] = jnp.exp(jnp.where(tri, gcols[u_] - grow, NEG))
    for u_ in units:
        kkt = _nt_acc(ks[u_], ks[u_])
        ms_[u_] = jnp.where(strict, bcols[u_] * kkt * gammas[u_], 0.0)

    def prod(a, b):
        return {u_: _nn(_split_cols(a[u_]), _split_rows(b[u_])) for u_ in units}

    pw = {u_: jnp.where(blocks[0], -ms_[u_], 0.0) for u_ in units}
    tinv = {u_: eye + pw[u_] for u_ in units}
    for _ in range(2):
        pw = prod(pw, pw)
        step = prod(tinv, pw)
        tinv = {u_: tinv[u_] + step[u_] for u_ in units}
    for inner, outer in zip(blocks[:-1], blocks[1:]):
        coupling = {u_: jnp.where(jnp.logical_and(outer, jnp.logical_not(inner)), ms_[u_], 0.0)
                    for u_ in units}
        step = prod(tinv, prod(coupling, tinv))
        tinv = {u_: tinv[u_] - step[u_] for u_ in units}

    us, ws, aqks, qds, kds = {}, {}, {}, {}, {}
    for u_ in units:
        egc = jnp.exp(gcols[u_])
        rhs = jnp.concatenate([vs[u_] * bcols[u_], ks[u_] * (bcols[u_] * egc)], axis=1)
        uw = _nn(_split_cols(tinv[u_]), _split_rows(rhs))
        us[u_], ws[u_] = uw[:, :HEAD_W], uw[:, HEAD_W:]
        aqks[u_] = _nt(qs[u_], ks[u_]) * gammas[u_]
        qds[u_] = qs[u_] * egc
        kds[u_] = ks[u_] * jnp.exp(glasts[u_] - gcols[u_])

    states = [s_ref[h] for h in range(nh)]
    for ch in range(nchunk):
        r0 = ch * CHUNK
        v_news, o_inters = {}, {}
        for h in range(nh):
            u_ = (ch, h)
            s = states[h]
            s_hi, s_lo = _hi_lo(s)
            w_hi, w_lo = _hi_lo(ws[u_])
            ws_prod = (_nn(jnp.concatenate([w_hi, w_lo], axis=1), jnp.concatenate([s_hi, s_hi], axis=0))
                       + _nn(w_hi, s_lo))
            v_news[h] = us[u_] - ws_prod
            o_inters[h] = _nn(qds[u_], s)
        for h in range(nh):
            u_ = (ch, h)
            v_new = v_news[h]
            o = o_inters[h] + _nn(aqks[u_], v_new)
            states[h] = (jnp.exp(glasts[u_]) * states[h]
                         + _tn(_split_rows_lhs(kds[u_]), _split_rows(v_new)))
            if ch == nchunk - 1:
                s_ref[h] = states[h]
            ms = jnp.mean(o * o, axis=-1, keepdims=True)
            o = o * lax.rsqrt(ms + RMS_EPS) * gain_ref[...]
            zg = z_ref[0, r0:r0 + CHUNK, h * HEAD_W:(h + 1) * HEAD_W]
            o_ref[0, r0:r0 + CHUNK, h * HEAD_W:(h + 1) * HEAD_W] = (o * _silu(zg)).astype(o_ref.dtype)


def _delta_net(pf, small, conv_w, par, gain, bsz, t, tc=CHUNK):
    cw = 3 * BRANCH_W
    return pl.pallas_call(
        _delta_kernel,
        grid=(bsz, t // tc),
        in_specs=[pl.BlockSpec((1, tc, cw), lambda b, i: (b, i, 0)),
                  pl.BlockSpec((1, tc, BRANCH_W), lambda b, i: (b, i, PF_CZ)),
                  pl.BlockSpec((1, tc, HEAD_W), lambda b, i: (b, i, 0)),
                  pl.BlockSpec((8, cw), lambda b, i: (0, 0)),
                  pl.BlockSpec((8, HEAD_W), lambda b, i: (0, 0)),
                  pl.BlockSpec((1, HEAD_W), lambda b, i: (0, 0))],
        out_specs=pl.BlockSpec((1, tc, BRANCH_W), lambda b, i: (b, i, 0)),
        out_shape=jax.ShapeDtypeStruct((bsz, t, BRANCH_W), BF16),
        scratch_shapes=[pltpu.VMEM((tc + 8, cw), F32),
                        pltpu.VMEM((BRANCH_W // HEAD_W, HEAD_W, HEAD_W), F32)],
        compiler_params=_cparams(("arbitrary", "arbitrary")),
        name="delta_net",
    )(pf, pf, small, conv_w, par, gain.reshape(1, HEAD_W).astype(F32))


def _hgrn_kernel(q_ref, f_ref, i_ref, gate_ref, lb_ref, gain_ref, o_ref, st_ref):
    tc = q_ref.shape[1]
    nchunk = tc // CHUNK
    nh = BRANCH_W // HEAD_W
    nsub = CHUNK // SUB

    @pl.when(pl.program_id(1) == 0)
    def _():
        st_ref[...] = jnp.zeros_like(st_ref)

    lb = lb_ref[...]
    df = f_ref[0]
    log_sig = jnp.minimum(df, 0.0) - _log1p_exp_neg_abs(df)
    a = jnp.log(lb)
    b = jnp.log1p(-lb) + log_sig
    logf_all = jnp.maximum(a, b) + _log1p_exp_neg_abs(a - b)
    k_all = (1.0 - lb) * _sigmoid(-df)

    ri = lax.broadcasted_iota(jnp.int32, (CHUNK, 3 * CHUNK), 0)
    ci = lax.broadcasted_iota(jnp.int32, (CHUNK, 3 * CHUNK), 1)
    tri3 = jnp.where(ri >= ci % CHUNK, 1.0, 0.0).astype(BF16)
    sub_row = lax.broadcasted_iota(jnp.int32, (HALF, HEAD_W), 0)
    units = [(ch, h) for ch in range(nchunk) for h in range(nh)]

    bcs = []
    for ch in range(nchunk):
        lf = logf_all[ch * CHUNK:(ch + 1) * CHUNK]
        hi = lf.astype(BF16)
        r1 = lf - hi.astype(F32)
        mid = r1.astype(BF16)
        lo = (r1 - mid.astype(F32)).astype(BF16)
        bcs.append(_nn(tri3, jnp.concatenate([hi, mid, lo], axis=0)))

    qs, ks, vs, bcu, bls, qes, dstate, intra = {}, {}, {}, {}, {}, {}, {}, {}
    for u_ in units:
        ch, h = u_
        r0 = ch * CHUNK
        cs = slice(h * HEAD_W, (h + 1) * HEAD_W)
        qs[u_] = q_ref[0, r0:r0 + CHUNK, cs]
        vs[u_] = i_ref[0, r0:r0 + CHUNK, cs]
        ks[u_] = k_all[r0:r0 + CHUNK, cs]
        bcu[u_] = bcs[ch][:, cs]
        bls[u_] = bcu[u_][CHUNK - 1:CHUNK, :]
        qes[u_] = (qs[u_] * jnp.exp(bcu[u_])).astype(BF16)
        k_end = ks[u_] * jnp.exp(bls[u_] - bcu[u_])
        dstate[u_] = _tn(_split_rows_lhs(vs[u_]), _split_rows(k_end))

    row_c = lax.broadcasted_iota(jnp.int32, (CHUNK, HEAD_W), 0)
    second_half = (row_c // HALF) % 2 == 1
    rr = lax.broadcasted_iota(jnp.int32, (CHUNK, CHUNK), 0)
    cc = lax.broadcasted_iota(jnp.int32, (CHUNK, CHUNK), 1)
    same_block = rr // SUB == cc // SUB
    logits, v_bfs = {}, {}
    for u_ in units:
        q, k, bc = qs[u_], ks[u_], bcu[u_]
        v_bfs[u_] = vs[u_].astype(BF16)
        for si in range(1, nsub):
            rs = slice(si * SUB, (si + 1) * SUB)
            ref_row = bc[si * SUB - 1:si * SUB, :]
            q_dec = (q[rs] * jnp.exp(bc[rs] - ref_row)).astype(BF16)
            k_dec = (k[:si * SUB] * jnp.exp(ref_row - bc[:si * SUB])).astype(BF16)
            logits[u_, si] = _nt(q_dec, k_dec)
        mid_rows = jnp.concatenate(
            [jnp.broadcast_to(bc[si * SUB + HALF - 1:si * SUB + HALF, :], (SUB, HEAD_W))
             for si in range(nsub)], axis=0)
        e_mid = jnp.exp(-jnp.abs(bc - mid_rows))
        q_mid = jnp.where(second_half, q * e_mid, 0.0).astype(BF16)
        k_mid = jnp.where(second_half, 0.0, k * e_mid).astype(BF16)
        logits[u_, "mid"] = jnp.where(same_block, _nt(q_mid, k_mid), 0.0)
    for u_ in units:
        pieces = [jnp.zeros((SUB, HEAD_W), F32)]
        for si in range(1, nsub):
            pieces.append(_nn(logits[u_, si].astype(BF16), v_bfs[u_][:si * SUB]))
        o_b = _nn(logits[u_, "mid"].astype(BF16), v_bfs[u_])
        intra[u_] = jnp.concatenate(pieces, axis=0) + o_b

    diag = {u_: [] for u_ in units}
    for sb in range(CHUNK // HALF):
        rs = slice(sb * HALF, (sb + 1) * HALF)
        o_s = {u_: jnp.zeros((HALF, HEAD_W), F32) for u_ in units}
        for j in range(HALF):
            jj = sb * HALF + j
            for u_ in units:
                q, k, v, bc = qs[u_], ks[u_], vs[u_], bcu[u_]
                e = jnp.exp(jnp.where(sub_row >= j, bc[rs] - bc[jj:jj + 1, :], NEG))
                a_col = jnp.sum(q[rs] * k[jj:jj + 1, :] * e, axis=-1, keepdims=True)
                o_s[u_] = o_s[u_] + a_col * v[jj:jj + 1, :]
        for u_ in units:
            diag[u_].append(o_s[u_])
    for u_ in units:
        intra[u_] = intra[u_] + jnp.concatenate(diag[u_], axis=0)

    states = [st_ref[h] for h in range(nh)]
    for u_ in units:
        ch, h = u_
        r0 = ch * CHUNK
        cs = slice(h * HEAD_W, (h + 1) * HEAD_W)
        o = _nt(qes[u_], states[h].astype(BF16)) + intra[u_]
        states[h] = states[h] * jnp.exp(bls[u_]) + dstate[u_]
        if ch == nchunk - 1:
            st_ref[h] = states[h]
        ms = jnp.mean(o * o, axis=-1, keepdims=True)
        o = o * lax.rsqrt(ms + RMS_EPS) * gain_ref[...]
        o_ref[0, r0:r0 + CHUNK, cs] = (o * _silu(gate_ref[0, r0:r0 + CHUNK, cs])).astype(o_ref.dtype)


def _hgrn(pf, lb, gain, bsz, t, tc=CHUNK):
    def spec(cb):
        return pl.BlockSpec((1, tc, BRANCH_W), lambda b, i: (b, i, cb))

    return pl.pallas_call(
        _hgrn_kernel,
        grid=(bsz, t // tc),
        in_specs=[spec(PF_DQ), spec(PF_DF), spec(PF_DI), spec(PF_DGATE),
                  pl.BlockSpec((1, BRANCH_W), lambda b, i: (0, 0)),
                  pl.BlockSpec((1, HEAD_W), lambda b, i: (0, 0))],
        out_specs=pl.BlockSpec((1, tc, BRANCH_W), lambda b, i: (b, i, 0)),
        out_shape=jax.ShapeDtypeStruct((bsz, t, BRANCH_W), BF16),
        scratch_shapes=[pltpu.VMEM((BRANCH_W // HEAD_W, HEAD_W, HEAD_W), F32)],
        compiler_params=_cparams(("arbitrary", "arbitrary")),
        name="hgrn2",
    )(pf, pf, pf, pf, lb.reshape(1, BRANCH_W), gain.reshape(1, HEAD_W).astype(F32))


def _split_w_in(w):
    o = 0
    parts = {}
    for name, width in (("a_q", 512), ("a_k", 512), ("a_v", 512), ("a_gate", 512),
                        ("b_q", 512), ("b_k", 512), ("b_v", 512), ("b_gate", 512),
                        ("c_qkv", 1536), ("c_z", 512), ("c_beta", 4), ("c_a", 4),
                        ("d_q", 512), ("d_f", 512), ("d_i", 512), ("d_gate", 512),
                        ("merge", 4 * D_MODEL)):
        parts[name] = w[:, o:o + width]
        o += width
    w_f = jnp.concatenate([parts[k] for k in ("c_qkv", "c_z", "a_gate", "b_gate", "d_q", "d_f",
                                              "d_i", "d_gate", "a_q", "a_k", "a_v")], axis=1)
    w_b = jnp.concatenate([parts[k] for k in ("b_q", "b_k", "b_v")], axis=1)
    w_s = jnp.concatenate([parts["c_beta"], parts["c_a"],
                           jnp.zeros((w.shape[0], HEAD_W - 8), w.dtype)], axis=1)
    return w_f.astype(BF16), w_b.astype(BF16), w_s.astype(BF16), parts["merge"].astype(BF16)


def kernel(x, norm_gain, w_in, rel_bias, diff_lambda, diff_subln_gain, dn_conv, dn_a_log, dn_dt_bias,
           dn_norm_gain, hg_lb_logits, hg_norm_gain, w_branch, w_out, final_gain):
    bsz, t, d = x.shape
    n = bsz * t
    depth = w_in.shape[0]
    lb_p = jax.nn.softmax(hg_lb_logits.astype(F32), axis=0)
    hg_lb = jnp.clip(jnp.cumsum(lb_p, axis=0) - lb_p[0], 0.0, 1.0)
    bias_a = _dilated_bias(rel_bias[:, :8].astype(F32))
    bias_b = _diff_bias(rel_bias[:, 8:].astype(F32))
    cfar = rel_bias[NUM_BUCKETS - 1, 8:].astype(F32)

    xf = x.reshape(n, d).astype(F32)
    h = _rmsnorm(xf, norm_gain[0], BF16)
    for layer in range(depth):
        w_f, w_b, w_s, w_g = _split_w_in(w_in[layer])
        pf = _mm(h, w_f, F32, name="proj_f32").reshape(bsz, t, PF_COLS)
        pb = _mm(h, w_b, BF16, name="proj_bf16").reshape(bsz, t, 3 * BRANCH_W)
        ps = _mm(h, w_s, F32, name="proj_small").reshape(bsz, t, HEAD_W)
        gates = _mm(h, w_g, BF16, act="sigmoid", name="proj_gates")

        y_a = _dil_attn(pf, bias_a, bsz, t)

        lam_init = 0.8 - 0.6 * math.exp(-0.3 * layer)
        lq1, lk1, lq2, lk2 = diff_lambda[layer].astype(F32)
        lam = jnp.exp(jnp.sum(lq1 * lk1)) - jnp.exp(jnp.sum(lq2 * lk2)) + lam_init
        scal = jnp.concatenate([jnp.stack([lam, jnp.asarray(1.0 - lam_init, F32)]), cfar,
                                jnp.zeros((2,), F32)])
        y_b = _diff_attn(pb, pf, bias_b, scal, diff_subln_gain[layer], bsz, t)

        conv_w = jnp.concatenate([dn_conv[layer].astype(F32),
                                  jnp.zeros((8 - CONV_K, 3 * BRANCH_W), F32)], axis=0)
        par = jnp.zeros((8, HEAD_W), F32)
        par = par.at[0, 4:8].set(dn_a_log[layer].astype(F32)).at[1, 4:8].set(dn_dt_bias[layer].astype(F32))
        y_c = _delta_net(pf, ps, conv_w, par, dn_norm_gain[layer], bsz, t, tc=DELTA_TILE)

        y_d = _hgrn(pf, hg_lb[layer], hg_norm_gain[layer], bsz, t, tc=HGRN_TILE)

        ys = [y.reshape(n, BRANCH_W) for y in (y_a, y_b, y_c, y_d)]
        last = layer == depth - 1
        res = _merge_out(gates, ys, w_branch[layer].astype(BF16), w_out[layer].astype(BF16), xf,
                         final_gain if last else norm_gain[layer + 1], last)
        if last:
            out = res
        else:
            xf, h = res
    return out.reshape(bsz, t, d).astype(x.dtype)
```

```python
import functools
import math

import jax
import jax.numpy as jnp
from jax import lax
from jax.experimental import pallas as pl
from jax.experimental.pallas import tpu as pltpu

F32 = jnp.float32
BF16 = jnp.bfloat16
HI = lax.Precision.HIGHEST

D_MODEL = 2048
BRANCH_W = 512
HEAD_W = 128
A_HEAD_DIM = 64
DILATIONS = (1, 4, 16)
BAND = 128
A_TILE = BAND * DILATIONS[-1]
A_GROUP = 4
DIFF_QK = 64
B_BT = 256
B_TQ = 512
B_TK = 512
B_NEAR = 2048 // B_BT + 1
CHUNK = 64
DELTA_TILE = 256
HGRN_TILE = 256
SUB = 16
HALF = SUB // 2
CONV_K = 4
NUM_BUCKETS = 32
MAX_DISTANCE = 2048
RMS_EPS = 1e-6
NEG = -1e30
VMEM_LIMIT = 56 * 1024 * 1024
MM_TM, MM_TN = 2048, 1024
MERGE_TM = 256

PF_CQKV, PF_CZ, PF_AGATE, PF_BGATE, PF_DQ, PF_DF, PF_DI, PF_DGATE, PF_AQ, PF_AK, PF_AV = 0, 3, 4, 5, 6, 7, 8, 9, 10, 11, 12
PF_COLS = 13 * 512


def _cparams(sem):
    return pltpu.CompilerParams(dimension_semantics=sem, vmem_limit_bytes=VMEM_LIMIT)


def _nt(a, b, precision=None):
    return lax.dot_general(a, b, (((1,), (1,)), ((), ())), precision=precision,
                           preferred_element_type=F32)


def _tn(a, b, precision=None):
    return lax.dot_general(a, b, (((0,), (0,)), ((), ())), precision=precision,
                           preferred_element_type=F32)


def _nn(a, b, precision=None):
    return jnp.dot(a, b, precision=precision, preferred_element_type=F32)


def _sigmoid(x):
    return 0.5 * jnp.tanh(0.5 * x) + 0.5


def _silu(x):
    return x * _sigmoid(x)


def _log1p_exp_neg_abs(x):
    return jnp.log1p(jnp.exp(-jnp.abs(x)))


def _hi_lo(x):
    bits = lax.bitcast_convert_type(x, jnp.int32)
    hi = lax.bitcast_convert_type(bits & jnp.int32(-65536), F32)
    return hi, x - hi


def _split_cols(a):
    hl = jnp.concatenate(_hi_lo(a), axis=1)
    return jnp.concatenate([hl, hl], axis=1)


def _split_rows(b):
    hi, lo = _hi_lo(b)
    return jnp.concatenate([hi, hi, lo, lo], axis=0)


def _split_rows_lhs(a):
    hi, lo = _hi_lo(a)
    return jnp.concatenate([hi, lo, hi, lo], axis=0)


def _nt_acc(a, b):
    ah, al = _hi_lo(a)
    bh, bl = _hi_lo(b)
    return (_nt(jnp.concatenate([ah, al], axis=1), jnp.concatenate([bh, bh], axis=1))
            + _nt(ah, bl))


def _rmsnorm_kernel(x_ref, g_ref, o_ref):
    x = x_ref[...]
    ms = jnp.mean(x * x, axis=-1, keepdims=True)
    o_ref[...] = (x * lax.rsqrt(ms + RMS_EPS) * g_ref[...]).astype(o_ref.dtype)


def _rmsnorm(x, gain, out_dtype, tm=512):
    n, d = x.shape
    return pl.pallas_call(
        _rmsnorm_kernel,
        grid=(n // tm,),
        in_specs=[pl.BlockSpec((tm, d), lambda i: (i, 0)),
                  pl.BlockSpec((1, d), lambda i: (0, 0))],
        out_specs=pl.BlockSpec((tm, d), lambda i: (i, 0)),
        out_shape=jax.ShapeDtypeStruct((n, d), out_dtype),
        compiler_params=_cparams(("parallel",)),
        name="rmsnorm",
    )(x, gain.reshape(1, d).astype(F32))


def _mm_kernel(a_ref, w_ref, o_ref, *, act):
    acc = _nn(a_ref[...], w_ref[...])
    if act == "sigmoid":
        acc = _sigmoid(acc)
    o_ref[...] = acc.astype(o_ref.dtype)


def _mm(a, w, out_dtype, act=None, tm=MM_TM, tn=MM_TN, name="mm"):
    n, k = a.shape
    c = w.shape[1]
    while c % tn:
        tn //= 2
    return pl.pallas_call(
        functools.partial(_mm_kernel, act=act),
        grid=(n // tm, c // tn),
        in_specs=[pl.BlockSpec((tm, k), lambda i, j: (i, 0)),
                  pl.BlockSpec((k, tn), lambda i, j: (0, j))],
        out_specs=pl.BlockSpec((tm, tn), lambda i, j: (i, j)),
        out_shape=jax.ShapeDtypeStruct((n, c), out_dtype),
        compiler_params=_cparams(("parallel", "parallel")),
        name=name,
    )(a, w)


def _merge_out_kernel(g0, g1, g2, g3, y0, y1, y2, y3, wbr_ref, wout_ref, x_ref, gain_ref, *out_refs,
                      last):
    mixed = g0[...].astype(F32) * _nn(y0[...], wbr_ref[0])
    mixed += g1[...].astype(F32) * _nn(y1[...], wbr_ref[1])
    mixed += g2[...].astype(F32) * _nn(y2[...], wbr_ref[2])
    mixed += g3[...].astype(F32) * _nn(y3[...], wbr_ref[3])
    x = x_ref[...] + _nn(mixed.astype(BF16), wout_ref[...])
    ms = jnp.mean(x * x, axis=-1, keepdims=True)
    normed = x * lax.rsqrt(ms + RMS_EPS) * gain_ref[...]
    if last:
        out_refs[0][...] = normed
    else:
        out_refs[0][...] = x
        out_refs[1][...] = normed.astype(BF16)


def _merge_out(gates, ys, w_br, w_out, x, next_gain, last, tm=MERGE_TM):
    n = gates.shape[0]
    gate_specs = [pl.BlockSpec((tm, D_MODEL), functools.partial(lambda i, b: (i, b), b=b))
                  for b in range(4)]
    y_specs = [pl.BlockSpec((tm, BRANCH_W), lambda i: (i, 0)) for _ in range(4)]
    resident = dict(pipeline_mode=pl.Buffered(1))
    row_spec = pl.BlockSpec((tm, D_MODEL), lambda i: (i, 0))
    if last:
        out_specs, out_shape = row_spec, jax.ShapeDtypeStruct((n, D_MODEL), F32)
    else:
        out_specs = (row_spec, row_spec)
        out_shape = (jax.ShapeDtypeStruct((n, D_MODEL), F32), jax.ShapeDtypeStruct((n, D_MODEL), BF16))
    return pl.pallas_call(
        functools.partial(_merge_out_kernel, last=last),
        grid=(n // tm,),
        in_specs=gate_specs + y_specs + [
            pl.BlockSpec((4, BRANCH_W, D_MODEL), lambda i: (0, 0, 0), **resident),
            pl.BlockSpec((D_MODEL, D_MODEL), lambda i: (0, 0), **resident),
            row_spec,
            pl.BlockSpec((1, D_MODEL), lambda i: (0, 0))],
        out_specs=out_specs,
        out_shape=out_shape,
        compiler_params=_cparams(("parallel",)),
        name="merge_out",
    )(gates, gates, gates, gates, *ys, w_br, w_out, x, next_gain.reshape(1, D_MODEL).astype(F32))


def _t5_bucket(dist):
    n = jnp.maximum(dist, 0)
    max_exact = NUM_BUCKETS // 2
    nf = jnp.maximum(n, max_exact).astype(F32)
    large = max_exact + (jnp.log(nf / max_exact) / math.log(MAX_DISTANCE / max_exact)
                         * (NUM_BUCKETS - max_exact)).astype(jnp.int32)
    large = jnp.minimum(large, NUM_BUCKETS - 1)
    return jnp.where(n < max_exact, n, large)


def _bucket_lookup(table, bucket):
    tab = table.T.reshape((table.shape[1],) + (1,) * bucket.ndim + (NUM_BUCKETS,))
    out = jnp.zeros((table.shape[1],) + bucket.shape, F32)
    for b in range(NUM_BUCKETS):
        out = jnp.where(bucket[None] == b, tab[..., b], out)
    return out


def _dilated_bias(bias_a):
    qi = jnp.arange(BAND)[:, None]
    kj = jnp.arange(2 * BAND)[None, :]
    rel = qi + BAND - kj
    valid = (rel >= 0) & (rel <= BAND)
    tabs = []
    for dil in DILATIONS:
        b = _bucket_lookup(bias_a, _t5_bucket(rel * dil))
        tabs.append(jnp.where(valid[None], b, NEG))
    return jnp.stack(tabs)


def _diff_bias(bias_b):
    qi = jnp.arange(B_BT)[:, None]
    kj = jnp.arange(B_BT)[None, :]
    tabs = []
    for d in range(-1, B_NEAR + 1):
        rel = d * B_BT + qi - kj
        b = _bucket_lookup(bias_b, _t5_bucket(rel))
        tabs.append(jnp.where((rel >= 0)[None], b, NEG))
    return jnp.stack(tabs, axis=1)


def _dil_attn_kernel(q_ref, kp_ref, kc_ref, vp_ref, vc_ref, gate_ref, bias_ref, o_ref,
                     kcat, vcat, acc_ref, m_ref, l_ref):
    n = pl.program_id(1)
    kcat[0:A_TILE, :] = kp_ref[0]
    kcat[A_TILE:2 * A_TILE, :] = kc_ref[0]
    vcat[0:A_TILE, :] = vp_ref[0]
    vcat[A_TILE:2 * A_TILE, :] = vc_ref[0]
    lane = lax.broadcasted_iota(jnp.int32, (BAND, HEAD_W), 1)
    lo = lane < A_HEAD_DIM
    col = lax.broadcasted_iota(jnp.int32, (BAND, 2 * BAND), 1)
    nblk_tile = A_TILE // BAND

    def rows(start, r):
        return pl.ds(start, BAND, stride=r) if r > 1 else pl.ds(start, BAND)

    trips = [(p, r, it) for p, r in reversed(list(enumerate(DILATIONS)))
             for it in range(nblk_tile // A_GROUP)]
    first_p = trips[0][0]

    def logits_stage(p, r, it):
        starts, first_blk, vvs, ss = [], [], [], {}
        for g in range(A_GROUP):
            idx = it * A_GROUP + g
            j = idx // r
            start = j * (BAND * r) + idx % r
            starts.append(start)
            first_blk.append(j == 0)
            q = (q_ref[0, rows(start, r), :] * (A_HEAD_DIM ** -0.5)).astype(BF16)
            kk = jnp.concatenate([kcat[rows(A_TILE + start - BAND * r, r), :],
                                  kcat[rows(A_TILE + start, r), :]], axis=0).astype(BF16)
            vvs.append(jnp.concatenate([vcat[rows(A_TILE + start - BAND * r, r), :],
                                        vcat[rows(A_TILE + start, r), :]], axis=0).astype(BF16))
            for hh in range(2):
                qz = jnp.where(lo if hh == 0 else jnp.logical_not(lo), q, jnp.zeros_like(q))
                ss[g, hh] = _nt(qz, kk)
        return starts, first_blk, vvs, ss

    def finish_stage(p, r, staged):
        starts, first_blk, vvs, ss = staged
        es, ms, ls, outs = {}, {}, {}, {}
        for g in range(A_GROUP):
            for hh in range(2):
                s = ss[g, hh] + bias_ref[p, hh]
                if first_blk[g]:
                    s = jnp.where(jnp.logical_and(n == 0, col < BAND), NEG, s)
                m = jnp.max(s, axis=-1, keepdims=True)
                e = jnp.exp(s - m)
                ms[g, hh] = jnp.broadcast_to(m, (BAND, HEAD_W))
                ls[g, hh] = jnp.broadcast_to(jnp.sum(e, axis=-1, keepdims=True), (BAND, HEAD_W))
                es[g, hh] = e.astype(BF16)
        for g in range(A_GROUP):
            for hh in range(2):
                outs[g, hh] = _nn(es[g, hh], vvs[g])
        merged = []
        for g in range(A_GROUP):
            o_new = jnp.where(lo, outs[g, 0], outs[g, 1])
            m_new = jnp.where(lo, ms[g, 0], ms[g, 1])
            l_new = jnp.where(lo, ls[g, 0], ls[g, 1])
            if p != first_p:
                rws = rows(starts[g], r)
                m_old = m_ref[rws, :]
                m_tot = jnp.maximum(m_old, m_new)
                a = jnp.exp(m_old - m_tot)
                b = jnp.exp(m_new - m_tot)
                o_new = a * acc_ref[rws, :] + b * o_new
                l_new = a * l_ref[rws, :] + b * l_new
                m_new = m_tot
            merged.append((o_new, m_new, l_new))
        for g in range(A_GROUP):
            rws = rows(starts[g], r)
            acc_ref[rws, :], m_ref[rws, :], l_ref[rws, :] = merged[g]

    staged = logits_stage(*trips[0])
    for i, (p, r, _) in enumerate(trips):
        nxt = logits_stage(*trips[i + 1]) if i + 1 < len(trips) else None
        finish_stage(p, r, staged)
        staged = nxt

    o_ref[0] = (acc_ref[...] / l_ref[...] * _silu(gate_ref[0])).astype(o_ref.dtype)


def _dil_attn(pf, bias_tab, bsz, t):
    nt = t // A_TILE
    hp = BRANCH_W // HEAD_W
    blk = (1, A_TILE, HEAD_W)

    def cur(cb):
        return pl.BlockSpec(blk, lambda b, n, h: (b, n, cb * hp + h))

    def prev(cb):
        return pl.BlockSpec(blk, lambda b, n, h: (b, jnp.maximum(n - 1, 0), cb * hp + h))

    return pl.pallas_call(
        _dil_attn_kernel,
        grid=(bsz, nt, hp),
        in_specs=[cur(PF_AQ), prev(PF_AK), cur(PF_AK), prev(PF_AV), cur(PF_AV), cur(PF_AGATE),
                  pl.BlockSpec((3, 2, BAND, 2 * BAND), lambda b, n, h: (0, h, 0, 0))],
        out_specs=pl.BlockSpec(blk, lambda b, n, h: (b, n, h)),
        out_shape=jax.ShapeDtypeStruct((bsz, t, BRANCH_W), BF16),
        scratch_shapes=[pltpu.VMEM((2 * A_TILE, HEAD_W), F32), pltpu.VMEM((2 * A_TILE, HEAD_W), F32),
                        pltpu.VMEM((A_TILE, HEAD_W), F32), pltpu.VMEM((A_TILE, HEAD_W), F32),
                        pltpu.VMEM((A_TILE, HEAD_W), F32)],
        compiler_params=_cparams(("parallel", "parallel", "parallel")),
        name="dilated_attn",
    )(pf, pf, pf, pf, pf, pf, bias_tab)


def _diff_attn_kernel(scal_ref, q_ref, k_ref, v_ref, gate_ref, bias_ref, gain_ref, o_ref,
                      kt_ref, vx_ref, s0_ref, s1_ref, p0_ref, p1_ref, al0_ref, al1_ref,
                      acc_ref, m_ref):
    h = pl.program_id(1)
    qi = pl.program_id(2)
    nkb = kt_ref.shape[0]
    nrh = B_TQ // B_BT
    nsub = B_TK // B_BT
    ncol = B_TK // HEAD_W

    @pl.when(qi == 0)
    def _():
        def tr(kb, c):
            rows = pl.ds(pl.multiple_of(kb * B_TK, B_TK), B_TK)
            kt_ref[kb] = k_ref[0, rows, :].astype(F32).T.astype(BF16)
            vx_ref[rows, 0:HEAD_W] = v_ref[0, rows, :]
            vx_ref[rows, HEAD_W:2 * HEAD_W] = jnp.ones((B_TK, HEAD_W), BF16)
            return c
        lax.fori_loop(0, nkb, tr, 0)

    lane = lax.broadcasted_iota(jnp.int32, (B_TQ, HEAD_W), 1)
    lo = lane < DIFF_QK
    q = q_ref[0] * (DIFF_QK ** -0.5)
    zero = jnp.zeros_like(q)
    qz = (jnp.where(lo, q, zero), jnp.where(lo, zero, q))
    cfar = scal_ref[2 + h]
    nk = ((qi + 1) * nrh - 1) // nsub + 1
    n_far = jnp.maximum((nrh * qi - B_NEAR + 1) // nsub, 0)
    per_sub = B_BT // HEAD_W

    s_bufs, p_bufs, al_bufs = (s0_ref, s1_ref), (p0_ref, p1_ref), (al0_ref, al1_ref)
    acc_ref[...] = jnp.zeros_like(acc_ref)
    m_ref[...] = jnp.full_like(m_ref, NEG)
    al1_ref[...] = jnp.ones_like(al1_ref)
    p1_ref[...] = jnp.zeros_like(p1_ref)

    row_halves = [slice(rh * B_BT, (rh + 1) * B_BT) for rh in range(nrh)]

    def qk(kb, par):
        kt = kt_ref[jnp.minimum(kb, nk - 1)]
        for mi in range(2):
            for rows in row_halves:
                s_bufs[par][mi, rows, :] = _nn(qz[mi][rows], kt)

    def pv(kb, par):
        kb = jnp.clip(kb, 0, nk - 1)
        vx = vx_ref[pl.ds(pl.multiple_of(kb * B_TK, B_TK), B_TK), :]
        for mi in range(2):
            for rows in row_halves:
                a = al_bufs[par][mi, rows]
                upd = _nn(p_bufs[par][mi, rows, :], vx)
                acc_ref[mi, rows] = jnp.concatenate([a, a], axis=1) * acc_ref[mi, rows] + upd

    def col(par, mi, c, kb, near):
        x = s_bufs[par][mi, :, c * HEAD_W:(c + 1) * HEAD_W]
        if near:
            cc = (c % per_sub) * HEAD_W
            tiles = [jnp.clip(nrh * qi + rh - nsub * kb - c // per_sub, -1, B_NEAR) + 1
                     for rh in range(nrh)]
            x = x + jnp.concatenate([bias_ref[0, tl, :, cc:cc + HEAD_W] for tl in tiles], axis=0)
        return x

    def softmax(kb, par, near):
        shift = 0.0 if near else cfar
        m_sub = []
        for mi in range(2):
            mx = col(par, mi, 0, kb, near)
            for c in range(1, ncol):
                mx = jnp.maximum(mx, col(par, mi, c, kb, near))
            m_prev = m_ref[mi]
            m_next = jnp.maximum(m_prev, jnp.max(mx, axis=-1, keepdims=True) + shift)
            al_bufs[par][mi] = jnp.exp(m_prev - m_next)
            m_ref[mi] = m_next
            m_sub.append(m_next - shift)
        for mi in range(2):
            for c in range(ncol):
                e = jnp.exp(col(par, mi, c, kb, near) - m_sub[mi])
                p_bufs[par][mi, :, c * HEAD_W:(c + 1) * HEAD_W] = e.astype(BF16)

    def pair(u, near):
        for par in range(2):
            t = 2 * u + par
            qk(t + 1, 1 - par)
            pv(t - 1, 1 - par)
            softmax(t, par, near)

    def far_body(u, c):
        pair(u, False)
        return c

    def near_body(u, c):
        pair(u, True)
        return c

    far_pairs = n_far // 2
    full_pairs = nk // 2
    qk(0, 0)
    lax.fori_loop(0, far_pairs, far_body, 0)
    lax.fori_loop(far_pairs, full_pairs, near_body, 0)

    @pl.when(nk % 2 == 1)
    def _():
        pv(nk - 2, 1)
        softmax(nk - 1, 0, True)
        pv(nk - 1, 0)

    @pl.when(nk % 2 == 0)
    def _():
        pv(nk - 1, 1)

    lam = scal_ref[0]
    a1, a2 = acc_ref[0], acc_ref[1]
    o = a1[:, :HEAD_W] / a1[:, HEAD_W:] - lam * (a2[:, :HEAD_W] / a2[:, HEAD_W:])
    ms = jnp.mean(o * o, axis=-1, keepdims=True)
    o = o * lax.rsqrt(ms + RMS_EPS) * gain_ref[...] * scal_ref[1]
    o_ref[0] = (o * _silu(gate_ref[0])).astype(o_ref.dtype)


def _diff_attn(pb, pf, bias_tab, scal, gain, bsz, t):
    nh = BRANCH_W // HEAD_W
    nq = t // B_TQ
    return pl.pallas_call(
        _diff_attn_kernel,
        grid=(bsz, nh, nq),
        in_specs=[pl.BlockSpec(memory_space=pltpu.SMEM),
                  pl.BlockSpec((1, B_TQ, HEAD_W), lambda b, h, i: (b, i, h)),
                  pl.BlockSpec((1, t, HEAD_W), lambda b, h, i: (b, 0, nh + h)),
                  pl.BlockSpec((1, t, HEAD_W), lambda b, h, i: (b, 0, 2 * nh + h)),
                  pl.BlockSpec((1, B_TQ, HEAD_W), lambda b, h, i: (b, i, PF_BGATE * nh + h)),
                  pl.BlockSpec((1, B_NEAR + 2, B_BT, B_BT), lambda b, h, i: (h, 0, 0, 0)),
                  pl.BlockSpec((1, HEAD_W), lambda b, h, i: (0, 0))],
        out_specs=pl.BlockSpec((1, B_TQ, HEAD_W), lambda b, h, i: (b, i, h)),
        out_shape=jax.ShapeDtypeStruct((bsz, t, BRANCH_W), BF16),
        scratch_shapes=[pltpu.VMEM((t // B_TK, HEAD_W, B_TK), BF16),
                        pltpu.VMEM((t, 2 * HEAD_W), BF16),
                        pltpu.VMEM((2, B_TQ, B_TK), F32),
                        pltpu.VMEM((2, B_TQ, B_TK), F32),
                        pltpu.VMEM((2, B_TQ, B_TK), BF16),
                        pltpu.VMEM((2, B_TQ, B_TK), BF16),
                        pltpu.VMEM((2, B_TQ, HEAD_W), F32),
                        pltpu.VMEM((2, B_TQ, HEAD_W), F32),
                        pltpu.VMEM((2, B_TQ, 2 * HEAD_W), F32),
                        pltpu.VMEM((2, B_TQ, HEAD_W), F32)],
        compiler_params=_cparams(("arbitrary", "arbitrary", "arbitrary")),
        name="diff_attn",
    )(scal, pb, pb, pb, pf, bias_tab, gain.reshape(1, HEAD_W).astype(F32))


def _delta_kernel(qkv_ref, z_ref, small_ref, conv_ref, par_ref, gain_ref, o_ref,
                  xe_ref, s_ref):
    tc = qkv_ref.shape[1]
    nchunk = tc // CHUNK

    @pl.when(pl.program_id(1) == 0)
    def _():
        xe_ref[0:8, :] = jnp.zeros((8, xe_ref.shape[1]), F32)
        s_ref[...] = jnp.zeros_like(s_ref)

    xe_ref[8:8 + tc, :] = qkv_ref[0]
    conv = conv_ref[0:1, :] * xe_ref[pl.ds(8 - 3, tc), :]
    for kk in range(1, CONV_K):
        conv = conv + conv_ref[kk:kk + 1, :] * xe_ref[pl.ds(8 - 3 + kk, tc), :]
    xe_ref[0:8, :] = xe_ref[tc:tc + 8, :]
    c = _silu(conv)

    small = small_ref[0]
    beta_all = _sigmoid(small)
    xa = small + par_ref[1:2, :]
    softplus = jnp.maximum(xa, 0.0) + _log1p_exp_neg_abs(xa)
    g_all = -jnp.exp(par_ref[0:1, :]) * softplus

    ri = lax.broadcasted_iota(jnp.int32, (CHUNK, CHUNK), 0)
    ci = lax.broadcasted_iota(jnp.int32, (CHUNK, CHUNK), 1)
    tri = ri >= ci
    strict = ri > ci
    eye = (ri == ci).astype(F32)
    blocks = [ri // size == ci // size for size in (8, 16, 32, CHUNK)]
    rt = lax.broadcasted_iota(jnp.int32, (tc, tc), 0)
    ct = lax.broadcasted_iota(jnp.int32, (tc, tc), 1)
    tri_blocks = jnp.logical_and(rt >= ct, rt // CHUNK == ct // CHUNK).astype(F32)
    sel = (lax.broadcasted_iota(jnp.int32, (8, HEAD_W), 1)
           == lax.broadcasted_iota(jnp.int32, (8, HEAD_W), 0) + 4).astype(F32)
    nh = BRANCH_W // HEAD_W
    units = [(ch, h) for ch in range(nchunk) for h in range(nh)]

    gcum_all = _nn(tri_blocks, g_all, HI)
    grow_all = _nt(sel, gcum_all, HI)

    qs, ks, vs, bcols, gcols, glasts, gammas, ms_ = {}, {}, {}, {}, {}, {}, {}, {}
    for h in range(nh):
        qf = c[:, h * HEAD_W:(h + 1) * HEAD_W]
        kf = c[:, BRANCH_W + h * HEAD_W:BRANCH_W + (h + 1) * HEAD_W]
        qf = qf * lax.rsqrt(jnp.sum(qf * qf, axis=-1, keepdims=True) + 1e-6) * (HEAD_W ** -0.5)
        kf = kf * lax.rsqrt(jnp.sum(kf * kf, axis=-1, keepdims=True) + 1e-6)
        for ch in range(nchunk):
            r0 = ch * CHUNK
            u_ = (ch, h)
            qs[u_] = qf[r0:r0 + CHUNK]
            ks[u_] = kf[r0:r0 + CHUNK]
            vs[u_] = c[r0:r0 + CHUNK, 2 * BRANCH_W + h * HEAD_W:2 * BRANCH_W + (h + 1) * HEAD_W]
            bcols[u_] = beta_all[r0:r0 + CHUNK, h:h + 1]
            gcols[u_] = gcum_all[r0:r0 + CHUNK, 4 + h:5 + h]
            glasts[u_] = gcum_all[r0 + CHUNK - 1:r0 + CHUNK, 4 + h:5 + h]
            grow = grow_all[h:h + 1, r0:r0 + CHUNK]
            gammas[u_] = jnp.exp(jnp.where(tri, gcols[u_] - grow, NEG))
    for u_ in units:
        kkt = _nt_acc(ks[u_], ks[u_])
        ms_[u_] = jnp.where(strict, bcols[u_] * kkt * gammas[u_], 0.0)

    def prod(a, b):
        return {u_: _nn(_split_cols(a[u_]), _split_rows(b[u_])) for u_ in units}

    pw = {u_: jnp.where(blocks[0], -ms_[u_], 0.0) for u_ in units}
    tinv = {u_: eye + pw[u_] for u_ in units}
    for _ in range(2):
        pw = prod(pw, pw)
        step = prod(tinv, pw)
        tinv = {u_: tinv[u_] + step[u_] for u_ in units}
    for inner, outer in zip(blocks[:-1], blocks[1:]):
        coupling = {u_: jnp.where(jnp.logical_and(outer, jnp.logical_not(inner)), ms_[u_], 0.0)
                    for u_ in units}
        step = prod(tinv, prod(coupling, tinv))
        tinv = {u_: tinv[u_] - step[u_] for u_ in units}

    us, ws, aqks, qds, kds = {}, {}, {}, {}, {}
    for u_ in units:
        egc = jnp.exp(gcols[u_])
        rhs = jnp.concatenate([vs[u_] * bcols[u_], ks[u_] * (bcols[u_] * egc)], axis=1)
        uw = _nn(_split_cols(tinv[u_]), _split_rows(rhs))
        us[u_], ws[u_] = uw[:, :HEAD_W], uw[:, HEAD_W:]
        aqks[u_] = _nt(qs[u_], ks[u_]) * gammas[u_]
        qds[u_] = qs[u_] * egc
        kds[u_] = ks[u_] * jnp.exp(glasts[u_] - gcols[u_])

    states = [s_ref[h] for h in range(nh)]
    for ch in range(nchunk):
        r0 = ch * CHUNK
        v_news, o_inters = {}, {}
        for h in range(nh):
            u_ = (ch, h)
            s = states[h]
            s_hi, s_lo = _hi_lo(s)
            w_hi, w_lo = _hi_lo(ws[u_])
            ws_prod = (_nn(jnp.concatenate([w_hi, w_lo], axis=1), jnp.concatenate([s_hi, s_hi], axis=0))
                       + _nn(w_hi, s_lo))
            v_news[h] = us[u_] - ws_prod
            o_inters[h] = _nn(qds[u_], s)
        for h in range(nh):
            u_ = (ch, h)
            v_new = v_news[h]
            o = o_inters[h] + _nn(aqks[u_], v_new)
            states[h] = (jnp.exp(glasts[u_]) * states[h]
                         + _tn(_split_rows_lhs(kds[u_]), _split_rows(v_new)))
            if ch == nchunk - 1:
                s_ref[h] = states[h]
            ms = jnp.mean(o * o, axis=-1, keepdims=True)
            o = o * lax.rsqrt(ms + RMS_EPS) * gain_ref[...]
            zg = z_ref[0, r0:r0 + CHUNK, h * HEAD_W:(h + 1) * HEAD_W]
            o_ref[0, r0:r0 + CHUNK, h * HEAD_W:(h + 1) * HEAD_W] = (o * _silu(zg)).astype(o_ref.dtype)


def _delta_net(pf, small, conv_w, par, gain, bsz, t, tc=CHUNK):
    cw = 3 * BRANCH_W
    return pl.pallas_call(
        _delta_kernel,
        grid=(bsz, t // tc),
        in_specs=[pl.BlockSpec((1, tc, cw), lambda b, i: (b, i, 0)),
                  pl.BlockSpec((1, tc, BRANCH_W), lambda b, i: (b, i, PF_CZ)),
                  pl.BlockSpec((1, tc, HEAD_W), lambda b, i: (b, i, 0)),
                  pl.BlockSpec((8, cw), lambda b, i: (0, 0)),
                  pl.BlockSpec((8, HEAD_W), lambda b, i: (0, 0)),
                  pl.BlockSpec((1, HEAD_W), lambda b, i: (0, 0))],
        out_specs=pl.BlockSpec((1, tc, BRANCH_W), lambda b, i: (b, i, 0)),
        out_shape=jax.ShapeDtypeStruct((bsz, t, BRANCH_W), BF16),
        scratch_shapes=[pltpu.VMEM((tc + 8, cw), F32),
                        pltpu.VMEM((BRANCH_W // HEAD_W, HEAD_W, HEAD_W), F32)],
        compiler_params=_cparams(("arbitrary", "arbitrary")),
        name="delta_net",
    )(pf, pf, small, conv_w, par, gain.reshape(1, HEAD_W).astype(F32))


def _hgrn_kernel(q_ref, f_ref, i_ref, gate_ref, lb_ref, gain_ref, o_ref, st_ref):
    tc = q_ref.shape[1]
    nchunk = tc // CHUNK
    nh = BRANCH_W // HEAD_W
    nsub = CHUNK // SUB

    @pl.when(pl.program_id(1) == 0)
    def _():
        st_ref[...] = jnp.zeros_like(st_ref)

    lb = lb_ref[...]
    df = f_ref[0]
    log_sig = jnp.minimum(df, 0.0) - _log1p_exp_neg_abs(df)
    a = jnp.log(lb)
    b = jnp.log1p(-lb) + log_sig
    logf_all = jnp.maximum(a, b) + _log1p_exp_neg_abs(a - b)
    k_all = (1.0 - lb) * _sigmoid(-df)

    ri = lax.broadcasted_iota(jnp.int32, (CHUNK, 3 * CHUNK), 0)
    ci = lax.broadcasted_iota(jnp.int32, (CHUNK, 3 * CHUNK), 1)
    tri3 = jnp.where(ri >= ci % CHUNK, 1.0, 0.0).astype(BF16)
    sub_row = lax.broadcasted_iota(jnp.int32, (HALF, HEAD_W), 0)
    units = [(ch, h) for ch in range(nchunk) for h in range(nh)]

    bcs = []
    for ch in range(nchunk):
        lf = logf_all[ch * CHUNK:(ch + 1) * CHUNK]
        hi = lf.astype(BF16)
        r1 = lf - hi.astype(F32)
        mid = r1.astype(BF16)
        lo = (r1 - mid.astype(F32)).astype(BF16)
        bcs.append(_nn(tri3, jnp.concatenate([hi, mid, lo], axis=0)))

    qs, ks, vs, bcu, bls, qes, dstate, intra = {}, {}, {}, {}, {}, {}, {}, {}
    for u_ in units:
        ch, h = u_
        r0 = ch * CHUNK
        cs = slice(h * HEAD_W, (h + 1) * HEAD_W)
        qs[u_] = q_ref[0, r0:r0 + CHUNK, cs]
        vs[u_] = i_ref[0, r0:r0 + CHUNK, cs]
        ks[u_] = k_all[r0:r0 + CHUNK, cs]
        bcu[u_] = bcs[ch][:, cs]
        bls[u_] = bcu[u_][CHUNK - 1:CHUNK, :]
        qes[u_] = (qs[u_] * jnp.exp(bcu[u_])).astype(BF16)
        k_end = ks[u_] * jnp.exp(bls[u_] - bcu[u_])
        dstate[u_] = _tn(_split_rows_lhs(vs[u_]), _split_rows(k_end))

    row_c = lax.broadcasted_iota(jnp.int32, (CHUNK, HEAD_W), 0)
    second_half = (row_c // HALF) % 2 == 1
    rr = lax.broadcasted_iota(jnp.int32, (CHUNK, CHUNK), 0)
    cc = lax.broadcasted_iota(jnp.int32, (CHUNK, CHUNK), 1)
    same_block = rr // SUB == cc // SUB
    logits, v_bfs = {}, {}
    for u_ in units:
        q, k, bc = qs[u_], ks[u_], bcu[u_]
        v_bfs[u_] = vs[u_].astype(BF16)
        for si in range(1, nsub):
            rs = slice(si * SUB, (si + 1) * SUB)
            ref_row = bc[si * SUB - 1:si * SUB, :]
            q_dec = (q[rs] * jnp.exp(bc[rs] - ref_row)).astype(BF16)
            k_dec = (k[:si * SUB] * jnp.exp(ref_row - bc[:si * SUB])).astype(BF16)
            logits[u_, si] = _nt(q_dec, k_dec)
        mid_rows = jnp.concatenate(
            [jnp.broadcast_to(bc[si * SUB + HALF - 1:si * SUB + HALF, :], (SUB, HEAD_W))
             for si in range(nsub)], axis=0)
        e_mid = jnp.exp(-jnp.abs(bc - mid_rows))
        q_mid = jnp.where(second_half, q * e_mid, 0.0).astype(BF16)
        k_mid = jnp.where(second_half, 0.0, k * e_mid).astype(BF16)
        logits[u_, "mid"] = jnp.where(same_block, _nt(q_mid, k_mid), 0.0)
    for u_ in units:
        pieces = [jnp.zeros((SUB, HEAD_W), F32)]
        for si in range(1, nsub):
            pieces.append(_nn(logits[u_, si].astype(BF16), v_bfs[u_][:si * SUB]))
        o_b = _nn(logits[u_, "mid"].astype(BF16), v_bfs[u_])
        intra[u_] = jnp.concatenate(pieces, axis=0) + o_b

    diag = {u_: [] for u_ in units}
    for sb in range(CHUNK // HALF):
        rs = slice(sb * HALF, (sb + 1) * HALF)
        o_s = {u_: jnp.zeros((HALF, HEAD_W), F32) for u_ in units}
        for j in range(HALF):
            jj = sb * HALF + j
            for u_ in units:
                q, k, v, bc = qs[u_], ks[u_], vs[u_], bcu[u_]
                e = jnp.exp(jnp.where(sub_row >= j, bc[rs] - bc[jj:jj + 1, :], NEG))
                a_col = jnp.sum(q[rs] * k[jj:jj + 1, :] * e, axis=-1, keepdims=True)
                o_s[u_] = o_s[u_] + a_col * v[jj:jj + 1, :]
        for u_ in units:
            diag[u_].append(o_s[u_])
    for u_ in units:
        intra[u_] = intra[u_] + jnp.concatenate(diag[u_], axis=0)

    states = [st_ref[h] for h in range(nh)]
    for u_ in units:
        ch, h = u_
        r0 = ch * CHUNK
        cs = slice(h * HEAD_W, (h + 1) * HEAD_W)
        o = _nt(qes[u_], states[h].astype(BF16)) + intra[u_]
        states[h] = states[h] * jnp.exp(bls[u_]) + dstate[u_]
        if ch == nchunk - 1:
            st_ref[h] = states[h]
        ms = jnp.mean(o * o, axis=-1, keepdims=True)
        o = o * lax.rsqrt(ms + RMS_EPS) * gain_ref[...]
        o_ref[0, r0:r0 + CHUNK, cs] = (o * _silu(gate_ref[0, r0:r0 + CHUNK, cs])).astype(o_ref.dtype)


def _hgrn(pf, lb, gain, bsz, t, tc=CHUNK):
    def spec(cb):
        return pl.BlockSpec((1, tc, BRANCH_W), lambda b, i: (b, i, cb))

    return pl.pallas_call(
        _hgrn_kernel,
        grid=(bsz, t // tc),
        in_specs=[spec(PF_DQ), spec(PF_DF), spec(PF_DI), spec(PF_DGATE),
                  pl.BlockSpec((1, BRANCH_W), lambda b, i: (0, 0)),
                  pl.BlockSpec((1, HEAD_W), lambda b, i: (0, 0))],
        out_specs=pl.BlockSpec((1, tc, BRANCH_W), lambda b, i: (b, i, 0)),
        out_shape=jax.ShapeDtypeStruct((bsz, t, BRANCH_W), BF16),
        scratch_shapes=[pltpu.VMEM((BRANCH_W // HEAD_W, HEAD_W, HEAD_W), F32)],
        compiler_params=_cparams(("arbitrary", "arbitrary")),
        name="hgrn2",
    )(pf, pf, pf, pf, lb.reshape(1, BRANCH_W), gain.reshape(1, HEAD_W).astype(F32))


def _split_w_in(w):
    o = 0
    parts = {}
    for name, width in (("a_q", 512), ("a_k", 512), ("a_v", 512), ("a_gate", 512),
                        ("b_q", 512), ("b_k", 512), ("b_v", 512), ("b_gate", 512),
                        ("c_qkv", 1536), ("c_z", 512), ("c_beta", 4), ("c_a", 4),
                        ("d_q", 512), ("d_f", 512), ("d_i", 512), ("d_gate", 512),
                        ("merge", 4 * D_MODEL)):
        parts[name] = w[:, o:o + width]
        o += width
    w_f = jnp.concatenate([parts[k] for k in ("c_qkv", "c_z", "a_gate", "b_gate", "d_q", "d_f",
                                              "d_i", "d_gate", "a_q", "a_k", "a_v")], axis=1)
    w_b = jnp.concatenate([parts[k] for k in ("b_q", "b_k", "b_v")], axis=1)
    w_s = jnp.concatenate([parts["c_beta"], parts["c_a"],
                           jnp.zeros((w.shape[0], HEAD_W - 8), w.dtype)], axis=1)
    return w_f.astype(BF16), w_b.astype(BF16), w_s.astype(BF16), parts["merge"].astype(BF16)


def kernel(x, norm_gain, w_in, rel_bias, diff_lambda, diff_subln_gain, dn_conv, dn_a_log, dn_dt_bias,
           dn_norm_gain, hg_lb_logits, hg_norm_gain, w_branch, w_out, final_gain):
    bsz, t, d = x.shape
    n = bsz * t
    depth = w_in.shape[0]
    lb_p = jax.nn.softmax(hg_lb_logits.astype(F32), axis=0)
    hg_lb = jnp.clip(jnp.cumsum(lb_p, axis=0) - lb_p[0], 0.0, 1.0)
    bias_a = _dilated_bias(rel_bias[:, :8].astype(F32))
    bias_b = _diff_bias(rel_bias[:, 8:].astype(F32))
    cfar = rel_bias[NUM_BUCKETS - 1, 8:].astype(F32)

    xf = x.reshape(n, d).astype(F32)
    h = _rmsnorm(xf, norm_gain[0], BF16)
    for layer in range(depth):
        w_f, w_b, w_s, w_g = _split_w_in(w_in[layer])
        pf = _mm(h, w_f, F32, name="proj_f32").reshape(bsz, t, PF_COLS)
        pb = _mm(h, w_b, BF16, name="proj_bf16").reshape(bsz, t, 3 * BRANCH_W)
        ps = _mm(h, w_s, F32, name="proj_small").reshape(bsz, t, HEAD_W)
        gates = _mm(h, w_g, BF16, act="sigmoid", name="proj_gates")

        y_a = _dil_attn(pf, bias_a, bsz, t)

        lam_init = 0.8 - 0.6 * math.exp(-0.3 * layer)
        lq1, lk1, lq2, lk2 = diff_lambda[layer].astype(F32)
        lam = jnp.exp(jnp.sum(lq1 * lk1)) - jnp.exp(jnp.sum(lq2 * lk2)) + lam_init
        scal = jnp.concatenate([jnp.stack([lam, jnp.asarray(1.0 - lam_init, F32)]), cfar,
                                jnp.zeros((2,), F32)])
        y_b = _diff_attn(pb, pf, bias_b, scal, diff_subln_gain[layer], bsz, t)

        conv_w = jnp.concatenate([dn_conv[layer].astype(F32),
                                  jnp.zeros((8 - CONV_K, 3 * BRANCH_W), F32)], axis=0)
        par = jnp.zeros((8, HEAD_W), F32)
        par = par.at[0, 4:8].set(dn_a_log[layer].astype(F32)).at[1, 4:8].set(dn_dt_bias[layer].astype(F32))
        y_c = _delta_net(pf, ps, conv_w, par, dn_norm_gain[layer], bsz, t, tc=DELTA_TILE)

        y_d = _hgrn(pf, hg_lb[layer], hg_norm_gain[layer], bsz, t, tc=HGRN_TILE)

        ys = [y.reshape(n, BRANCH_W) for y in (y_a, y_b, y_c, y_d)]
        last = layer == depth - 1
        res = _merge_out(gates, ys, w_branch[layer].astype(BF16), w_out[layer].astype(BF16), xf,
                         final_gain if last else norm_gain[layer + 1], last)
        if last:
            out = res
        else:
            xf, h = res
    return out.reshape(bsz, t, d).astype(x.dtype)
```

```python
import functools
import math

import jax
import jax.numpy as jnp
from jax import lax
from jax.experimental import pallas as pl
from jax.experimental.pallas import tpu as pltpu

F32 = jnp.float32
BF16 = jnp.bfloat16
HI = lax.Precision.HIGHEST

SUBLANES = 8
D_MODEL = 2048
BRANCH_W = 512
HEAD_W = 128
C_HEADS = BRANCH_W // HEAD_W
L2_EPS = 1e-6
A_HEAD_DIM = 64
DILATIONS = (1, 4, 16)
BAND = 128
A_TILE = BAND * DILATIONS[-1]
A_GROUP = 4
DIFF_QK = 64
B_BT = 256
B_TQ = 512
B_TK = 512
B_NEAR = 2048 // B_BT + 1
CHUNK = 64
DELTA_TILE = 256
HGRN_TILE = 256
SUB = 16
HALF = SUB // 2
CONV_K = 4
NUM_BUCKETS = 32
MAX_DISTANCE = 2048
RMS_EPS = 1e-6
NEG = -1e30
VMEM_LIMIT = 56 * 1024 * 1024
MM_TM, MM_TN = 2048, 1024
MERGE_TM = 256

PF_CQKV, PF_CZ, PF_AGATE, PF_BGATE, PF_DQ, PF_DF, PF_DI, PF_DGATE, PF_AQ, PF_AK, PF_AV = 0, 3, 4, 5, 6, 7, 8, 9, 10, 11, 12
PF_COLS = 13 * 512


def _cparams(sem):
    return pltpu.CompilerParams(dimension_semantics=sem, vmem_limit_bytes=VMEM_LIMIT)


def _nt(a, b, precision=None):
    return lax.dot_general(a, b, (((1,), (1,)), ((), ())), precision=precision,
                           preferred_element_type=F32)


def _tn(a, b, precision=None):
    return lax.dot_general(a, b, (((0,), (0,)), ((), ())), precision=precision,
                           preferred_element_type=F32)


def _nn(a, b, precision=None):
    return jnp.dot(a, b, precision=precision, preferred_element_type=F32)


def _sigmoid(x):
    return 0.5 * jnp.tanh(0.5 * x) + 0.5


def _silu(x):
    return x * _sigmoid(x)


def _log1p_exp_neg_abs(x):
    return jnp.log1p(jnp.exp(-jnp.abs(x)))


def _hi_lo(x):
    bits = lax.bitcast_convert_type(x, jnp.int32)
    hi = lax.bitcast_convert_type(bits & jnp.int32(-65536), F32)
    return hi, x - hi


def _split_cols(a):
    hl = jnp.concatenate(_hi_lo(a), axis=1)
    return jnp.concatenate([hl, hl], axis=1)


def _split_rows(b):
    hi, lo = _hi_lo(b)
    return jnp.concatenate([hi, hi, lo, lo], axis=0)


def _split_rows_lhs(a):
    hi, lo = _hi_lo(a)
    return jnp.concatenate([hi, lo, hi, lo], axis=0)


def _nt_acc(a, b):
    ah, al = _hi_lo(a)
    bh, bl = _hi_lo(b)
    return (_nt(jnp.concatenate([ah, al], axis=1), jnp.concatenate([bh, bh], axis=1))
            + _nt(ah, bl))


def _rmsnorm_kernel(x_ref, g_ref, o_ref):
    x = x_ref[...]
    ms = jnp.mean(x * x, axis=-1, keepdims=True)
    o_ref[...] = (x * lax.rsqrt(ms + RMS_EPS) * g_ref[...]).astype(o_ref.dtype)


def _rmsnorm(x, gain, out_dtype, tm=512):
    n, d = x.shape
    return pl.pallas_call(
        _rmsnorm_kernel,
        grid=(n // tm,),
        in_specs=[pl.BlockSpec((tm, d), lambda i: (i, 0)),
                  pl.BlockSpec((1, d), lambda i: (0, 0))],
        out_specs=pl.BlockSpec((tm, d), lambda i: (i, 0)),
        out_shape=jax.ShapeDtypeStruct((n, d), out_dtype),
        compiler_params=_cparams(("parallel",)),
        name="rmsnorm",
    )(x, gain.reshape(1, d).astype(F32))


def _mm_kernel(a_ref, w_ref, o_ref, *, act):
    acc = _nn(a_ref[...], w_ref[...])
    if act == "sigmoid":
        acc = _sigmoid(acc)
    o_ref[...] = acc.astype(o_ref.dtype)


def _mm(a, w, out_dtype, act=None, tm=MM_TM, tn=MM_TN, name="mm"):
    n, k = a.shape
    c = w.shape[1]
    while c % tn:
        tn //= 2
    return pl.pallas_call(
        functools.partial(_mm_kernel, act=act),
        grid=(n // tm, c // tn),
        in_specs=[pl.BlockSpec((tm, k), lambda i, j: (i, 0)),
                  pl.BlockSpec((k, tn), lambda i, j: (0, j))],
        out_specs=pl.BlockSpec((tm, tn), lambda i, j: (i, j)),
        out_shape=jax.ShapeDtypeStruct((n, c), out_dtype),
        compiler_params=_cparams(("parallel", "parallel")),
        name=name,
    )(a, w)


def _merge_out_kernel(g0, g1, g2, g3, y0, y1, y2, y3, wbr_ref, wout_ref, x_ref, gain_ref, *out_refs,
                      last):
    mixed = g0[...].astype(F32) * _nn(y0[...], wbr_ref[0])
    mixed += g1[...].astype(F32) * _nn(y1[...], wbr_ref[1])
    mixed += g2[...].astype(F32) * _nn(y2[...], wbr_ref[2])
    mixed += g3[...].astype(F32) * _nn(y3[...], wbr_ref[3])
    x = x_ref[...] + _nn(mixed.astype(BF16), wout_ref[...])
    ms = jnp.mean(x * x, axis=-1, keepdims=True)
    normed = x * lax.rsqrt(ms + RMS_EPS) * gain_ref[...]
    if last:
        out_refs[0][...] = normed
    else:
        out_refs[0][...] = x
        out_refs[1][...] = normed.astype(BF16)


def _merge_out(gates, ys, w_br, w_out, x, next_gain, last, tm=MERGE_TM):
    n = gates.shape[0]
    gate_specs = [pl.BlockSpec((tm, D_MODEL), functools.partial(lambda i, b: (i, b), b=b))
                  for b in range(4)]
    y_specs = [pl.BlockSpec((tm, BRANCH_W), lambda i: (i, 0)) for _ in range(4)]
    resident = dict(pipeline_mode=pl.Buffered(1))
    row_spec = pl.BlockSpec((tm, D_MODEL), lambda i: (i, 0))
    if last:
        out_specs, out_shape = row_spec, jax.ShapeDtypeStruct((n, D_MODEL), F32)
    else:
        out_specs = (row_spec, row_spec)
        out_shape = (jax.ShapeDtypeStruct((n, D_MODEL), F32), jax.ShapeDtypeStruct((n, D_MODEL), BF16))
    return pl.pallas_call(
        functools.partial(_merge_out_kernel, last=last),
        grid=(n // tm,),
        in_specs=gate_specs + y_specs + [
            pl.BlockSpec((4, BRANCH_W, D_MODEL), lambda i: (0, 0, 0), **resident),
            pl.BlockSpec((D_MODEL, D_MODEL), lambda i: (0, 0), **resident),
            row_spec,
            pl.BlockSpec((1, D_MODEL), lambda i: (0, 0))],
        out_specs=out_specs,
        out_shape=out_shape,
        compiler_params=_cparams(("parallel",)),
        name="merge_out",
    )(gates, gates, gates, gates, *ys, w_br, w_out, x, next_gain.reshape(1, D_MODEL).astype(F32))


def _t5_bucket(dist):
    n = jnp.maximum(dist, 0)
    max_exact = NUM_BUCKETS // 2
    nf = jnp.maximum(n, max_exact).astype(F32)
    large = max_exact + (jnp.log(nf / max_exact) / math.log(MAX_DISTANCE / max_exact)
                         * (NUM_BUCKETS - max_exact)).astype(jnp.int32)
    large = jnp.minimum(large, NUM_BUCKETS - 1)
    return jnp.where(n < max_exact, n, large)


def _bucket_lookup(table, bucket):
    tab = table.T.reshape((table.shape[1],) + (1,) * bucket.ndim + (NUM_BUCKETS,))
    out = jnp.zeros((table.shape[1],) + bucket.shape, F32)
    for b in range(NUM_BUCKETS):
        out = jnp.where(bucket[None] == b, tab[..., b], out)
    return out


def _dilated_bias(bias_a):
    qi = jnp.arange(BAND)[:, None]
    kj = jnp.arange(2 * BAND)[None, :]
    rel = qi + BAND - kj
    valid = (rel >= 0) & (rel <= BAND)
    tabs = []
    for dil in DILATIONS:
        b = _bucket_lookup(bias_a, _t5_bucket(rel * dil))
        tabs.append(jnp.where(valid[None], b, NEG))
    return jnp.stack(tabs)


def _diff_bias(bias_b):
    qi = jnp.arange(B_BT)[:, None]
    kj = jnp.arange(B_BT)[None, :]
    tabs = []
    for d in range(-1, B_NEAR + 1):
        rel = d * B_BT + qi - kj
        b = _bucket_lookup(bias_b, _t5_bucket(rel))
        tabs.append(jnp.where((rel >= 0)[None], b, NEG))
    return jnp.stack(tabs, axis=1)


def _dil_attn_kernel(q_ref, kp_ref, kc_ref, vp_ref, vc_ref, gate_ref, bias_ref, o_ref,
                     kcat, vcat, acc_ref, m_ref, l_ref):
    n = pl.program_id(1)
    kcat[0:A_TILE, :] = kp_ref[0]
    kcat[A_TILE:2 * A_TILE, :] = kc_ref[0]
    vcat[0:A_TILE, :] = vp_ref[0]
    vcat[A_TILE:2 * A_TILE, :] = vc_ref[0]
    lane = lax.broadcasted_iota(jnp.int32, (BAND, HEAD_W), 1)
    lo = lane < A_HEAD_DIM
    col = lax.broadcasted_iota(jnp.int32, (BAND, 2 * BAND), 1)
    nblk_tile = A_TILE // BAND

    def rows(start, r):
        return pl.ds(start, BAND, stride=r) if r > 1 else pl.ds(start, BAND)

    trips = [(p, r, it) for p, r in reversed(list(enumerate(DILATIONS)))
             for it in range(nblk_tile // A_GROUP)]
    first_p = trips[0][0]

    def logits_stage(p, r, it):
        starts, first_blk, vvs, ss = [], [], [], {}
        for g in range(A_GROUP):
            idx = it * A_GROUP + g
            j = idx // r
            start = j * (BAND * r) + idx % r
            starts.append(start)
            first_blk.append(j == 0)
            q = (q_ref[0, rows(start, r), :] * (A_HEAD_DIM ** -0.5)).astype(BF16)
            kk = jnp.concatenate([kcat[rows(A_TILE + start - BAND * r, r), :],
                                  kcat[rows(A_TILE + start, r), :]], axis=0).astype(BF16)
            vvs.append(jnp.concatenate([vcat[rows(A_TILE + start - BAND * r, r), :],
                                        vcat[rows(A_TILE + start, r), :]], axis=0).astype(BF16))
            for hh in range(2):
                qz = jnp.where(lo if hh == 0 else jnp.logical_not(lo), q, jnp.zeros_like(q))
                ss[g, hh] = _nt(qz, kk)
        return starts, first_blk, vvs, ss

    def finish_stage(p, r, staged):
        starts, first_blk, vvs, ss = staged
        es, ms, ls, outs = {}, {}, {}, {}
        for g in range(A_GROUP):
            for hh in range(2):
                s = ss[g, hh] + bias_ref[p, hh]
                if first_blk[g]:
                    s = jnp.where(jnp.logical_and(n == 0, col < BAND), NEG, s)
                m = jnp.max(s, axis=-1, keepdims=True)
                e = jnp.exp(s - m)
                ms[g, hh] = jnp.broadcast_to(m, (BAND, HEAD_W))
                ls[g, hh] = jnp.broadcast_to(jnp.sum(e, axis=-1, keepdims=True), (BAND, HEAD_W))
                es[g, hh] = e.astype(BF16)
        for g in range(A_GROUP):
            for hh in range(2):
                outs[g, hh] = _nn(es[g, hh], vvs[g])
        merged = []
        for g in range(A_GROUP):
            o_new = jnp.where(lo, outs[g, 0], outs[g, 1])
            m_new = jnp.where(lo, ms[g, 0], ms[g, 1])
            l_new = jnp.where(lo, ls[g, 0], ls[g, 1])
            if p != first_p:
                rws = rows(starts[g], r)
                m_old = m_ref[rws, :]
                m_tot = jnp.maximum(m_old, m_new)
                a = jnp.exp(m_old - m_tot)
                b = jnp.exp(m_new - m_tot)
                o_new = a * acc_ref[rws, :] + b * o_new
                l_new = a * l_ref[rws, :] + b * l_new
                m_new = m_tot
            merged.append((o_new, m_new, l_new))
        for g in range(A_GROUP):
            rws = rows(starts[g], r)
            acc_ref[rws, :], m_ref[rws, :], l_ref[rws, :] = merged[g]

    staged = logits_stage(*trips[0])
    for i, (p, r, _) in enumerate(trips):
        nxt = logits_stage(*trips[i + 1]) if i + 1 < len(trips) else None
        finish_stage(p, r, staged)
        staged = nxt

    o_ref[0] = (acc_ref[...] / l_ref[...] * _silu(gate_ref[0])).astype(o_ref.dtype)


def _dil_attn(pf, bias_tab, bsz, t):
    nt = t // A_TILE
    hp = BRANCH_W // HEAD_W
    blk = (1, A_TILE, HEAD_W)

    def cur(cb):
        return pl.BlockSpec(blk, lambda b, n, h: (b, n, cb * hp + h))

    def prev(cb):
        return pl.BlockSpec(blk, lambda b, n, h: (b, jnp.maximum(n - 1, 0), cb * hp + h))

    return pl.pallas_call(
        _dil_attn_kernel,
        grid=(bsz, nt, hp),
        in_specs=[cur(PF_AQ), prev(PF_AK), cur(PF_AK), prev(PF_AV), cur(PF_AV), cur(PF_AGATE),
                  pl.BlockSpec((3, 2, BAND, 2 * BAND), lambda b, n, h: (0, h, 0, 0))],
        out_specs=pl.BlockSpec(blk, lambda b, n, h: (b, n, h)),
        out_shape=jax.ShapeDtypeStruct((bsz, t, BRANCH_W), BF16),
        scratch_shapes=[pltpu.VMEM((2 * A_TILE, HEAD_W), F32), pltpu.VMEM((2 * A_TILE, HEAD_W), F32),
                        pltpu.VMEM((A_TILE, HEAD_W), F32), pltpu.VMEM((A_TILE, HEAD_W), F32),
                        pltpu.VMEM((A_TILE, HEAD_W), F32)],
        compiler_params=_cparams(("parallel", "parallel", "parallel")),
        name="dilated_attn",
    )(pf, pf, pf, pf, pf, pf, bias_tab)


def _diff_attn_kernel(scal_ref, q_ref, k_ref, v_ref, gate_ref, bias_ref, gain_ref, o_ref,
                      kt_ref, vx_ref, s0_ref, s1_ref, p0_ref, p1_ref, al0_ref, al1_ref,
                      acc_ref, m_ref):
    h = pl.program_id(1)
    qi = pl.program_id(2)
    nkb = kt_ref.shape[0]
    nrh = B_TQ // B_BT
    nsub = B_TK // B_BT
    ncol = B_TK // HEAD_W

    @pl.when(qi == 0)
    def _():
        def tr(kb, c):
            rows = pl.ds(pl.multiple_of(kb * B_TK, B_TK), B_TK)
            kt_ref[kb] = k_ref[0, rows, :].astype(F32).T.astype(BF16)
            vx_ref[rows, 0:HEAD_W] = v_ref[0, rows, :]
            vx_ref[rows, HEAD_W:2 * HEAD_W] = jnp.ones((B_TK, HEAD_W), BF16)
            return c
        lax.fori_loop(0, nkb, tr, 0)

    lane = lax.broadcasted_iota(jnp.int32, (B_TQ, HEAD_W), 1)
    lo = lane < DIFF_QK
    q = q_ref[0] * (DIFF_QK ** -0.5)
    zero = jnp.zeros_like(q)
    qz = (jnp.where(lo, q, zero), jnp.where(lo, zero, q))
    cfar = scal_ref[2 + h]
    nk = ((qi + 1) * nrh - 1) // nsub + 1
    n_far = jnp.maximum((nrh * qi - B_NEAR + 1) // nsub, 0)
    per_sub = B_BT // HEAD_W

    s_bufs, p_bufs, al_bufs = (s0_ref, s1_ref), (p0_ref, p1_ref), (al0_ref, al1_ref)
    acc_ref[...] = jnp.zeros_like(acc_ref)
    m_ref[...] = jnp.full_like(m_ref, NEG)
    al1_ref[...] = jnp.ones_like(al1_ref)
    p1_ref[...] = jnp.zeros_like(p1_ref)

    row_halves = [slice(rh * B_BT, (rh + 1) * B_BT) for rh in range(nrh)]

    def qk(kb, par):
        kt = kt_ref[jnp.minimum(kb, nk - 1)]
        for mi in range(2):
            for rows in row_halves:
                s_bufs[par][mi, rows, :] = _nn(qz[mi][rows], kt)

    def pv(kb, par):
        kb = jnp.clip(kb, 0, nk - 1)
        vx = vx_ref[pl.ds(pl.multiple_of(kb * B_TK, B_TK), B_TK), :]
        for mi in range(2):
            for rows in row_halves:
                a = al_bufs[par][mi, rows]
                upd = _nn(p_bufs[par][mi, rows, :], vx)
                acc_ref[mi, rows] = jnp.concatenate([a, a], axis=1) * acc_ref[mi, rows] + upd

    def col(par, mi, c, kb, near):
        x = s_bufs[par][mi, :, c * HEAD_W:(c + 1) * HEAD_W]
        if near:
            cc = (c % per_sub) * HEAD_W
            tiles = [jnp.clip(nrh * qi + rh - nsub * kb - c // per_sub, -1, B_NEAR) + 1
                     for rh in range(nrh)]
            x = x + jnp.concatenate([bias_ref[0, tl, :, cc:cc + HEAD_W] for tl in tiles], axis=0)
        return x

    def softmax(kb, par, near):
        shift = 0.0 if near else cfar
        m_sub = []
        for mi in range(2):
            mx = col(par, mi, 0, kb, near)
            for c in range(1, ncol):
                mx = jnp.maximum(mx, col(par, mi, c, kb, near))
            m_prev = m_ref[mi]
            m_next = jnp.maximum(m_prev, jnp.max(mx, axis=-1, keepdims=True) + shift)
            al_bufs[par][mi] = jnp.exp(m_prev - m_next)
            m_ref[mi] = m_next
            m_sub.append(m_next - shift)
        for mi in range(2):
            for c in range(ncol):
                e = jnp.exp(col(par, mi, c, kb, near) - m_sub[mi])
                p_bufs[par][mi, :, c * HEAD_W:(c + 1) * HEAD_W] = e.astype(BF16)

    def pair(u, near):
        for par in range(2):
            t = 2 * u + par
            qk(t + 1, 1 - par)
            pv(t - 1, 1 - par)
            softmax(t, par, near)

    def far_body(u, c):
        pair(u, False)
        return c

    def near_body(u, c):
        pair(u, True)
        return c

    far_pairs = n_far // 2
    full_pairs = nk // 2
    qk(0, 0)
    lax.fori_loop(0, far_pairs, far_body, 0)
    lax.fori_loop(far_pairs, full_pairs, near_body, 0)

    @pl.when(nk % 2 == 1)
    def _():
        pv(nk - 2, 1)
        softmax(nk - 1, 0, True)
        pv(nk - 1, 0)

    @pl.when(nk % 2 == 0)
    def _():
        pv(nk - 1, 1)

    lam = scal_ref[0]
    a1, a2 = acc_ref[0], acc_ref[1]
    o = a1[:, :HEAD_W] / a1[:, HEAD_W:] - lam * (a2[:, :HEAD_W] / a2[:, HEAD_W:])
    ms = jnp.mean(o * o, axis=-1, keepdims=True)
    o = o * lax.rsqrt(ms + RMS_EPS) * gain_ref[...] * scal_ref[1]
    o_ref[0] = (o * _silu(gate_ref[0])).astype(o_ref.dtype)


def _diff_attn(pb, pf, bias_tab, scal, gain, bsz, t):
    nh = BRANCH_W // HEAD_W
    nq = t // B_TQ
    return pl.pallas_call(
        _diff_attn_kernel,
        grid=(bsz, nh, nq),
        in_specs=[pl.BlockSpec(memory_space=pltpu.SMEM),
                  pl.BlockSpec((1, B_TQ, HEAD_W), lambda b, h, i: (b, i, h)),
                  pl.BlockSpec((1, t, HEAD_W), lambda b, h, i: (b, 0, nh + h)),
                  pl.BlockSpec((1, t, HEAD_W), lambda b, h, i: (b, 0, 2 * nh + h)),
                  pl.BlockSpec((1, B_TQ, HEAD_W), lambda b, h, i: (b, i, PF_BGATE * nh + h)),
                  pl.BlockSpec((1, B_NEAR + 2, B_BT, B_BT), lambda b, h, i: (h, 0, 0, 0)),
                  pl.BlockSpec((1, HEAD_W), lambda b, h, i: (0, 0))],
        out_specs=pl.BlockSpec((1, B_TQ, HEAD_W), lambda b, h, i: (b, i, h)),
        out_shape=jax.ShapeDtypeStruct((bsz, t, BRANCH_W), BF16),
        scratch_shapes=[pltpu.VMEM((t // B_TK, HEAD_W, B_TK), BF16),
                        pltpu.VMEM((t, 2 * HEAD_W), BF16),
                        pltpu.VMEM((2, B_TQ, B_TK), F32),
                        pltpu.VMEM((2, B_TQ, B_TK), F32),
                        pltpu.VMEM((2, B_TQ, B_TK), BF16),
                        pltpu.VMEM((2, B_TQ, B_TK), BF16),
                        pltpu.VMEM((2, B_TQ, HEAD_W), F32),
                        pltpu.VMEM((2, B_TQ, HEAD_W), F32),
                        pltpu.VMEM((2, B_TQ, 2 * HEAD_W), F32),
                        pltpu.VMEM((2, B_TQ, HEAD_W), F32)],
        compiler_params=_cparams(("arbitrary", "arbitrary", "arbitrary")),
        name="diff_attn",
    )(scal, pb, pb, pb, pf, bias_tab, gain.reshape(1, HEAD_W).astype(F32))


def _delta_kernel(qkv_ref, z_ref, small_ref, conv_ref, par_ref, gain_ref, o_ref,
                  xe_ref, s_ref):
    tc = qkv_ref.shape[1]
    nchunk = tc // CHUNK

    @pl.when(pl.program_id(1) == 0)
    def _():
        xe_ref[0:SUBLANES, :] = jnp.zeros((SUBLANES, xe_ref.shape[1]), F32)
        s_ref[...] = jnp.zeros_like(s_ref)

    xe_ref[SUBLANES:SUBLANES + tc, :] = qkv_ref[0]
    first_tap = SUBLANES - (CONV_K - 1)
    conv = conv_ref[0:1, :] * xe_ref[pl.ds(first_tap, tc), :]
    for kk in range(1, CONV_K):
        conv = conv + conv_ref[kk:kk + 1, :] * xe_ref[pl.ds(first_tap + kk, tc), :]
    xe_ref[0:SUBLANES, :] = xe_ref[tc:tc + SUBLANES, :]
    c = _silu(conv)

    small = small_ref[0]
    beta_all = _sigmoid(small)
    xa = small + par_ref[1:2, :]
    softplus = jnp.maximum(xa, 0.0) + _log1p_exp_neg_abs(xa)
    g_all = -jnp.exp(par_ref[0:1, :]) * softplus

    ri = lax.broadcasted_iota(jnp.int32, (CHUNK, CHUNK), 0)
    ci = lax.broadcasted_iota(jnp.int32, (CHUNK, CHUNK), 1)
    tri = ri >= ci
    strict = ri > ci
    eye = (ri == ci).astype(F32)
    blocks = [ri // size == ci // size for size in (8, 16, 32, CHUNK)]
    rt = lax.broadcasted_iota(jnp.int32, (tc, tc), 0)
    ct = lax.broadcasted_iota(jnp.int32, (tc, tc), 1)
    tri_blocks = jnp.logical_and(rt >= ct, rt // CHUNK == ct // CHUNK).astype(F32)
    sel = (lax.broadcasted_iota(jnp.int32, (SUBLANES, HEAD_W), 1)
           == lax.broadcasted_iota(jnp.int32, (SUBLANES, HEAD_W), 0) + C_HEADS).astype(F32)
    nh = BRANCH_W // HEAD_W
    units = [(ch, h) for ch in range(nchunk) for h in range(nh)]

    gcum_all = _nn(tri_blocks, g_all, HI)
    grow_all = _nt(sel, gcum_all, HI)

    qs, ks, vs, bcols, gcols, glasts, gammas, ms_ = {}, {}, {}, {}, {}, {}, {}, {}
    for h in range(nh):
        qf = c[:, h * HEAD_W:(h + 1) * HEAD_W]
        kf = c[:, BRANCH_W + h * HEAD_W:BRANCH_W + (h + 1) * HEAD_W]
        qf = qf * lax.rsqrt(jnp.sum(qf * qf, axis=-1, keepdims=True) + L2_EPS) * (HEAD_W ** -0.5)
        kf = kf * lax.rsqrt(jnp.sum(kf * kf, axis=-1, keepdims=True) + L2_EPS)
        for ch in range(nchunk):
            r0 = ch * CHUNK
            u_ = (ch, h)
            qs[u_] = qf[r0:r0 + CHUNK]
            ks[u_] = kf[r0:r0 + CHUNK]
            vs[u_] = c[r0:r0 + CHUNK, 2 * BRANCH_W + h * HEAD_W:2 * BRANCH_W + (h + 1) * HEAD_W]
            bcols[u_] = beta_all[r0:r0 + CHUNK, h:h + 1]
            lane = C_HEADS + h
            gcols[u_] = gcum_all[r0:r0 + CHUNK, lane:lane + 1]
            glasts[u_] = gcum_all[r0 + CHUNK - 1:r0 + CHUNK, lane:lane + 1]
            grow = grow_all[h:h + 1, r0:r0 + CHUNK]
            gammas[u_] = jnp.exp(jnp.where(tri, gcols[u_] - grow, NEG))
    for u_ in units:
        kkt = _nt_acc(ks[u_], ks[u_])
        ms_[u_] = jnp.where(strict, bcols[u_] * kkt * gammas[u_], 0.0)

    def prod(a, b):
        return {u_: _nn(_split_cols(a[u_]), _split_rows(b[u_])) for u_ in units}

    pw = {u_: jnp.where(blocks[0], -ms_[u_], 0.0) for u_ in units}
    tinv = {u_: eye + pw[u_] for u_ in units}
    for _ in range(2):
        pw = prod(pw, pw)
        step = prod(tinv, pw)
        tinv = {u_: tinv[u_] + step[u_] for u_ in units}
    for inner, outer in zip(blocks[:-1], blocks[1:]):
        coupling = {u_: jnp.where(jnp.logical_and(outer, jnp.logical_not(inner)), ms_[u_], 0.0)
                    for u_ in units}
        step = prod(tinv, prod(coupling, tinv))
        tinv = {u_: tinv[u_] - step[u_] for u_ in units}

    us, ws, aqks, qds, kds = {}, {}, {}, {}, {}
    for u_ in units:
        egc = jnp.exp(gcols[u_])
        rhs = jnp.concatenate([vs[u_] * bcols[u_], ks[u_] * (bcols[u_] * egc)], axis=1)
        uw = _nn(_split_cols(tinv[u_]), _split_rows(rhs))
        us[u_], ws[u_] = uw[:, :HEAD_W], uw[:, HEAD_W:]
        aqks[u_] = _nt(qs[u_], ks[u_]) * gammas[u_]
        qds[u_] = qs[u_] * egc
        kds[u_] = ks[u_] * jnp.exp(glasts[u_] - gcols[u_])

    states = [s_ref[h] for h in range(nh)]
    for ch in range(nchunk):
        r0 = ch * CHUNK
        v_news, o_inters = {}, {}
        for h in range(nh):
            u_ = (ch, h)
            s = states[h]
            s_hi, s_lo = _hi_lo(s)
            w_hi, w_lo = _hi_lo(ws[u_])
            ws_prod = (_nn(jnp.concatenate([w_hi, w_lo], axis=1), jnp.concatenate([s_hi, s_hi], axis=0))
                       + _nn(w_hi, s_lo))
            v_news[h] = us[u_] - ws_prod
            o_inters[h] = _nn(qds[u_], s)
        for h in range(nh):
            u_ = (ch, h)
            v_new = v_news[h]
            o = o_inters[h] + _nn(aqks[u_], v_new)
            states[h] = (jnp.exp(glasts[u_]) * states[h]
                         + _tn(_split_rows_lhs(kds[u_]), _split_rows(v_new)))
            if ch == nchunk - 1:
                s_ref[h] = states[h]
            ms = jnp.mean(o * o, axis=-1, keepdims=True)
            o = o * lax.rsqrt(ms + RMS_EPS) * gain_ref[...]
            zg = z_ref[0, r0:r0 + CHUNK, h * HEAD_W:(h + 1) * HEAD_W]
            o_ref[0, r0:r0 + CHUNK, h * HEAD_W:(h + 1) * HEAD_W] = (o * _silu(zg)).astype(o_ref.dtype)


def _delta_net(pf, small, conv_w, par, gain, bsz, t, tc=CHUNK):
    cw = 3 * BRANCH_W
    return pl.pallas_call(
        _delta_kernel,
        grid=(bsz, t // tc),
        in_specs=[pl.BlockSpec((1, tc, cw), lambda b, i: (b, i, 0)),
                  pl.BlockSpec((1, tc, BRANCH_W), lambda b, i: (b, i, PF_CZ)),
                  pl.BlockSpec((1, tc, HEAD_W), lambda b, i: (b, i, 0)),
                  pl.BlockSpec((SUBLANES, cw), lambda b, i: (0, 0)),
                  pl.BlockSpec((SUBLANES, HEAD_W), lambda b, i: (0, 0)),
                  pl.BlockSpec((1, HEAD_W), lambda b, i: (0, 0))],
        out_specs=pl.BlockSpec((1, tc, BRANCH_W), lambda b, i: (b, i, 0)),
        out_shape=jax.ShapeDtypeStruct((bsz, t, BRANCH_W), BF16),
        scratch_shapes=[pltpu.VMEM((tc + SUBLANES, cw), F32),
                        pltpu.VMEM((BRANCH_W // HEAD_W, HEAD_W, HEAD_W), F32)],
        compiler_params=_cparams(("arbitrary", "arbitrary")),
        name="delta_net",
    )(pf, pf, small, conv_w, par, gain.reshape(1, HEAD_W).astype(F32))


def _hgrn_kernel(q_ref, f_ref, i_ref, gate_ref, lb_ref, gain_ref, o_ref, st_ref):
    tc = q_ref.shape[1]
    nchunk = tc // CHUNK
    nh = BRANCH_W // HEAD_W
    nsub = CHUNK // SUB

    @pl.when(pl.program_id(1) == 0)
    def _():
        st_ref[...] = jnp.zeros_like(st_ref)

    lb = lb_ref[...]
    df = f_ref[0]
    log_sig = jnp.minimum(df, 0.0) - _log1p_exp_neg_abs(df)
    a = jnp.log(lb)
    b = jnp.log1p(-lb) + log_sig
    logf_all = jnp.maximum(a, b) + _log1p_exp_neg_abs(a - b)
    k_all = (1.0 - lb) * _sigmoid(-df)

    ri = lax.broadcasted_iota(jnp.int32, (CHUNK, 3 * CHUNK), 0)
    ci = lax.broadcasted_iota(jnp.int32, (CHUNK, 3 * CHUNK), 1)
    tri3 = jnp.where(ri >= ci % CHUNK, 1.0, 0.0).astype(BF16)
    sub_row = lax.broadcasted_iota(jnp.int32, (HALF, HEAD_W), 0)
    units = [(ch, h) for ch in range(nchunk) for h in range(nh)]

    bcs = []
    for ch in range(nchunk):
        lf = logf_all[ch * CHUNK:(ch + 1) * CHUNK]
        hi = lf.astype(BF16)
        r1 = lf - hi.astype(F32)
        mid = r1.astype(BF16)
        lo = (r1 - mid.astype(F32)).astype(BF16)
        bcs.append(_nn(tri3, jnp.concatenate([hi, mid, lo], axis=0)))

    qs, ks, vs, bcu, bls, qes, dstate, intra = {}, {}, {}, {}, {}, {}, {}, {}
    for u_ in units:
        ch, h = u_
        r0 = ch * CHUNK
        cs = slice(h * HEAD_W, (h + 1) * HEAD_W)
        qs[u_] = q_ref[0, r0:r0 + CHUNK, cs]
        vs[u_] = i_ref[0, r0:r0 + CHUNK, cs]
        ks[u_] = k_all[r0:r0 + CHUNK, cs]
        bcu[u_] = bcs[ch][:, cs]
        bls[u_] = bcu[u_][CHUNK - 1:CHUNK, :]
        qes[u_] = (qs[u_] * jnp.exp(bcu[u_])).astype(BF16)
        k_end = ks[u_] * jnp.exp(bls[u_] - bcu[u_])
        dstate[u_] = _tn(_split_rows_lhs(vs[u_]), _split_rows(k_end))

    row_c = lax.broadcasted_iota(jnp.int32, (CHUNK, HEAD_W), 0)
    second_half = (row_c // HALF) % 2 == 1
    rr = lax.broadcasted_iota(jnp.int32, (CHUNK, CHUNK), 0)
    cc = lax.broadcasted_iota(jnp.int32, (CHUNK, CHUNK), 1)
    same_block = rr // SUB == cc // SUB
    logits, v_bfs = {}, {}
    for u_ in units:
        q, k, bc = qs[u_], ks[u_], bcu[u_]
        v_bfs[u_] = vs[u_].astype(BF16)
        for si in range(1, nsub):
            rs = slice(si * SUB, (si + 1) * SUB)
            ref_row = bc[si * SUB - 1:si * SUB, :]
            q_dec = (q[rs] * jnp.exp(bc[rs] - ref_row)).astype(BF16)
            k_dec = (k[:si * SUB] * jnp.exp(ref_row - bc[:si * SUB])).astype(BF16)
            logits[u_, si] = _nt(q_dec, k_dec)
        mid_rows = jnp.concatenate(
            [jnp.broadcast_to(bc[si * SUB + HALF - 1:si * SUB + HALF, :], (SUB, HEAD_W))
             for si in range(nsub)], axis=0)
        e_mid = jnp.exp(-jnp.abs(bc - mid_rows))
        q_mid = jnp.where(second_half, q * e_mid, 0.0).astype(BF16)
        k_mid = jnp.where(second_half, 0.0, k * e_mid).astype(BF16)
        logits[u_, "mid"] = jnp.where(same_block, _nt(q_mid, k_mid), 0.0)
    for u_ in units:
        pieces = [jnp.zeros((SUB, HEAD_W), F32)]
        for si in range(1, nsub):
            pieces.append(_nn(logits[u_, si].astype(BF16), v_bfs[u_][:si * SUB]))
        o_b = _nn(logits[u_, "mid"].astype(BF16), v_bfs[u_])
        intra[u_] = jnp.concatenate(pieces, axis=0) + o_b

    diag = {u_: [] for u_ in units}
    for sb in range(CHUNK // HALF):
        rs = slice(sb * HALF, (sb + 1) * HALF)
        o_s = {u_: jnp.zeros((HALF, HEAD_W), F32) for u_ in units}
        for j in range(HALF):
            jj = sb * HALF + j
            for u_ in units:
                q, k, v, bc = qs[u_], ks[u_], vs[u_], bcu[u_]
                e = jnp.exp(jnp.where(sub_row >= j, bc[rs] - bc[jj:jj + 1, :], NEG))
                a_col = jnp.sum(q[rs] * k[jj:jj + 1, :] * e, axis=-1, keepdims=True)
                o_s[u_] = o_s[u_] + a_col * v[jj:jj + 1, :]
        for u_ in units:
            diag[u_].append(o_s[u_])
    for u_ in units:
        intra[u_] = intra[u_] + jnp.concatenate(diag[u_], axis=0)

    states = [st_ref[h] for h in range(nh)]
    for u_ in units:
        ch, h = u_
        r0 = ch * CHUNK
        cs = slice(h * HEAD_W, (h + 1) * HEAD_W)
        o = _nt(qes[u_], states[h].astype(BF16)) + intra[u_]
        states[h] = states[h] * jnp.exp(bls[u_]) + dstate[u_]
        if ch == nchunk - 1:
            st_ref[h] = states[h]
        ms = jnp.mean(o * o, axis=-1, keepdims=True)
        o = o * lax.rsqrt(ms + RMS_EPS) * gain_ref[...]
        o_ref[0, r0:r0 + CHUNK, cs] = (o * _silu(gate_ref[0, r0:r0 + CHUNK, cs])).astype(o_ref.dtype)


def _hgrn(pf, lb, gain, bsz, t, tc=CHUNK):
    def spec(cb):
        return pl.BlockSpec((1, tc, BRANCH_W), lambda b, i: (b, i, cb))

    return pl.pallas_call(
        _hgrn_kernel,
        grid=(bsz, t // tc),
        in_specs=[spec(PF_DQ), spec(PF_DF), spec(PF_DI), spec(PF_DGATE),
                  pl.BlockSpec((1, BRANCH_W), lambda b, i: (0, 0)),
                  pl.BlockSpec((1, HEAD_W), lambda b, i: (0, 0))],
        out_specs=pl.BlockSpec((1, tc, BRANCH_W), lambda b, i: (b, i, 0)),
        out_shape=jax.ShapeDtypeStruct((bsz, t, BRANCH_W), BF16),
        scratch_shapes=[pltpu.VMEM((BRANCH_W // HEAD_W, HEAD_W, HEAD_W), F32)],
        compiler_params=_cparams(("arbitrary", "arbitrary")),
        name="hgrn2",
    )(pf, pf, pf, pf, lb.reshape(1, BRANCH_W), gain.reshape(1, HEAD_W).astype(F32))


def _split_w_in(w):
    o = 0
    parts = {}
    for name, width in (("a_q", 512), ("a_k", 512), ("a_v", 512), ("a_gate", 512),
                        ("b_q", 512), ("b_k", 512), ("b_v", 512), ("b_gate", 512),
                        ("c_qkv", 1536), ("c_z", 512), ("c_beta", 4), ("c_a", 4),
                        ("d_q", 512), ("d_f", 512), ("d_i", 512), ("d_gate", 512),
                        ("merge", 4 * D_MODEL)):
        parts[name] = w[:, o:o + width]
        o += width
    w_f = jnp.concatenate([parts[k] for k in ("c_qkv", "c_z", "a_gate", "b_gate", "d_q", "d_f",
                                              "d_i", "d_gate", "a_q", "a_k", "a_v")], axis=1)
    w_b = jnp.concatenate([parts[k] for k in ("b_q", "b_k", "b_v")], axis=1)
    w_s = jnp.concatenate([parts["c_beta"], parts["c_a"],
                           jnp.zeros((w.shape[0], HEAD_W - 2 * C_HEADS), w.dtype)], axis=1)
    return w_f.astype(BF16), w_b.astype(BF16), w_s.astype(BF16), parts["merge"].astype(BF16)


def kernel(x, norm_gain, w_in, rel_bias, diff_lambda, diff_subln_gain, dn_conv, dn_a_log, dn_dt_bias,
           dn_norm_gain, hg_lb_logits, hg_norm_gain, w_branch, w_out, final_gain):
    bsz, t, d = x.shape
    n = bsz * t
    depth = w_in.shape[0]
    lb_p = jax.nn.softmax(hg_lb_logits.astype(F32), axis=0)
    hg_lb = jnp.clip(jnp.cumsum(lb_p, axis=0) - lb_p[0], 0.0, 1.0)
    bias_a = _dilated_bias(rel_bias[:, :8].astype(F32))
    bias_b = _diff_bias(rel_bias[:, 8:].astype(F32))
    cfar = rel_bias[NUM_BUCKETS - 1, 8:].astype(F32)

    xf = x.reshape(n, d).astype(F32)
    h = _rmsnorm(xf, norm_gain[0], BF16)
    for layer in range(depth):
        w_f, w_b, w_s, w_g = _split_w_in(w_in[layer])
        pf = _mm(h, w_f, F32, name="proj_f32").reshape(bsz, t, PF_COLS)
        pb = _mm(h, w_b, BF16, name="proj_bf16").reshape(bsz, t, 3 * BRANCH_W)
        ps = _mm(h, w_s, F32, name="proj_small").reshape(bsz, t, HEAD_W)
        gates = _mm(h, w_g, BF16, act="sigmoid", name="proj_gates")

        y_a = _dil_attn(pf, bias_a, bsz, t)

        lam_init = 0.8 - 0.6 * math.exp(-0.3 * layer)
        lq1, lk1, lq2, lk2 = diff_lambda[layer].astype(F32)
        lam = jnp.exp(jnp.sum(lq1 * lk1)) - jnp.exp(jnp.sum(lq2 * lk2)) + lam_init
        scal = jnp.concatenate([jnp.stack([lam, jnp.asarray(1.0 - lam_init, F32)]), cfar,
                                jnp.zeros((2,), F32)])
        y_b = _diff_attn(pb, pf, bias_b, scal, diff_subln_gain[layer], bsz, t)

        conv_w = jnp.concatenate([dn_conv[layer].astype(F32),
                                  jnp.zeros((SUBLANES - CONV_K, 3 * BRANCH_W), F32)], axis=0)
        decay_lanes = slice(C_HEADS, 2 * C_HEADS)
        par = jnp.zeros((SUBLANES, HEAD_W), F32)
        par = (par.at[0, decay_lanes].set(dn_a_log[layer].astype(F32))
               .at[1, decay_lanes].set(dn_dt_bias[layer].astype(F32)))
        y_c = _delta_net(pf, ps, conv_w, par, dn_norm_gain[layer], bsz, t, tc=DELTA_TILE)

        y_d = _hgrn(pf, hg_lb[layer], hg_norm_gain[layer], bsz, t, tc=HGRN_TILE)

        ys = [y.reshape(n, BRANCH_W) for y in (y_a, y_b, y_c, y_d)]
        last = layer == depth - 1
        res = _merge_out(gates, ys, w_branch[layer].astype(BF16), w_out[layer].astype(BF16), xf,
                         final_gain if last else norm_gain[layer + 1], last)
        if last:
            out = res
        else:
            xf, h = res
    return out.reshape(bsz, t, d).astype(x.dtype)
```

```python
import functools
import math

import jax
import jax.numpy as jnp
from jax import lax
from jax.experimental import pallas as pl
from jax.experimental.pallas import tpu as pltpu

F32 = jnp.float32
BF16 = jnp.bfloat16
HI = lax.Precision.HIGHEST

SUBLANES = 8
D_MODEL = 2048
BRANCH_W = 512
HEAD_W = 128
C_HEADS = BRANCH_W // HEAD_W
L2_EPS = 1e-6
A_HEAD_DIM = 64
DILATIONS = (1, 4, 16)
BAND = 128
A_TILE = BAND * DILATIONS[-1]
A_GROUP = 4
DIFF_QK = 64
B_BT = 256
B_TQ = 512
B_TK = 512
B_NEAR = 2048 // B_BT + 1
CHUNK = 64
DELTA_TILE = 256
HGRN_TILE = 256
SUB = 16
HALF = SUB // 2
CONV_K = 4
NUM_BUCKETS = 32
MAX_DISTANCE = 2048
RMS_EPS = 1e-6
NEG = -1e30
VMEM_LIMIT = 56 * 1024 * 1024
MM_TM, MM_TN = 2048, 1024
MERGE_TM = 256

PF_CQKV, PF_CZ, PF_AGATE, PF_BGATE, PF_DQ, PF_DF, PF_DI, PF_DGATE, PF_AQ, PF_AK, PF_AV = 0, 3, 4, 5, 6, 7, 8, 9, 10, 11, 12
PF_COLS = 13 * 512


def _cparams(sem):
    return pltpu.CompilerParams(dimension_semantics=sem, vmem_limit_bytes=VMEM_LIMIT)


def _nt(a, b, precision=None):
    return lax.dot_general(a, b, (((1,), (1,)), ((), ())), precision=precision,
                           preferred_element_type=F32)


def _tn(a, b, precision=None):
    return lax.dot_general(a, b, (((0,), (0,)), ((), ())), precision=precision,
                           preferred_element_type=F32)


def _nn(a, b, precision=None):
    return jnp.dot(a, b, precision=precision, preferred_element_type=F32)


def _sigmoid(x):
    return 0.5 * jnp.tanh(0.5 * x) + 0.5


def _silu(x):
    return x * _sigmoid(x)


def _log1p_exp_neg_abs(x):
    return jnp.log1p(jnp.exp(-jnp.abs(x)))


def _hi_lo(x):
    bits = lax.bitcast_convert_type(x, jnp.int32)
    hi = lax.bitcast_convert_type(bits & jnp.int32(-65536), F32)
    return hi, x - hi


def _split_cols(a):
    hl = jnp.concatenate(_hi_lo(a), axis=1)
    return jnp.concatenate([hl, hl], axis=1)


def _split_rows(b):
    hi, lo = _hi_lo(b)
    return jnp.concatenate([hi, hi, lo, lo], axis=0)


def _split_rows_lhs(a):
    hi, lo = _hi_lo(a)
    return jnp.concatenate([hi, lo, hi, lo], axis=0)


def _nt_acc(a, b):
    ah, al = _hi_lo(a)
    bh, bl = _hi_lo(b)
    return (_nt(jnp.concatenate([ah, al], axis=1), jnp.concatenate([bh, bh], axis=1))
            + _nt(ah, bl))


def _rmsnorm_kernel(x_ref, g_ref, o_ref):
    x = x_ref[...]
    ms = jnp.mean(x * x, axis=-1, keepdims=True)
    o_ref[...] = (x * lax.rsqrt(ms + RMS_EPS) * g_ref[...]).astype(o_ref.dtype)


def _rmsnorm(x, gain, out_dtype, tm=512):
    n, d = x.shape
    return pl.pallas_call(
        _rmsnorm_kernel,
        grid=(n // tm,),
        in_specs=[pl.BlockSpec((tm, d), lambda i: (i, 0)),
                  pl.BlockSpec((1, d), lambda i: (0, 0))],
        out_specs=pl.BlockSpec((tm, d), lambda i: (i, 0)),
        out_shape=jax.ShapeDtypeStruct((n, d), out_dtype),
        compiler_params=_cparams(("parallel",)),
        name="rmsnorm",
    )(x, gain.reshape(1, d).astype(F32))


def _mm_kernel(a_ref, w_ref, o_ref, *, act):
    acc = _nn(a_ref[...], w_ref[...])
    if act == "sigmoid":
        acc = _sigmoid(acc)
    o_ref[...] = acc.astype(o_ref.dtype)


def _mm(a, w, out_dtype, act=None, tm=MM_TM, tn=MM_TN, name="mm"):
    n, k = a.shape
    c = w.shape[1]
    while c % tn:
        tn //= 2
    return pl.pallas_call(
        functools.partial(_mm_kernel, act=act),
        grid=(n // tm, c // tn),
        in_specs=[pl.BlockSpec((tm, k), lambda i, j: (i, 0)),
                  pl.BlockSpec((k, tn), lambda i, j: (0, j))],
        out_specs=pl.BlockSpec((tm, tn), lambda i, j: (i, j)),
        out_shape=jax.ShapeDtypeStruct((n, c), out_dtype),
        compiler_params=_cparams(("parallel", "parallel")),
        name=name,
    )(a, w)


def _merge_out_kernel(g0, g1, g2, g3, y0, y1, y2, y3, wbr_ref, wout_ref, x_ref, gain_ref, *out_refs,
                      last):
    mixed = g0[...].astype(F32) * _nn(y0[...], wbr_ref[0])
    mixed += g1[...].astype(F32) * _nn(y1[...], wbr_ref[1])
    mixed += g2[...].astype(F32) * _nn(y2[...], wbr_ref[2])
    mixed += g3[...].astype(F32) * _nn(y3[...], wbr_ref[3])
    x = x_ref[...] + _nn(mixed.astype(BF16), wout_ref[...])
    ms = jnp.mean(x * x, axis=-1, keepdims=True)
    normed = x * lax.rsqrt(ms + RMS_EPS) * gain_ref[...]
    if last:
        out_refs[0][...] = normed
    else:
        out_refs[0][...] = x
        out_refs[1][...] = normed.astype(BF16)


def _merge_out(gates, ys, w_br, w_out, x, next_gain, last, tm=MERGE_TM):
    n = gates.shape[0]
    gate_specs = [pl.BlockSpec((tm, D_MODEL), functools.partial(lambda i, b: (i, b), b=b))
                  for b in range(4)]
    y_specs = [pl.BlockSpec((tm, BRANCH_W), lambda i: (i, 0)) for _ in range(4)]
    resident = dict(pipeline_mode=pl.Buffered(1))
    row_spec = pl.BlockSpec((tm, D_MODEL), lambda i: (i, 0))
    if last:
        out_specs, out_shape = row_spec, jax.ShapeDtypeStruct((n, D_MODEL), F32)
    else:
        out_specs = (row_spec, row_spec)
        out_shape = (jax.ShapeDtypeStruct((n, D_MODEL), F32), jax.ShapeDtypeStruct((n, D_MODEL), BF16))
    return pl.pallas_call(
        functools.partial(_merge_out_kernel, last=last),
        grid=(n // tm,),
        in_specs=gate_specs + y_specs + [
            pl.BlockSpec((4, BRANCH_W, D_MODEL), lambda i: (0, 0, 0), **resident),
            pl.BlockSpec((D_MODEL, D_MODEL), lambda i: (0, 0), **resident),
            row_spec,
            pl.BlockSpec((1, D_MODEL), lambda i: (0, 0))],
        out_specs=out_specs,
        out_shape=out_shape,
        compiler_params=_cparams(("parallel",)),
        name="merge_out",
    )(gates, gates, gates, gates, *ys, w_br, w_out, x, next_gain.reshape(1, D_MODEL).astype(F32))


def _t5_bucket(dist):
    n = jnp.maximum(dist, 0)
    max_exact = NUM_BUCKETS // 2
    nf = jnp.maximum(n, max_exact).astype(F32)
    large = max_exact + (jnp.log(nf / max_exact) / math.log(MAX_DISTANCE / max_exact)
                         * (NUM_BUCKETS - max_exact)).astype(jnp.int32)
    large = jnp.minimum(large, NUM_BUCKETS - 1)
    return jnp.where(n < max_exact, n, large)


def _bucket_lookup(table, bucket):
    tab = table.T.reshape((table.shape[1],) + (1,) * bucket.ndim + (NUM_BUCKETS,))
    out = jnp.zeros((table.shape[1],) + bucket.shape, F32)
    for b in range(NUM_BUCKETS):
        out = jnp.where(bucket[None] == b, tab[..., b], out)
    return out


def _dilated_bias(bias_a):
    qi = jnp.arange(BAND)[:, None]
    kj = jnp.arange(2 * BAND)[None, :]
    rel = qi + BAND - kj
    valid = (rel >= 0) & (rel <= BAND)
    tabs = []
    for dil in DILATIONS:
        b = _bucket_lookup(bias_a, _t5_bucket(rel * dil))
        tabs.append(jnp.where(valid[None], b, NEG))
    return jnp.stack(tabs)


def _diff_bias(bias_b):
    qi = jnp.arange(B_BT)[:, None]
    kj = jnp.arange(B_BT)[None, :]
    tabs = []
    for d in range(-1, B_NEAR + 1):
        rel = d * B_BT + qi - kj
        b = _bucket_lookup(bias_b, _t5_bucket(rel))
        tabs.append(jnp.where((rel >= 0)[None], b, NEG))
    return jnp.stack(tabs, axis=1)


def _dil_attn_kernel(q_ref, kp_ref, kc_ref, vp_ref, vc_ref, gate_ref, bias_ref, o_ref,
                     kcat, vcat, acc_ref, m_ref, l_ref):
    n = pl.program_id(1)
    kcat[0:A_TILE, :] = kp_ref[0]
    kcat[A_TILE:2 * A_TILE, :] = kc_ref[0]
    vcat[0:A_TILE, :] = vp_ref[0]
    vcat[A_TILE:2 * A_TILE, :] = vc_ref[0]
    lane = lax.broadcasted_iota(jnp.int32, (BAND, HEAD_W), 1)
    lo = lane < A_HEAD_DIM
    col = lax.broadcasted_iota(jnp.int32, (BAND, 2 * BAND), 1)
    nblk_tile = A_TILE // BAND

    def rows(start, r):
        return pl.ds(start, BAND, stride=r) if r > 1 else pl.ds(start, BAND)

    trips = [(p, r, it) for p, r in reversed(list(enumerate(DILATIONS)))
             for it in range(nblk_tile // A_GROUP)]
    first_p = trips[0][0]

    def logits_stage(p, r, it):
        starts, first_blk, vvs, ss = [], [], [], {}
        for g in range(A_GROUP):
            idx = it * A_GROUP + g
            j = idx // r
            start = j * (BAND * r) + idx % r
            starts.append(start)
            first_blk.append(j == 0)
            q = (q_ref[0, rows(start, r), :] * (A_HEAD_DIM ** -0.5)).astype(BF16)
            kk = jnp.concatenate([kcat[rows(A_TILE + start - BAND * r, r), :],
                                  kcat[rows(A_TILE + start, r), :]], axis=0).astype(BF16)
            vvs.append(jnp.concatenate([vcat[rows(A_TILE + start - BAND * r, r), :],
                                        vcat[rows(A_TILE + start, r), :]], axis=0).astype(BF16))
            for hh in range(2):
                qz = jnp.where(lo if hh == 0 else jnp.logical_not(lo), q, jnp.zeros_like(q))
                ss[g, hh] = _nt(qz, kk)
        return starts, first_blk, vvs, ss

    def finish_stage(p, r, staged):
        starts, first_blk, vvs, ss = staged
        es, ms, ls, outs = {}, {}, {}, {}
        for g in range(A_GROUP):
            for hh in range(2):
                s = ss[g, hh] + bias_ref[p, hh]
                if first_blk[g]:
                    s = jnp.where(jnp.logical_and(n == 0, col < BAND), NEG, s)
                m = jnp.max(s, axis=-1, keepdims=True)
                ms[g, hh] = jnp.broadcast_to(m, (BAND, HEAD_W))
                es[g, hh] = jnp.exp(s - m).astype(BF16)
        ones_blk = jnp.ones((2 * BAND, HEAD_W), BF16)
        for g in range(A_GROUP):
            v_ones = jnp.concatenate([vvs[g], ones_blk], axis=1)
            for hh in range(2):
                pv = _nn(es[g, hh], v_ones)
                outs[g, hh], ls[g, hh] = pv[:, :HEAD_W], pv[:, HEAD_W:]
        merged = []
        for g in range(A_GROUP):
            o_new = jnp.where(lo, outs[g, 0], outs[g, 1])
            m_new = jnp.where(lo, ms[g, 0], ms[g, 1])
            l_new = jnp.where(lo, ls[g, 0], ls[g, 1])
            if p != first_p:
                rws = rows(starts[g], r)
                m_old = m_ref[rws, :]
                m_tot = jnp.maximum(m_old, m_new)
                a = jnp.exp(m_old - m_tot)
                b = jnp.exp(m_new - m_tot)
                o_new = a * acc_ref[rws, :] + b * o_new
                l_new = a * l_ref[rws, :] + b * l_new
                m_new = m_tot
            merged.append((o_new, m_new, l_new))
        for g in range(A_GROUP):
            rws = rows(starts[g], r)
            acc_ref[rws, :], m_ref[rws, :], l_ref[rws, :] = merged[g]

    staged = logits_stage(*trips[0])
    for i, (p, r, _) in enumerate(trips):
        nxt = logits_stage(*trips[i + 1]) if i + 1 < len(trips) else None
        finish_stage(p, r, staged)
        staged = nxt

    o_ref[0] = (acc_ref[...] / l_ref[...] * _silu(gate_ref[0])).astype(o_ref.dtype)


def _dil_attn(pf, bias_tab, bsz, t):
    nt = t // A_TILE
    hp = BRANCH_W // HEAD_W
    blk = (1, A_TILE, HEAD_W)

    def cur(cb):
        return pl.BlockSpec(blk, lambda b, n, h: (b, n, cb * hp + h))

    def prev(cb):
        return pl.BlockSpec(blk, lambda b, n, h: (b, jnp.maximum(n - 1, 0), cb * hp + h))

    return pl.pallas_call(
        _dil_attn_kernel,
        grid=(bsz, nt, hp),
        in_specs=[cur(PF_AQ), prev(PF_AK), cur(PF_AK), prev(PF_AV), cur(PF_AV), cur(PF_AGATE),
                  pl.BlockSpec((3, 2, BAND, 2 * BAND), lambda b, n, h: (0, h, 0, 0))],
        out_specs=pl.BlockSpec(blk, lambda b, n, h: (b, n, h)),
        out_shape=jax.ShapeDtypeStruct((bsz, t, BRANCH_W), BF16),
        scratch_shapes=[pltpu.VMEM((2 * A_TILE, HEAD_W), F32), pltpu.VMEM((2 * A_TILE, HEAD_W), F32),
                        pltpu.VMEM((A_TILE, HEAD_W), F32), pltpu.VMEM((A_TILE, HEAD_W), F32),
                        pltpu.VMEM((A_TILE, HEAD_W), F32)],
        compiler_params=_cparams(("parallel", "parallel", "parallel")),
        name="dilated_attn",
    )(pf, pf, pf, pf, pf, pf, bias_tab)


def _diff_attn_kernel(scal_ref, q_ref, k_ref, v_ref, gate_ref, bias_ref, gain_ref, o_ref,
                      kt_ref, vx_ref, s0_ref, s1_ref, p0_ref, p1_ref, al0_ref, al1_ref,
                      acc_ref, m_ref):
    h = pl.program_id(1)
    qi = pl.program_id(2)
    nkb = kt_ref.shape[0]
    nrh = B_TQ // B_BT
    nsub = B_TK // B_BT
    ncol = B_TK // HEAD_W

    @pl.when(qi == 0)
    def _():
        def tr(kb, c):
            rows = pl.ds(pl.multiple_of(kb * B_TK, B_TK), B_TK)
            kt_ref[kb] = k_ref[0, rows, :].astype(F32).T.astype(BF16)
            vx_ref[rows, 0:HEAD_W] = v_ref[0, rows, :]
            vx_ref[rows, HEAD_W:2 * HEAD_W] = jnp.ones((B_TK, HEAD_W), BF16)
            return c
        lax.fori_loop(0, nkb, tr, 0)

    lane = lax.broadcasted_iota(jnp.int32, (B_TQ, HEAD_W), 1)
    lo = lane < DIFF_QK
    q = q_ref[0] * (DIFF_QK ** -0.5)
    zero = jnp.zeros_like(q)
    qz = (jnp.where(lo, q, zero), jnp.where(lo, zero, q))
    cfar = scal_ref[2 + h]
    nk = ((qi + 1) * nrh - 1) // nsub + 1
    n_far = jnp.maximum((nrh * qi - B_NEAR + 1) // nsub, 0)
    per_sub = B_BT // HEAD_W

    s_bufs, p_bufs, al_bufs = (s0_ref, s1_ref), (p0_ref, p1_ref), (al0_ref, al1_ref)
    acc_ref[...] = jnp.zeros_like(acc_ref)
    m_ref[...] = jnp.full_like(m_ref, NEG)
    al1_ref[...] = jnp.ones_like(al1_ref)
    p1_ref[...] = jnp.zeros_like(p1_ref)

    row_halves = [slice(rh * B_BT, (rh + 1) * B_BT) for rh in range(nrh)]

    def qk(kb, par):
        kt = kt_ref[jnp.minimum(kb, nk - 1)]
        for mi in range(2):
            for rows in row_halves:
                s_bufs[par][mi, rows, :] = _nn(qz[mi][rows], kt)

    def pv(kb, par):
        kb = jnp.clip(kb, 0, nk - 1)
        vx = vx_ref[pl.ds(pl.multiple_of(kb * B_TK, B_TK), B_TK), :]
        for mi in range(2):
            for rows in row_halves:
                a = al_bufs[par][mi, rows]
                upd = _nn(p_bufs[par][mi, rows, :], vx)
                acc_ref[mi, rows] = jnp.concatenate([a, a], axis=1) * acc_ref[mi, rows] + upd

    def col(par, mi, c, kb, near):
        x = s_bufs[par][mi, :, c * HEAD_W:(c + 1) * HEAD_W]
        if near:
            cc = (c % per_sub) * HEAD_W
            tiles = [jnp.clip(nrh * qi + rh - nsub * kb - c // per_sub, -1, B_NEAR) + 1
                     for rh in range(nrh)]
            x = x + jnp.concatenate([bias_ref[0, tl, :, cc:cc + HEAD_W] for tl in tiles], axis=0)
        return x

    def softmax(kb, par, near):
        shift = 0.0 if near else cfar
        m_sub = []
        for mi in range(2):
            mx = col(par, mi, 0, kb, near)
            for c in range(1, ncol):
                mx = jnp.maximum(mx, col(par, mi, c, kb, near))
            m_prev = m_ref[mi]
            m_next = jnp.maximum(m_prev, jnp.max(mx, axis=-1, keepdims=True) + shift)
            al_bufs[par][mi] = jnp.exp(m_prev - m_next)
            m_ref[mi] = m_next
            m_sub.append(m_next - shift)
        for mi in range(2):
            for c in range(ncol):
                e = jnp.exp(col(par, mi, c, kb, near) - m_sub[mi])
                p_bufs[par][mi, :, c * HEAD_W:(c + 1) * HEAD_W] = e.astype(BF16)

    def pair(u, near):
        for par in range(2):
            t = 2 * u + par
            qk(t + 1, 1 - par)
            pv(t - 1, 1 - par)
            softmax(t, par, near)

    def far_body(u, c):
        pair(u, False)
        return c

    def near_body(u, c):
        pair(u, True)
        return c

    far_pairs = n_far // 2
    full_pairs = nk // 2
    qk(0, 0)
    lax.fori_loop(0, far_pairs, far_body, 0)
    lax.fori_loop(far_pairs, full_pairs, near_body, 0)

    @pl.when(nk % 2 == 1)
    def _():
        pv(nk - 2, 1)
        softmax(nk - 1, 0, True)
        pv(nk - 1, 0)

    @pl.when(nk % 2 == 0)
    def _():
        pv(nk - 1, 1)

    lam = scal_ref[0]
    a1, a2 = acc_ref[0], acc_ref[1]
    o = a1[:, :HEAD_W] / a1[:, HEAD_W:] - lam * (a2[:, :HEAD_W] / a2[:, HEAD_W:])
    ms = jnp.mean(o * o, axis=-1, keepdims=True)
    o = o * lax.rsqrt(ms + RMS_EPS) * gain_ref[...] * scal_ref[1]
    o_ref[0] = (o * _silu(gate_ref[0])).astype(o_ref.dtype)


def _diff_attn(pb, pf, bias_tab, scal, gain, bsz, t):
    nh = BRANCH_W // HEAD_W
    nq = t // B_TQ
    return pl.pallas_call(
        _diff_attn_kernel,
        grid=(bsz, nh, nq),
        in_specs=[pl.BlockSpec(memory_space=pltpu.SMEM),
                  pl.BlockSpec((1, B_TQ, HEAD_W), lambda b, h, i: (b, i, h)),
                  pl.BlockSpec((1, t, HEAD_W), lambda b, h, i: (b, 0, nh + h)),
                  pl.BlockSpec((1, t, HEAD_W), lambda b, h, i: (b, 0, 2 * nh + h)),
                  pl.BlockSpec((1, B_TQ, HEAD_W), lambda b, h, i: (b, i, PF_BGATE * nh + h)),
                  pl.BlockSpec((1, B_NEAR + 2, B_BT, B_BT), lambda b, h, i: (h, 0, 0, 0)),
                  pl.BlockSpec((1, HEAD_W), lambda b, h, i: (0, 0))],
        out_specs=pl.BlockSpec((1, B_TQ, HEAD_W), lambda b, h, i: (b, i, h)),
        out_shape=jax.ShapeDtypeStruct((bsz, t, BRANCH_W), BF16),
        scratch_shapes=[pltpu.VMEM((t // B_TK, HEAD_W, B_TK), BF16),
                        pltpu.VMEM((t, 2 * HEAD_W), BF16),
                        pltpu.VMEM((2, B_TQ, B_TK), F32),
                        pltpu.VMEM((2, B_TQ, B_TK), F32),
                        pltpu.VMEM((2, B_TQ, B_TK), BF16),
                        pltpu.VMEM((2, B_TQ, B_TK), BF16),
                        pltpu.VMEM((2, B_TQ, HEAD_W), F32),
                        pltpu.VMEM((2, B_TQ, HEAD_W), F32),
                        pltpu.VMEM((2, B_TQ, 2 * HEAD_W), F32),
                        pltpu.VMEM((2, B_TQ, HEAD_W), F32)],
        compiler_params=_cparams(("arbitrary", "arbitrary", "arbitrary")),
        name="diff_attn",
    )(scal, pb, pb, pb, pf, bias_tab, gain.reshape(1, HEAD_W).astype(F32))


def _delta_kernel(qkv_ref, z_ref, small_ref, conv_ref, par_ref, gain_ref, o_ref,
                  xe_ref, s_ref):
    tc = qkv_ref.shape[1]
    nchunk = tc // CHUNK

    @pl.when(pl.program_id(1) == 0)
    def _():
        xe_ref[0:SUBLANES, :] = jnp.zeros((SUBLANES, xe_ref.shape[1]), F32)
        s_ref[...] = jnp.zeros_like(s_ref)

    xe_ref[SUBLANES:SUBLANES + tc, :] = qkv_ref[0]
    first_tap = SUBLANES - (CONV_K - 1)
    conv = conv_ref[0:1, :] * xe_ref[pl.ds(first_tap, tc), :]
    for kk in range(1, CONV_K):
        conv = conv + conv_ref[kk:kk + 1, :] * xe_ref[pl.ds(first_tap + kk, tc), :]
    xe_ref[0:SUBLANES, :] = xe_ref[tc:tc + SUBLANES, :]
    c = _silu(conv)

    small = small_ref[0]
    beta_all = _sigmoid(small)
    xa = small + par_ref[1:2, :]
    softplus = jnp.maximum(xa, 0.0) + _log1p_exp_neg_abs(xa)
    g_all = -jnp.exp(par_ref[0:1, :]) * softplus

    ri = lax.broadcasted_iota(jnp.int32, (CHUNK, CHUNK), 0)
    ci = lax.broadcasted_iota(jnp.int32, (CHUNK, CHUNK), 1)
    tri = ri >= ci
    strict = ri > ci
    eye = (ri == ci).astype(F32)
    blocks = [ri // size == ci // size for size in (8, 16, 32, CHUNK)]
    rt = lax.broadcasted_iota(jnp.int32, (tc, tc), 0)
    ct = lax.broadcasted_iota(jnp.int32, (tc, tc), 1)
    tri_blocks = jnp.logical_and(rt >= ct, rt // CHUNK == ct // CHUNK).astype(F32)
    sel = (lax.broadcasted_iota(jnp.int32, (SUBLANES, HEAD_W), 1)
           == lax.broadcasted_iota(jnp.int32, (SUBLANES, HEAD_W), 0) + C_HEADS).astype(F32)
    nh = BRANCH_W // HEAD_W
    units = [(ch, h) for ch in range(nchunk) for h in range(nh)]

    gcum_all = _nn(tri_blocks, g_all, HI)
    grow_all = _nt(sel, gcum_all, HI)

    qs, ks, vs, bcols, gcols, glasts, gammas, ms_ = {}, {}, {}, {}, {}, {}, {}, {}
    for h in range(nh):
        qf = c[:, h * HEAD_W:(h + 1) * HEAD_W]
        kf = c[:, BRANCH_W + h * HEAD_W:BRANCH_W + (h + 1) * HEAD_W]
        qf = qf * lax.rsqrt(jnp.sum(qf * qf, axis=-1, keepdims=True) + L2_EPS) * (HEAD_W ** -0.5)
        kf = kf * lax.rsqrt(jnp.sum(kf * kf, axis=-1, keepdims=True) + L2_EPS)
        for ch in range(nchunk):
            r0 = ch * CHUNK
            u_ = (ch, h)
            qs[u_] = qf[r0:r0 + CHUNK]
            ks[u_] = kf[r0:r0 + CHUNK]
            vs[u_] = c[r0:r0 + CHUNK, 2 * BRANCH_W + h * HEAD_W:2 * BRANCH_W + (h + 1) * HEAD_W]
            bcols[u_] = beta_all[r0:r0 + CHUNK, h:h + 1]
            lane = C_HEADS + h
            gcols[u_] = gcum_all[r0:r0 + CHUNK, lane:lane + 1]
            glasts[u_] = gcum_all[r0 + CHUNK - 1:r0 + CHUNK, lane:lane + 1]
            grow = grow_all[h:h + 1, r0:r0 + CHUNK]
            gammas[u_] = jnp.exp(jnp.where(tri, gcols[u_] - grow, NEG))
    for u_ in units:
        kkt = _nt_acc(ks[u_], ks[u_])
        ms_[u_] = jnp.where(strict, bcols[u_] * kkt * gammas[u_], 0.0)

    def prod(a, b):
        return {u_: _nn(a[u_], b[u_]) for u_ in units}

    pw = {u_: jnp.where(blocks[0], -ms_[u_], 0.0) for u_ in units}
    tinv = {u_: eye + pw[u_] for u_ in units}
    for _ in range(2):
        pw = prod(pw, pw)
        step = prod(tinv, pw)
        tinv = {u_: tinv[u_] + step[u_] for u_ in units}
    for inner, outer in zip(blocks[:-1], blocks[1:]):
        coupling = {u_: jnp.where(jnp.logical_and(outer, jnp.logical_not(inner)), ms_[u_], 0.0)
                    for u_ in units}
        step = prod(tinv, prod(coupling, tinv))
        tinv = {u_: tinv[u_] - step[u_] for u_ in units}
    lower = {u_: _split_cols(eye + ms_[u_]) for u_ in units}
    for _ in range(2):
        resid = {u_: eye - _nn(lower[u_], _split_rows(tinv[u_])) for u_ in units}
        step = prod(tinv, resid)
        tinv = {u_: tinv[u_] + step[u_] for u_ in units}

    us, ws, aqks, qds, kds = {}, {}, {}, {}, {}
    for u_ in units:
        egc = jnp.exp(gcols[u_])
        rhs = jnp.concatenate([vs[u_] * bcols[u_], ks[u_] * (bcols[u_] * egc)], axis=1)
        uw = _nn(_split_cols(tinv[u_]), _split_rows(rhs))
        us[u_], ws[u_] = uw[:, :HEAD_W], uw[:, HEAD_W:]
        aqks[u_] = _nt(qs[u_], ks[u_]) * gammas[u_]
        qds[u_] = qs[u_] * egc
        kds[u_] = ks[u_] * jnp.exp(glasts[u_] - gcols[u_])

    states = [s_ref[h] for h in range(nh)]
    for ch in range(nchunk):
        r0 = ch * CHUNK
        v_news, o_inters = {}, {}
        for h in range(nh):
            u_ = (ch, h)
            s = states[h]
            s_hi, s_lo = _hi_lo(s)
            w_hi, w_lo = _hi_lo(ws[u_])
            ws_prod = (_nn(jnp.concatenate([w_hi, w_lo], axis=1), jnp.concatenate([s_hi, s_hi], axis=0))
                       + _nn(w_hi, s_lo))
            v_news[h] = us[u_] - ws_prod
            o_inters[h] = _nn(qds[u_], s)
        for h in range(nh):
            u_ = (ch, h)
            v_new = v_news[h]
            o = o_inters[h] + _nn(aqks[u_], v_new)
            states[h] = (jnp.exp(glasts[u_]) * states[h]
                         + _tn(_split_rows_lhs(kds[u_]), _split_rows(v_new)))
            if ch == nchunk - 1:
                s_ref[h] = states[h]
            ms = jnp.mean(o * o, axis=-1, keepdims=True)
            o = o * lax.rsqrt(ms + RMS_EPS) * gain_ref[...]
            zg = z_ref[0, r0:r0 + CHUNK, h * HEAD_W:(h + 1) * HEAD_W]
            o_ref[0, r0:r0 + CHUNK, h * HEAD_W:(h + 1) * HEAD_W] = (o * _silu(zg)).astype(o_ref.dtype)


def _delta_net(pf, small, conv_w, par, gain, bsz, t, tc=CHUNK):
    cw = 3 * BRANCH_W
    return pl.pallas_call(
        _delta_kernel,
        grid=(bsz, t // tc),
        in_specs=[pl.BlockSpec((1, tc, cw), lambda b, i: (b, i, 0)),
                  pl.BlockSpec((1, tc, BRANCH_W), lambda b, i: (b, i, PF_CZ)),
                  pl.BlockSpec((1, tc, HEAD_W), lambda b, i: (b, i, 0)),
                  pl.BlockSpec((SUBLANES, cw), lambda b, i: (0, 0)),
                  pl.BlockSpec((SUBLANES, HEAD_W), lambda b, i: (0, 0)),
                  pl.BlockSpec((1, HEAD_W), lambda b, i: (0, 0))],
        out_specs=pl.BlockSpec((1, tc, BRANCH_W), lambda b, i: (b, i, 0)),
        out_shape=jax.ShapeDtypeStruct((bsz, t, BRANCH_W), BF16),
        scratch_shapes=[pltpu.VMEM((tc + SUBLANES, cw), F32),
                        pltpu.VMEM((BRANCH_W // HEAD_W, HEAD_W, HEAD_W), F32)],
        compiler_params=_cparams(("arbitrary", "arbitrary")),
        name="delta_net",
    )(pf, pf, small, conv_w, par, gain.reshape(1, HEAD_W).astype(F32))


def _hgrn_kernel(q_ref, f_ref, i_ref, gate_ref, lb_ref, gain_ref, o_ref, st_ref):
    tc = q_ref.shape[1]
    nchunk = tc // CHUNK
    nh = BRANCH_W // HEAD_W
    nsub = CHUNK // SUB

    @pl.when(pl.program_id(1) == 0)
    def _():
        st_ref[...] = jnp.zeros_like(st_ref)

    lb = lb_ref[...]
    df = f_ref[0]
    log_sig = jnp.minimum(df, 0.0) - _log1p_exp_neg_abs(df)
    a = jnp.log(lb)
    b = jnp.log1p(-lb) + log_sig
    logf_all = jnp.maximum(a, b) + _log1p_exp_neg_abs(a - b)
    k_all = (1.0 - lb) * _sigmoid(-df)

    ri = lax.broadcasted_iota(jnp.int32, (CHUNK, 3 * CHUNK), 0)
    ci = lax.broadcasted_iota(jnp.int32, (CHUNK, 3 * CHUNK), 1)
    tri3 = jnp.where(ri >= ci % CHUNK, 1.0, 0.0).astype(BF16)
    sub_row = lax.broadcasted_iota(jnp.int32, (HALF, HEAD_W), 0)
    units = [(ch, h) for ch in range(nchunk) for h in range(nh)]

    bcs = []
    for ch in range(nchunk):
        lf = logf_all[ch * CHUNK:(ch + 1) * CHUNK]
        hi = lf.astype(BF16)
        r1 = lf - hi.astype(F32)
        mid = r1.astype(BF16)
        lo = (r1 - mid.astype(F32)).astype(BF16)
        bcs.append(_nn(tri3, jnp.concatenate([hi, mid, lo], axis=0)))

    qs, ks, vs, bcu, bls, qes, dstate, intra = {}, {}, {}, {}, {}, {}, {}, {}
    for u_ in units:
        ch, h = u_
        r0 = ch * CHUNK
        cs = slice(h * HEAD_W, (h + 1) * HEAD_W)
        qs[u_] = q_ref[0, r0:r0 + CHUNK, cs]
        vs[u_] = i_ref[0, r0:r0 + CHUNK, cs]
        ks[u_] = k_all[r0:r0 + CHUNK, cs]
        bcu[u_] = bcs[ch][:, cs]
        bls[u_] = bcu[u_][CHUNK - 1:CHUNK, :]
        qes[u_] = (qs[u_] * jnp.exp(bcu[u_])).astype(BF16)
        k_end = ks[u_] * jnp.exp(bls[u_] - bcu[u_])
        dstate[u_] = _tn(_split_rows_lhs(vs[u_]), _split_rows(k_end))

    row_c = lax.broadcasted_iota(jnp.int32, (CHUNK, HEAD_W), 0)
    second_half = (row_c // HALF) % 2 == 1
    rr = lax.broadcasted_iota(jnp.int32, (CHUNK, CHUNK), 0)
    cc = lax.broadcasted_iota(jnp.int32, (CHUNK, CHUNK), 1)
    same_block = rr // SUB == cc // SUB
    logits, v_bfs = {}, {}
    for u_ in units:
        q, k, bc = qs[u_], ks[u_], bcu[u_]
        v_bfs[u_] = vs[u_].astype(BF16)
        for si in range(1, nsub):
            rs = slice(si * SUB, (si + 1) * SUB)
            ref_row = bc[si * SUB - 1:si * SUB, :]
            q_dec = (q[rs] * jnp.exp(bc[rs] - ref_row)).astype(BF16)
            k_dec = (k[:si * SUB] * jnp.exp(ref_row - bc[:si * SUB])).astype(BF16)
            logits[u_, si] = _nt(q_dec, k_dec)
        mid_rows = jnp.concatenate(
            [jnp.broadcast_to(bc[si * SUB + HALF - 1:si * SUB + HALF, :], (SUB, HEAD_W))
             for si in range(nsub)], axis=0)
        e_mid = jnp.exp(-jnp.abs(bc - mid_rows))
        q_mid = jnp.where(second_half, q * e_mid, 0.0).astype(BF16)
        k_mid = jnp.where(second_half, 0.0, k * e_mid).astype(BF16)
        logits[u_, "mid"] = jnp.where(same_block, _nt(q_mid, k_mid), 0.0)
    for u_ in units:
        pieces = [jnp.zeros((SUB, HEAD_W), F32)]
        for si in range(1, nsub):
            pieces.append(_nn(logits[u_, si].astype(BF16), v_bfs[u_][:si * SUB]))
        o_b = _nn(logits[u_, "mid"].astype(BF16), v_bfs[u_])
        intra[u_] = jnp.concatenate(pieces, axis=0) + o_b

    diag = {u_: [] for u_ in units}
    for sb in range(CHUNK // HALF):
        rs = slice(sb * HALF, (sb + 1) * HALF)
        o_s = {u_: jnp.zeros((HALF, HEAD_W), F32) for u_ in units}
        for j in range(HALF):
            jj = sb * HALF + j
            for u_ in units:
                q, k, v, bc = qs[u_], ks[u_], vs[u_], bcu[u_]
                e = jnp.exp(jnp.where(sub_row >= j, bc[rs] - bc[jj:jj + 1, :], NEG))
                a_col = jnp.sum(q[rs] * k[jj:jj + 1, :] * e, axis=-1, keepdims=True)
                o_s[u_] = o_s[u_] + a_col * v[jj:jj + 1, :]
        for u_ in units:
            diag[u_].append(o_s[u_])
    for u_ in units:
        intra[u_] = intra[u_] + jnp.concatenate(diag[u_], axis=0)

    states = [st_ref[h] for h in range(nh)]
    for u_ in units:
        ch, h = u_
        r0 = ch * CHUNK
        cs = slice(h * HEAD_W, (h + 1) * HEAD_W)
        o = _nt(qes[u_], states[h].astype(BF16)) + intra[u_]
        states[h] = states[h] * jnp.exp(bls[u_]) + dstate[u_]
        if ch == nchunk - 1:
            st_ref[h] = states[h]
        ms = jnp.mean(o * o, axis=-1, keepdims=True)
        o = o * lax.rsqrt(ms + RMS_EPS) * gain_ref[...]
        o_ref[0, r0:r0 + CHUNK, cs] = (o * _silu(gate_ref[0, r0:r0 + CHUNK, cs])).astype(o_ref.dtype)


def _hgrn(pf, lb, gain, bsz, t, tc=CHUNK):
    def spec(cb):
        return pl.BlockSpec((1, tc, BRANCH_W), lambda b, i: (b, i, cb))

    return pl.pallas_call(
        _hgrn_kernel,
        grid=(bsz, t // tc),
        in_specs=[spec(PF_DQ), spec(PF_DF), spec(PF_DI), spec(PF_DGATE),
                  pl.BlockSpec((1, BRANCH_W), lambda b, i: (0, 0)),
                  pl.BlockSpec((1, HEAD_W), lambda b, i: (0, 0))],
        out_specs=pl.BlockSpec((1, tc, BRANCH_W), lambda b, i: (b, i, 0)),
        out_shape=jax.ShapeDtypeStruct((bsz, t, BRANCH_W), BF16),
        scratch_shapes=[pltpu.VMEM((BRANCH_W // HEAD_W, HEAD_W, HEAD_W), F32)],
        compiler_params=_cparams(("arbitrary", "arbitrary")),
        name="hgrn2",
    )(pf, pf, pf, pf, lb.reshape(1, BRANCH_W), gain.reshape(1, HEAD_W).astype(F32))


def _split_w_in(w):
    o = 0
    parts = {}
    for name, width in (("a_q", 512), ("a_k", 512), ("a_v", 512), ("a_gate", 512),
                        ("b_q", 512), ("b_k", 512), ("b_v", 512), ("b_gate", 512),
                        ("c_qkv", 1536), ("c_z", 512), ("c_beta", 4), ("c_a", 4),
                        ("d_q", 512), ("d_f", 512), ("d_i", 512), ("d_gate", 512),
                        ("merge", 4 * D_MODEL)):
        parts[name] = w[:, o:o + width]
        o += width
    w_f = jnp.concatenate([parts[k] for k in ("c_qkv", "c_z", "a_gate", "b_gate", "d_q", "d_f",
                                              "d_i", "d_gate", "a_q", "a_k", "a_v")], axis=1)
    w_b = jnp.concatenate([parts[k] for k in ("b_q", "b_k", "b_v")], axis=1)
    w_s = jnp.concatenate([parts["c_beta"], parts["c_a"],
                           jnp.zeros((w.shape[0], HEAD_W - 2 * C_HEADS), w.dtype)], axis=1)
    return w_f.astype(BF16), w_b.astype(BF16), w_s.astype(BF16), parts["merge"].astype(BF16)


def kernel(x, norm_gain, w_in, rel_bias, diff_lambda, diff_subln_gain, dn_conv, dn_a_log, dn_dt_bias,
           dn_norm_gain, hg_lb_logits, hg_norm_gain, w_branch, w_out, final_gain):
    bsz, t, d = x.shape
    n = bsz * t
    depth = w_in.shape[0]
    lb_p = jax.nn.softmax(hg_lb_logits.astype(F32), axis=0)
    hg_lb = jnp.clip(jnp.cumsum(lb_p, axis=0) - lb_p[0], 0.0, 1.0)
    bias_a = _dilated_bias(rel_bias[:, :8].astype(F32))
    bias_b = _diff_bias(rel_bias[:, 8:].astype(F32))
    cfar = rel_bias[NUM_BUCKETS - 1, 8:].astype(F32)

    xf = x.reshape(n, d).astype(F32)
    h = _rmsnorm(xf, norm_gain[0], BF16)
    for layer in range(depth):
        w_f, w_b, w_s, w_g = _split_w_in(w_in[layer])
        pf = _mm(h, w_f, F32, name="proj_f32").reshape(bsz, t, PF_COLS)
        pb = _mm(h, w_b, BF16, name="proj_bf16").reshape(bsz, t, 3 * BRANCH_W)
        ps = _mm(h, w_s, F32, name="proj_small").reshape(bsz, t, HEAD_W)
        gates = _mm(h, w_g, BF16, act="sigmoid", name="proj_gates")

        y_a = _dil_attn(pf, bias_a, bsz, t)

        lam_init = 0.8 - 0.6 * math.exp(-0.3 * layer)
        lq1, lk1, lq2, lk2 = diff_lambda[layer].astype(F32)
        lam = jnp.exp(jnp.sum(lq1 * lk1)) - jnp.exp(jnp.sum(lq2 * lk2)) + lam_init
        scal = jnp.concatenate([jnp.stack([lam, jnp.asarray(1.0 - lam_init, F32)]), cfar,
                                jnp.zeros((2,), F32)])
        y_b = _diff_attn(pb, pf, bias_b, scal, diff_subln_gain[layer], bsz, t)

        conv_w = jnp.concatenate([dn_conv[layer].astype(F32),
                                  jnp.zeros((SUBLANES - CONV_K, 3 * BRANCH_W), F32)], axis=0)
        decay_lanes = slice(C_HEADS, 2 * C_HEADS)
        par = jnp.zeros((SUBLANES, HEAD_W), F32)
        par = (par.at[0, decay_lanes].set(dn_a_log[layer].astype(F32))
               .at[1, decay_lanes].set(dn_dt_bias[layer].astype(F32)))
        y_c = _delta_net(pf, ps, conv_w, par, dn_norm_gain[layer], bsz, t, tc=DELTA_TILE)

        y_d = _hgrn(pf, hg_lb[layer], hg_norm_gain[layer], bsz, t, tc=HGRN_TILE)

        ys = [y.reshape(n, BRANCH_W) for y in (y_a, y_b, y_c, y_d)]
        last = layer == depth - 1
        res = _merge_out(gates, ys, w_branch[layer].astype(BF16), w_out[layer].astype(BF16), xf,
                         final_gain if last else norm_gain[layer + 1], last)
        if last:
            out = res
        else:
            xf, h = res
    return out.reshape(bsz, t, d).astype(x.dtype)
```

```python
import functools
import math

import jax
import jax.numpy as jnp
from jax import lax
from jax.experimental import pallas as pl
from jax.experimental.pallas import tpu as pltpu

F32 = jnp.float32
BF16 = jnp.bfloat16
HI = lax.Precision.HIGHEST

SUBLANES = 8
D_MODEL = 2048
BRANCH_W = 512
HEAD_W = 128
C_HEADS = BRANCH_W // HEAD_W
L2_EPS = 1e-6
A_HEAD_DIM = 64
DILATIONS = (1, 4, 16)
BAND = 128
A_TILE = BAND * DILATIONS[-1]
A_GROUP = 4
DIFF_QK = 64
B_BT = 256
B_TQ = 512
B_TK = 512
B_NEAR = 2048 // B_BT + 1
CHUNK = 64
DELTA_TILE = 256
HGRN_TILE = 256
SUB = 16
HALF = SUB // 2
CONV_K = 4
NUM_BUCKETS = 32
MAX_DISTANCE = 2048
RMS_EPS = 1e-6
NEG = -1e30
VMEM_LIMIT = 56 * 1024 * 1024
MM_TM, MM_TN = 2048, 1024
MERGE_TM = 256

PF_CQKV, PF_CZ, PF_AGATE, PF_BGATE, PF_DQ, PF_DF, PF_DI, PF_DGATE, PF_AQ, PF_AK, PF_AV = 0, 3, 4, 5, 6, 7, 8, 9, 10, 11, 12
PF_COLS = 13 * 512


def _cparams(sem):
    return pltpu.CompilerParams(dimension_semantics=sem, vmem_limit_bytes=VMEM_LIMIT)


def _nt(a, b, precision=None):
    return lax.dot_general(a, b, (((1,), (1,)), ((), ())), precision=precision,
                           preferred_element_type=F32)


def _tn(a, b, precision=None):
    return lax.dot_general(a, b, (((0,), (0,)), ((), ())), precision=precision,
                           preferred_element_type=F32)


def _nn(a, b, precision=None):
    return jnp.dot(a, b, precision=precision, preferred_element_type=F32)


def _sigmoid(x):
    return 0.5 * jnp.tanh(0.5 * x) + 0.5


def _silu(x):
    return x * _sigmoid(x)


def _log1p_exp_neg_abs(x):
    return jnp.log1p(jnp.exp(-jnp.abs(x)))


def _hi_lo(x):
    bits = lax.bitcast_convert_type(x, jnp.int32)
    hi = lax.bitcast_convert_type(bits & jnp.int32(-65536), F32)
    return hi, x - hi


def _split_cols(a):
    hl = jnp.concatenate(_hi_lo(a), axis=1)
    return jnp.concatenate([hl, hl], axis=1)


def _split_rows(b):
    hi, lo = _hi_lo(b)
    return jnp.concatenate([hi, hi, lo, lo], axis=0)


def _split_rows_lhs(a):
    hi, lo = _hi_lo(a)
    return jnp.concatenate([hi, lo, hi, lo], axis=0)


def _nt_acc(a, b):
    ah, al = _hi_lo(a)
    bh, bl = _hi_lo(b)
    return (_nt(jnp.concatenate([ah, al], axis=1), jnp.concatenate([bh, bh], axis=1))
            + _nt(ah, bl))


def _rmsnorm_kernel(x_ref, g_ref, o_ref):
    x = x_ref[...]
    ms = jnp.mean(x * x, axis=-1, keepdims=True)
    o_ref[...] = (x * lax.rsqrt(ms + RMS_EPS) * g_ref[...]).astype(o_ref.dtype)


def _rmsnorm(x, gain, out_dtype, tm=512):
    n, d = x.shape
    return pl.pallas_call(
        _rmsnorm_kernel,
        grid=(n // tm,),
        in_specs=[pl.BlockSpec((tm, d), lambda i: (i, 0)),
                  pl.BlockSpec((1, d), lambda i: (0, 0))],
        out_specs=pl.BlockSpec((tm, d), lambda i: (i, 0)),
        out_shape=jax.ShapeDtypeStruct((n, d), out_dtype),
        compiler_params=_cparams(("parallel",)),
        name="rmsnorm",
    )(x, gain.reshape(1, d).astype(F32))


def _mm_kernel(a_ref, w_ref, o_ref, *, act):
    acc = _nn(a_ref[...], w_ref[...])
    if act == "sigmoid":
        acc = _sigmoid(acc)
    o_ref[...] = acc.astype(o_ref.dtype)


def _mm(a, w, out_dtype, act=None, tm=MM_TM, tn=MM_TN, name="mm"):
    n, k = a.shape
    c = w.shape[1]
    while c % tn:
        tn //= 2
    return pl.pallas_call(
        functools.partial(_mm_kernel, act=act),
        grid=(n // tm, c // tn),
        in_specs=[pl.BlockSpec((tm, k), lambda i, j: (i, 0)),
                  pl.BlockSpec((k, tn), lambda i, j: (0, j))],
        out_specs=pl.BlockSpec((tm, tn), lambda i, j: (i, j)),
        out_shape=jax.ShapeDtypeStruct((n, c), out_dtype),
        compiler_params=_cparams(("parallel", "parallel")),
        name=name,
    )(a, w)


def _merge_out_kernel(g0, g1, g2, g3, y0, y1, y2, y3, wbr_ref, wout_ref, x_ref, gain_ref, *out_refs,
                      last):
    mixed = g0[...].astype(F32) * _nn(y0[...], wbr_ref[0])
    mixed += g1[...].astype(F32) * _nn(y1[...], wbr_ref[1])
    mixed += g2[...].astype(F32) * _nn(y2[...], wbr_ref[2])
    mixed += g3[...].astype(F32) * _nn(y3[...], wbr_ref[3])
    x = x_ref[...] + _nn(mixed.astype(BF16), wout_ref[...])
    ms = jnp.mean(x * x, axis=-1, keepdims=True)
    normed = x * lax.rsqrt(ms + RMS_EPS) * gain_ref[...]
    if last:
        out_refs[0][...] = normed
    else:
        out_refs[0][...] = x
        out_refs[1][...] = normed.astype(BF16)


def _merge_out(gates, ys, w_br, w_out, x, next_gain, last, tm=MERGE_TM):
    n = gates.shape[0]
    gate_specs = [pl.BlockSpec((tm, D_MODEL), functools.partial(lambda i, b: (i, b), b=b))
                  for b in range(4)]
    y_specs = [pl.BlockSpec((tm, BRANCH_W), lambda i: (i, 0)) for _ in range(4)]
    resident = dict(pipeline_mode=pl.Buffered(1))
    row_spec = pl.BlockSpec((tm, D_MODEL), lambda i: (i, 0))
    if last:
        out_specs, out_shape = row_spec, jax.ShapeDtypeStruct((n, D_MODEL), F32)
    else:
        out_specs = (row_spec, row_spec)
        out_shape = (jax.ShapeDtypeStruct((n, D_MODEL), F32), jax.ShapeDtypeStruct((n, D_MODEL), BF16))
    return pl.pallas_call(
        functools.partial(_merge_out_kernel, last=last),
        grid=(n // tm,),
        in_specs=gate_specs + y_specs + [
            pl.BlockSpec((4, BRANCH_W, D_MODEL), lambda i: (0, 0, 0), **resident),
            pl.BlockSpec((D_MODEL, D_MODEL), lambda i: (0, 0), **resident),
            row_spec,
            pl.BlockSpec((1, D_MODEL), lambda i: (0, 0))],
        out_specs=out_specs,
        out_shape=out_shape,
        compiler_params=_cparams(("parallel",)),
        name="merge_out",
    )(gates, gates, gates, gates, *ys, w_br, w_out, x, next_gain.reshape(1, D_MODEL).astype(F32))


def _t5_bucket(dist):
    n = jnp.maximum(dist, 0)
    max_exact = NUM_BUCKETS // 2
    nf = jnp.maximum(n, max_exact).astype(F32)
    large = max_exact + (jnp.log(nf / max_exact) / math.log(MAX_DISTANCE / max_exact)
                         * (NUM_BUCKETS - max_exact)).astype(jnp.int32)
    large = jnp.minimum(large, NUM_BUCKETS - 1)
    return jnp.where(n < max_exact, n, large)


def _bucket_span(d_min, d_max):
    def host_bucket(n):
        n = max(n, 0)
        half = NUM_BUCKETS // 2
        if n < half:
            return n
        return min(half + int(math.log(n / half) / math.log(MAX_DISTANCE / half) * (NUM_BUCKETS - half)),
                   NUM_BUCKETS - 1)
    return max(host_bucket(d_min) - 1, 0), min(host_bucket(d_max) + 1, NUM_BUCKETS - 1)


def _bucket_lookup(table, bucket, span=(0, NUM_BUCKETS - 1)):
    tab = table.T.reshape((table.shape[1],) + (1,) * bucket.ndim + (NUM_BUCKETS,))
    out = jnp.zeros((table.shape[1],) + bucket.shape, F32)
    for b in range(span[0], span[1] + 1):
        out = jnp.where(bucket[None] == b, tab[..., b], out)
    return out


def _dilated_bias(bias_a):
    qi = jnp.arange(BAND)[:, None]
    kj = jnp.arange(2 * BAND)[None, :]
    rel = qi + BAND - kj
    valid = (rel >= 0) & (rel <= BAND)
    tabs = []
    for dil in DILATIONS:
        b = _bucket_lookup(bias_a, _t5_bucket(rel * dil))
        tabs.append(jnp.where(valid[None], b, NEG))
    return jnp.stack(tabs)


def _diff_bias(bias_b):
    qi = jnp.arange(B_BT)[:, None]
    kj = jnp.arange(B_BT)[None, :]
    tabs = []
    for d in range(-1, B_NEAR + 1):
        rel = d * B_BT + qi - kj
        span = _bucket_span(d * B_BT - (B_BT - 1), d * B_BT + (B_BT - 1))
        b = _bucket_lookup(bias_b, _t5_bucket(rel), span)
        tabs.append(jnp.where((rel >= 0)[None], b, NEG))
    return jnp.stack(tabs, axis=1)


def _dil_attn_kernel(q_ref, kp_ref, kc_ref, vp_ref, vc_ref, gate_ref, bias_ref, o_ref,
                     kcat, vcat, acc_ref, m_ref, l_ref):
    n = pl.program_id(1)
    kcat[0:A_TILE, :] = kp_ref[0]
    kcat[A_TILE:2 * A_TILE, :] = kc_ref[0]
    vcat[0:A_TILE, :] = vp_ref[0]
    vcat[A_TILE:2 * A_TILE, :] = vc_ref[0]
    lane = lax.broadcasted_iota(jnp.int32, (BAND, HEAD_W), 1)
    lo = lane < A_HEAD_DIM
    col = lax.broadcasted_iota(jnp.int32, (BAND, 2 * BAND), 1)
    nblk_tile = A_TILE // BAND

    def rows(start, r):
        return pl.ds(start, BAND, stride=r) if r > 1 else pl.ds(start, BAND)

    trips = [(p, r, it) for p, r in reversed(list(enumerate(DILATIONS)))
             for it in range(nblk_tile // A_GROUP)]
    first_p = trips[0][0]

    def logits_stage(p, r, it):
        starts, first_blk, vvs, ss = [], [], [], {}
        for g in range(A_GROUP):
            idx = it * A_GROUP + g
            j = idx // r
            start = j * (BAND * r) + idx % r
            starts.append(start)
            first_blk.append(j == 0)
            q = (q_ref[0, rows(start, r), :] * (A_HEAD_DIM ** -0.5)).astype(BF16)
            kk = jnp.concatenate([kcat[rows(A_TILE + start - BAND * r, r), :],
                                  kcat[rows(A_TILE + start, r), :]], axis=0).astype(BF16)
            vvs.append(jnp.concatenate([vcat[rows(A_TILE + start - BAND * r, r), :],
                                        vcat[rows(A_TILE + start, r), :]], axis=0).astype(BF16))
            for hh in range(2):
                qz = jnp.where(lo if hh == 0 else jnp.logical_not(lo), q, jnp.zeros_like(q))
                ss[g, hh] = _nt(qz, kk)
        return starts, first_blk, vvs, ss

    def finish_stage(p, r, staged):
        starts, first_blk, vvs, ss = staged
        es, ms, ls, outs = {}, {}, {}, {}
        for g in range(A_GROUP):
            for hh in range(2):
                s = ss[g, hh] + bias_ref[p, hh]
                if first_blk[g]:
                    s = jnp.where(jnp.logical_and(n == 0, col < BAND), NEG, s)
                m = jnp.max(s, axis=-1, keepdims=True)
                ms[g, hh] = jnp.broadcast_to(m, (BAND, HEAD_W))
                es[g, hh] = jnp.exp(s - m).astype(BF16)
        ones_blk = jnp.ones((2 * BAND, HEAD_W), BF16)
        for g in range(A_GROUP):
            v_ones = jnp.concatenate([vvs[g], ones_blk], axis=1)
            for hh in range(2):
                pv = _nn(es[g, hh], v_ones)
                outs[g, hh], ls[g, hh] = pv[:, :HEAD_W], pv[:, HEAD_W:]
        merged = []
        for g in range(A_GROUP):
            o_new = jnp.where(lo, outs[g, 0], outs[g, 1])
            m_new = jnp.where(lo, ms[g, 0], ms[g, 1])
            l_new = jnp.where(lo, ls[g, 0], ls[g, 1])
            if p != first_p:
                rws = rows(starts[g], r)
                m_old = m_ref[rws, :]
                m_tot = jnp.maximum(m_old, m_new)
                a = jnp.exp(m_old - m_tot)
                b = jnp.exp(m_new - m_tot)
                o_new = a * acc_ref[rws, :] + b * o_new
                l_new = a * l_ref[rws, :] + b * l_new
                m_new = m_tot
            merged.append((o_new, m_new, l_new))
        for g in range(A_GROUP):
            rws = rows(starts[g], r)
            acc_ref[rws, :], m_ref[rws, :], l_ref[rws, :] = merged[g]

    staged = logits_stage(*trips[0])
    for i, (p, r, _) in enumerate(trips):
        nxt = logits_stage(*trips[i + 1]) if i + 1 < len(trips) else None
        finish_stage(p, r, staged)
        staged = nxt

    o_ref[0] = (acc_ref[...] / l_ref[...] * _silu(gate_ref[0])).astype(o_ref.dtype)


def _dil_attn(pf, bias_tab, bsz, t):
    nt = t // A_TILE
    hp = BRANCH_W // HEAD_W
    blk = (1, A_TILE, HEAD_W)

    def cur(cb):
        return pl.BlockSpec(blk, lambda b, n, h: (b, n, cb * hp + h))

    def prev(cb):
        return pl.BlockSpec(blk, lambda b, n, h: (b, jnp.maximum(n - 1, 0), cb * hp + h))

    return pl.pallas_call(
        _dil_attn_kernel,
        grid=(bsz, nt, hp),
        in_specs=[cur(PF_AQ), prev(PF_AK), cur(PF_AK), prev(PF_AV), cur(PF_AV), cur(PF_AGATE),
                  pl.BlockSpec((3, 2, BAND, 2 * BAND), lambda b, n, h: (0, h, 0, 0))],
        out_specs=pl.BlockSpec(blk, lambda b, n, h: (b, n, h)),
        out_shape=jax.ShapeDtypeStruct((bsz, t, BRANCH_W), BF16),
        scratch_shapes=[pltpu.VMEM((2 * A_TILE, HEAD_W), F32), pltpu.VMEM((2 * A_TILE, HEAD_W), F32),
                        pltpu.VMEM((A_TILE, HEAD_W), F32), pltpu.VMEM((A_TILE, HEAD_W), F32),
                        pltpu.VMEM((A_TILE, HEAD_W), F32)],
        compiler_params=_cparams(("parallel", "parallel", "parallel")),
        name="dilated_attn",
    )(pf, pf, pf, pf, pf, pf, bias_tab)


def _diff_attn_kernel(scal_ref, q_ref, k_ref, v_ref, gate_ref, bias_ref, gain_ref, o_ref,
                      kt_ref, vx_ref, s0_ref, s1_ref, p0_ref, p1_ref, al0_ref, al1_ref,
                      acc_ref, m_ref):
    h = pl.program_id(1)
    qi = pl.program_id(2)
    nkb = kt_ref.shape[0]
    nrh = B_TQ // B_BT
    nsub = B_TK // B_BT
    ncol = B_TK // HEAD_W

    @pl.when(qi == 0)
    def _():
        def tr(kb, c):
            rows = pl.ds(pl.multiple_of(kb * B_TK, B_TK), B_TK)
            kt_ref[kb] = k_ref[0, rows, :].astype(F32).T.astype(BF16)
            vx_ref[rows, 0:HEAD_W] = v_ref[0, rows, :]
            vx_ref[rows, HEAD_W:2 * HEAD_W] = jnp.ones((B_TK, HEAD_W), BF16)
            return c
        lax.fori_loop(0, nkb, tr, 0)

    lane = lax.broadcasted_iota(jnp.int32, (B_TQ, HEAD_W), 1)
    lo = lane < DIFF_QK
    q = q_ref[0] * (DIFF_QK ** -0.5)
    zero = jnp.zeros_like(q)
    qz = (jnp.where(lo, q, zero), jnp.where(lo, zero, q))
    cfar = scal_ref[2 + h]
    nk = ((qi + 1) * nrh - 1) // nsub + 1
    n_far = jnp.maximum((nrh * qi - B_NEAR + 1) // nsub, 0)
    per_sub = B_BT // HEAD_W

    s_bufs, p_bufs, al_bufs = (s0_ref, s1_ref), (p0_ref, p1_ref), (al0_ref, al1_ref)
    acc_ref[...] = jnp.zeros_like(acc_ref)
    m_ref[...] = jnp.full_like(m_ref, NEG)
    al1_ref[...] = jnp.ones_like(al1_ref)
    p1_ref[...] = jnp.zeros_like(p1_ref)

    row_halves = [slice(rh * B_BT, (rh + 1) * B_BT) for rh in range(nrh)]

    def qk(kb, par):
        kt = kt_ref[jnp.minimum(kb, nk - 1)]
        for mi in range(2):
            for rows in row_halves:
                s_bufs[par][mi, rows, :] = _nn(qz[mi][rows], kt)

    def pv(kb, par):
        kb = jnp.clip(kb, 0, nk - 1)
        vx = vx_ref[pl.ds(pl.multiple_of(kb * B_TK, B_TK), B_TK), :]
        for mi in range(2):
            for rows in row_halves:
                a = al_bufs[par][mi, rows]
                upd = _nn(p_bufs[par][mi, rows, :], vx)
                acc_ref[mi, rows] = jnp.concatenate([a, a], axis=1) * acc_ref[mi, rows] + upd

    def col(par, mi, c, kb, near):
        x = s_bufs[par][mi, :, c * HEAD_W:(c + 1) * HEAD_W]
        if near:
            cc = (c % per_sub) * HEAD_W
            tiles = [jnp.clip(nrh * qi + rh - nsub * kb - c // per_sub, -1, B_NEAR) + 1
                     for rh in range(nrh)]
            x = x + jnp.concatenate([bias_ref[0, tl, :, cc:cc + HEAD_W] for tl in tiles], axis=0)
        return x

    def softmax(kb, par, near):
        shift = 0.0 if near else cfar
        m_sub = []
        for mi in range(2):
            mx = col(par, mi, 0, kb, near)
            for c in range(1, ncol):
                mx = jnp.maximum(mx, col(par, mi, c, kb, near))
            m_prev = m_ref[mi]
            m_next = jnp.maximum(m_prev, jnp.max(mx, axis=-1, keepdims=True) + shift)
            al_bufs[par][mi] = jnp.exp(m_prev - m_next)
            m_ref[mi] = m_next
            m_sub.append(m_next - shift)
        for mi in range(2):
            for c in range(ncol):
                e = jnp.exp(col(par, mi, c, kb, near) - m_sub[mi])
                p_bufs[par][mi, :, c * HEAD_W:(c + 1) * HEAD_W] = e.astype(BF16)

    def pair(u, near):
        for par in range(2):
            t = 2 * u + par
            qk(t + 1, 1 - par)
            pv(t - 1, 1 - par)
            softmax(t, par, near)

    def far_body(u, c):
        pair(u, False)
        return c

    def near_body(u, c):
        pair(u, True)
        return c

    far_pairs = n_far // 2
    full_pairs = nk // 2
    qk(0, 0)
    lax.fori_loop(0, far_pairs, far_body, 0)
    lax.fori_loop(far_pairs, full_pairs, near_body, 0)

    @pl.when(nk % 2 == 1)
    def _():
        pv(nk - 2, 1)
        softmax(nk - 1, 0, True)
        pv(nk - 1, 0)

    @pl.when(nk % 2 == 0)
    def _():
        pv(nk - 1, 1)

    lam = scal_ref[0]
    a1, a2 = acc_ref[0], acc_ref[1]
    o = a1[:, :HEAD_W] / a1[:, HEAD_W:] - lam * (a2[:, :HEAD_W] / a2[:, HEAD_W:])
    ms = jnp.mean(o * o, axis=-1, keepdims=True)
    o = o * lax.rsqrt(ms + RMS_EPS) * gain_ref[...] * scal_ref[1]
    o_ref[0] = (o * _silu(gate_ref[0])).astype(o_ref.dtype)


def _diff_attn(pb, pf, bias_tab, scal, gain, bsz, t):
    nh = BRANCH_W // HEAD_W
    nq = t // B_TQ
    return pl.pallas_call(
        _diff_attn_kernel,
        grid=(bsz, nh, nq),
        in_specs=[pl.BlockSpec(memory_space=pltpu.SMEM),
                  pl.BlockSpec((1, B_TQ, HEAD_W), lambda b, h, i: (b, i, h)),
                  pl.BlockSpec((1, t, HEAD_W), lambda b, h, i: (b, 0, nh + h)),
                  pl.BlockSpec((1, t, HEAD_W), lambda b, h, i: (b, 0, 2 * nh + h)),
                  pl.BlockSpec((1, B_TQ, HEAD_W), lambda b, h, i: (b, i, PF_BGATE * nh + h)),
                  pl.BlockSpec((1, B_NEAR + 2, B_BT, B_BT), lambda b, h, i: (h, 0, 0, 0)),
                  pl.BlockSpec((1, HEAD_W), lambda b, h, i: (0, 0))],
        out_specs=pl.BlockSpec((1, B_TQ, HEAD_W), lambda b, h, i: (b, i, h)),
        out_shape=jax.ShapeDtypeStruct((bsz, t, BRANCH_W), BF16),
        scratch_shapes=[pltpu.VMEM((t // B_TK, HEAD_W, B_TK), BF16),
                        pltpu.VMEM((t, 2 * HEAD_W), BF16),
                        pltpu.VMEM((2, B_TQ, B_TK), F32),
                        pltpu.VMEM((2, B_TQ, B_TK), F32),
                        pltpu.VMEM((2, B_TQ, B_TK), BF16),
                        pltpu.VMEM((2, B_TQ, B_TK), BF16),
                        pltpu.VMEM((2, B_TQ, HEAD_W), F32),
                        pltpu.VMEM((2, B_TQ, HEAD_W), F32),
                        pltpu.VMEM((2, B_TQ, 2 * HEAD_W), F32),
                        pltpu.VMEM((2, B_TQ, HEAD_W), F32)],
        compiler_params=_cparams(("arbitrary", "arbitrary", "arbitrary")),
        name="diff_attn",
    )(scal, pb, pb, pb, pf, bias_tab, gain.reshape(1, HEAD_W).astype(F32))


def _delta_kernel(qkv_ref, z_ref, small_ref, conv_ref, par_ref, gain_ref, o_ref,
                  xe_ref, s_ref):
    tc = qkv_ref.shape[1]
    nchunk = tc // CHUNK

    @pl.when(pl.program_id(1) == 0)
    def _():
        xe_ref[0:SUBLANES, :] = jnp.zeros((SUBLANES, xe_ref.shape[1]), F32)
        s_ref[...] = jnp.zeros_like(s_ref)

    xe_ref[SUBLANES:SUBLANES + tc, :] = qkv_ref[0]
    first_tap = SUBLANES - (CONV_K - 1)
    conv = conv_ref[0:1, :] * xe_ref[pl.ds(first_tap, tc), :]
    for kk in range(1, CONV_K):
        conv = conv + conv_ref[kk:kk + 1, :] * xe_ref[pl.ds(first_tap + kk, tc), :]
    xe_ref[0:SUBLANES, :] = xe_ref[tc:tc + SUBLANES, :]
    c = _silu(conv)

    small = small_ref[0]
    beta_all = _sigmoid(small)
    xa = small + par_ref[1:2, :]
    softplus = jnp.maximum(xa, 0.0) + _log1p_exp_neg_abs(xa)
    g_all = -jnp.exp(par_ref[0:1, :]) * softplus

    ri = lax.broadcasted_iota(jnp.int32, (CHUNK, CHUNK), 0)
    ci = lax.broadcasted_iota(jnp.int32, (CHUNK, CHUNK), 1)
    tri = ri >= ci
    strict = ri > ci
    eye = (ri == ci).astype(F32)
    blocks = [ri // size == ci // size for size in (8, 16, 32, CHUNK)]
    rt = lax.broadcasted_iota(jnp.int32, (tc, tc), 0)
    ct = lax.broadcasted_iota(jnp.int32, (tc, tc), 1)
    tri_blocks = jnp.logical_and(rt >= ct, rt // CHUNK == ct // CHUNK).astype(F32)
    sel = (lax.broadcasted_iota(jnp.int32, (SUBLANES, HEAD_W), 1)
           == lax.broadcasted_iota(jnp.int32, (SUBLANES, HEAD_W), 0) + C_HEADS).astype(F32)
    nh = BRANCH_W // HEAD_W
    units = [(ch, h) for ch in range(nchunk) for h in range(nh)]

    gcum_all = _nn(tri_blocks, g_all, HI)
    grow_all = _nt(sel, gcum_all, HI)

    qs, ks, vs, bcols, gcols, glasts, gammas, ms_ = {}, {}, {}, {}, {}, {}, {}, {}
    for h in range(nh):
        qf = c[:, h * HEAD_W:(h + 1) * HEAD_W]
        kf = c[:, BRANCH_W + h * HEAD_W:BRANCH_W + (h + 1) * HEAD_W]
        qf = qf * lax.rsqrt(jnp.sum(qf * qf, axis=-1, keepdims=True) + L2_EPS) * (HEAD_W ** -0.5)
        kf = kf * lax.rsqrt(jnp.sum(kf * kf, axis=-1, keepdims=True) + L2_EPS)
        for ch in range(nchunk):
            r0 = ch * CHUNK
            u_ = (ch, h)
            qs[u_] = qf[r0:r0 + CHUNK]
            ks[u_] = kf[r0:r0 + CHUNK]
            vs[u_] = c[r0:r0 + CHUNK, 2 * BRANCH_W + h * HEAD_W:2 * BRANCH_W + (h + 1) * HEAD_W]
            bcols[u_] = beta_all[r0:r0 + CHUNK, h:h + 1]
            lane = C_HEADS + h
            gcols[u_] = gcum_all[r0:r0 + CHUNK, lane:lane + 1]
            glasts[u_] = gcum_all[r0 + CHUNK - 1:r0 + CHUNK, lane:lane + 1]
            grow = grow_all[h:h + 1, r0:r0 + CHUNK]
            gammas[u_] = jnp.exp(jnp.where(tri, gcols[u_] - grow, NEG))
    for u_ in units:
        kkt = _nt_acc(ks[u_], ks[u_])
        ms_[u_] = jnp.where(strict, bcols[u_] * kkt * gammas[u_], 0.0)

    def prod(a, b):
        return {u_: _nn(a[u_], b[u_]) for u_ in units}

    pw = {u_: jnp.where(blocks[0], -ms_[u_], 0.0) for u_ in units}
    tinv = {u_: eye + pw[u_] for u_ in units}
    for _ in range(2):
        pw = prod(pw, pw)
        step = prod(tinv, pw)
        tinv = {u_: tinv[u_] + step[u_] for u_ in units}
    for inner, outer in zip(blocks[:-1], blocks[1:]):
        coupling = {u_: jnp.where(jnp.logical_and(outer, jnp.logical_not(inner)), ms_[u_], 0.0)
                    for u_ in units}
        step = prod(tinv, prod(coupling, tinv))
        tinv = {u_: tinv[u_] - step[u_] for u_ in units}
    lower = {u_: _split_cols(eye + ms_[u_]) for u_ in units}
    for _ in range(2):
        resid = {u_: eye - _nn(lower[u_], _split_rows(tinv[u_])) for u_ in units}
        step = prod(tinv, resid)
        tinv = {u_: tinv[u_] + step[u_] for u_ in units}

    us, ws, aqks, qds, kds = {}, {}, {}, {}, {}
    for u_ in units:
        egc = jnp.exp(gcols[u_])
        rhs = jnp.concatenate([vs[u_] * bcols[u_], ks[u_] * (bcols[u_] * egc)], axis=1)
        uw = _nn(_split_cols(tinv[u_]), _split_rows(rhs))
        us[u_], ws[u_] = uw[:, :HEAD_W], uw[:, HEAD_W:]
        aqks[u_] = _nt(qs[u_], ks[u_]) * gammas[u_]
        qds[u_] = qs[u_] * egc
        kds[u_] = ks[u_] * jnp.exp(glasts[u_] - gcols[u_])

    states = [s_ref[h] for h in range(nh)]
    for ch in range(nchunk):
        r0 = ch * CHUNK
        v_news, o_inters = {}, {}
        for h in range(nh):
            u_ = (ch, h)
            s = states[h]
            s_hi, s_lo = _hi_lo(s)
            w_hi, w_lo = _hi_lo(ws[u_])
            ws_prod = (_nn(jnp.concatenate([w_hi, w_lo], axis=1), jnp.concatenate([s_hi, s_hi], axis=0))
                       + _nn(w_hi, s_lo))
            v_news[h] = us[u_] - ws_prod
            o_inters[h] = _nn(qds[u_], s)
        for h in range(nh):
            u_ = (ch, h)
            v_new = v_news[h]
            o = o_inters[h] + _nn(aqks[u_], v_new)
            states[h] = (jnp.exp(glasts[u_]) * states[h]
                         + _tn(_split_rows_lhs(kds[u_]), _split_rows(v_new)))
            if ch == nchunk - 1:
                s_ref[h] = states[h]
            ms = jnp.mean(o * o, axis=-1, keepdims=True)
            o = o * lax.rsqrt(ms + RMS_EPS) * gain_ref[...]
            zg = z_ref[0, r0:r0 + CHUNK, h * HEAD_W:(h + 1) * HEAD_W]
            o_ref[0, r0:r0 + CHUNK, h * HEAD_W:(h + 1) * HEAD_W] = (o * _silu(zg)).astype(o_ref.dtype)


def _delta_net(pf, small, conv_w, par, gain, bsz, t, tc=CHUNK):
    cw = 3 * BRANCH_W
    return pl.pallas_call(
        _delta_kernel,
        grid=(bsz, t // tc),
        in_specs=[pl.BlockSpec((1, tc, cw), lambda b, i: (b, i, 0)),
                  pl.BlockSpec((1, tc, BRANCH_W), lambda b, i: (b, i, PF_CZ)),
                  pl.BlockSpec((1, tc, HEAD_W), lambda b, i: (b, i, 0)),
                  pl.BlockSpec((SUBLANES, cw), lambda b, i: (0, 0)),
                  pl.BlockSpec((SUBLANES, HEAD_W), lambda b, i: (0, 0)),
                  pl.BlockSpec((1, HEAD_W), lambda b, i: (0, 0))],
        out_specs=pl.BlockSpec((1, tc, BRANCH_W), lambda b, i: (b, i, 0)),
        out_shape=jax.ShapeDtypeStruct((bsz, t, BRANCH_W), BF16),
        scratch_shapes=[pltpu.VMEM((tc + SUBLANES, cw), F32),
                        pltpu.VMEM((BRANCH_W // HEAD_W, HEAD_W, HEAD_W), F32)],
        compiler_params=_cparams(("arbitrary", "arbitrary")),
        name="delta_net",
    )(pf, pf, small, conv_w, par, gain.reshape(1, HEAD_W).astype(F32))


def _hgrn_kernel(q_ref, f_ref, i_ref, gate_ref, lb_ref, gain_ref, o_ref, st_ref):
    tc = q_ref.shape[1]
    nchunk = tc // CHUNK
    nh = BRANCH_W // HEAD_W
    nsub = CHUNK // SUB

    @pl.when(pl.program_id(1) == 0)
    def _():
        st_ref[...] = jnp.zeros_like(st_ref)

    lb = lb_ref[...]
    df = f_ref[0]
    log_sig = jnp.minimum(df, 0.0) - _log1p_exp_neg_abs(df)
    a = jnp.log(lb)
    b = jnp.log1p(-lb) + log_sig
    logf_all = jnp.maximum(a, b) + _log1p_exp_neg_abs(a - b)
    k_all = (1.0 - lb) * _sigmoid(-df)

    ri = lax.broadcasted_iota(jnp.int32, (CHUNK, 3 * CHUNK), 0)
    ci = lax.broadcasted_iota(jnp.int32, (CHUNK, 3 * CHUNK), 1)
    tri3 = jnp.where(ri >= ci % CHUNK, 1.0, 0.0).astype(BF16)
    sub_row = lax.broadcasted_iota(jnp.int32, (HALF, HEAD_W), 0)
    units = [(ch, h) for ch in range(nchunk) for h in range(nh)]

    bcs = []
    for ch in range(nchunk):
        lf = logf_all[ch * CHUNK:(ch + 1) * CHUNK]
        hi = lf.astype(BF16)
        r1 = lf - hi.astype(F32)
        mid = r1.astype(BF16)
        lo = (r1 - mid.astype(F32)).astype(BF16)
        bcs.append(_nn(tri3, jnp.concatenate([hi, mid, lo], axis=0)))

    qs, ks, vs, bcu, bls, qes, dstate, intra = {}, {}, {}, {}, {}, {}, {}, {}
    for u_ in units:
        ch, h = u_
        r0 = ch * CHUNK
        cs = slice(h * HEAD_W, (h + 1) * HEAD_W)
        qs[u_] = q_ref[0, r0:r0 + CHUNK, cs]
        vs[u_] = i_ref[0, r0:r0 + CHUNK, cs]
        ks[u_] = k_all[r0:r0 + CHUNK, cs]
        bcu[u_] = bcs[ch][:, cs]
        bls[u_] = bcu[u_][CHUNK - 1:CHUNK, :]
        qes[u_] = (qs[u_] * jnp.exp(bcu[u_])).astype(BF16)
        k_end = ks[u_] * jnp.exp(bls[u_] - bcu[u_])
        dstate[u_] = _tn(_split_rows_lhs(vs[u_]), _split_rows(k_end))

    row_c = lax.broadcasted_iota(jnp.int32, (CHUNK, HEAD_W), 0)
    second_half = (row_c // HALF) % 2 == 1
    rr = lax.broadcasted_iota(jnp.int32, (CHUNK, CHUNK), 0)
    cc = lax.broadcasted_iota(jnp.int32, (CHUNK, CHUNK), 1)
    same_block = rr // SUB == cc // SUB
    logits, v_bfs = {}, {}
    for u_ in units:
        q, k, bc = qs[u_], ks[u_], bcu[u_]
        v_bfs[u_] = vs[u_].astype(BF16)
        for si in range(1, nsub):
            rs = slice(si * SUB, (si + 1) * SUB)
            ref_row = bc[si * SUB - 1:si * SUB, :]
            q_dec = (q[rs] * jnp.exp(bc[rs] - ref_row)).astype(BF16)
            k_dec = (k[:si * SUB] * jnp.exp(ref_row - bc[:si * SUB])).astype(BF16)
            logits[u_, si] = _nt(q_dec, k_dec)
        mid_rows = jnp.concatenate(
            [jnp.broadcast_to(bc[si * SUB + HALF - 1:si * SUB + HALF, :], (SUB, HEAD_W))
             for si in range(nsub)], axis=0)
        e_mid = jnp.exp(-jnp.abs(bc - mid_rows))
        q_mid = jnp.where(second_half, q * e_mid, 0.0).astype(BF16)
        k_mid = jnp.where(second_half, 0.0, k * e_mid).astype(BF16)
        logits[u_, "mid"] = jnp.where(same_block, _nt(q_mid, k_mid), 0.0)
    for u_ in units:
        pieces = [jnp.zeros((SUB, HEAD_W), F32)]
        for si in range(1, nsub):
            pieces.append(_nn(logits[u_, si].astype(BF16), v_bfs[u_][:si * SUB]))
        o_b = _nn(logits[u_, "mid"].astype(BF16), v_bfs[u_])
        intra[u_] = jnp.concatenate(pieces, axis=0) + o_b

    diag = {u_: [] for u_ in units}
    for sb in range(CHUNK // HALF):
        rs = slice(sb * HALF, (sb + 1) * HALF)
        o_s = {u_: jnp.zeros((HALF, HEAD_W), F32) for u_ in units}
        for j in range(HALF):
            jj = sb * HALF + j
            for u_ in units:
                q, k, v, bc = qs[u_], ks[u_], vs[u_], bcu[u_]
                e = jnp.exp(jnp.where(sub_row >= j, bc[rs] - bc[jj:jj + 1, :], NEG))
                a_col = jnp.sum(q[rs] * k[jj:jj + 1, :] * e, axis=-1, keepdims=True)
                o_s[u_] = o_s[u_] + a_col * v[jj:jj + 1, :]
        for u_ in units:
            diag[u_].append(o_s[u_])
    for u_ in units:
        intra[u_] = intra[u_] + jnp.concatenate(diag[u_], axis=0)

    states = [st_ref[h] for h in range(nh)]
    for u_ in units:
        ch, h = u_
        r0 = ch * CHUNK
        cs = slice(h * HEAD_W, (h + 1) * HEAD_W)
        o = _nt(qes[u_], states[h].astype(BF16)) + intra[u_]
        states[h] = states[h] * jnp.exp(bls[u_]) + dstate[u_]
        if ch == nchunk - 1:
            st_ref[h] = states[h]
        ms = jnp.mean(o * o, axis=-1, keepdims=True)
        o = o * lax.rsqrt(ms + RMS_EPS) * gain_ref[...]
        o_ref[0, r0:r0 + CHUNK, cs] = (o * _silu(gate_ref[0, r0:r0 + CHUNK, cs])).astype(o_ref.dtype)


def _hgrn(pf, lb, gain, bsz, t, tc=CHUNK):
    def spec(cb):
        return pl.BlockSpec((1, tc, BRANCH_W), lambda b, i: (b, i, cb))

    return pl.pallas_call(
        _hgrn_kernel,
        grid=(bsz, t // tc),
        in_specs=[spec(PF_DQ), spec(PF_DF), spec(PF_DI), spec(PF_DGATE),
                  pl.BlockSpec((1, BRANCH_W), lambda b, i: (0, 0)),
                  pl.BlockSpec((1, HEAD_W), lambda b, i: (0, 0))],
        out_specs=pl.BlockSpec((1, tc, BRANCH_W), lambda b, i: (b, i, 0)),
        out_shape=jax.ShapeDtypeStruct((bsz, t, BRANCH_W), BF16),
        scratch_shapes=[pltpu.VMEM((BRANCH_W // HEAD_W, HEAD_W, HEAD_W), F32)],
        compiler_params=_cparams(("arbitrary", "arbitrary")),
        name="hgrn2",
    )(pf, pf, pf, pf, lb.reshape(1, BRANCH_W), gain.reshape(1, HEAD_W).astype(F32))


def _split_w_in(w):
    o = 0
    parts = {}
    for name, width in (("a_q", 512), ("a_k", 512), ("a_v", 512), ("a_gate", 512),
                        ("b_q", 512), ("b_k", 512), ("b_v", 512), ("b_gate", 512),
                        ("c_qkv", 1536), ("c_z", 512), ("c_beta", 4), ("c_a", 4),
                        ("d_q", 512), ("d_f", 512), ("d_i", 512), ("d_gate", 512),
                        ("merge", 4 * D_MODEL)):
        parts[name] = w[:, o:o + width]
        o += width
    w_f = jnp.concatenate([parts[k] for k in ("c_qkv", "c_z", "a_gate", "b_gate", "d_q", "d_f",
                                              "d_i", "d_gate", "a_q", "a_k", "a_v")], axis=1)
    w_b = jnp.concatenate([parts[k] for k in ("b_q", "b_k", "b_v")], axis=1)
    w_s = jnp.concatenate([parts["c_beta"], parts["c_a"],
                           jnp.zeros((w.shape[0], HEAD_W - 2 * C_HEADS), w.dtype)], axis=1)
    return w_f.astype(BF16), w_b.astype(BF16), w_s.astype(BF16), parts["merge"].astype(BF16)


def kernel(x, norm_gain, w_in, rel_bias, diff_lambda, diff_subln_gain, dn_conv, dn_a_log, dn_dt_bias,
           dn_norm_gain, hg_lb_logits, hg_norm_gain, w_branch, w_out, final_gain):
    bsz, t, d = x.shape
    n = bsz * t
    depth = w_in.shape[0]
    lb_p = jax.nn.softmax(hg_lb_logits.astype(F32), axis=0)
    hg_lb = jnp.clip(jnp.cumsum(lb_p, axis=0) - lb_p[0], 0.0, 1.0)
    bias_a = _dilated_bias(rel_bias[:, :8].astype(F32))
    bias_b = _diff_bias(rel_bias[:, 8:].astype(F32))
    cfar = rel_bias[NUM_BUCKETS - 1, 8:].astype(F32)

    xf = x.reshape(n, d).astype(F32)
    h = _rmsnorm(xf, norm_gain[0], BF16)
    for layer in range(depth):
        w_f, w_b, w_s, w_g = _split_w_in(w_in[layer])
        pf = _mm(h, w_f, F32, name="proj_f32").reshape(bsz, t, PF_COLS)
        pb = _mm(h, w_b, BF16, name="proj_bf16").reshape(bsz, t, 3 * BRANCH_W)
        ps = _mm(h, w_s, F32, name="proj_small").reshape(bsz, t, HEAD_W)
        gates = _mm(h, w_g, BF16, act="sigmoid", name="proj_gates")

        y_a = _dil_attn(pf, bias_a, bsz, t)

        lam_init = 0.8 - 0.6 * math.exp(-0.3 * layer)
        lq1, lk1, lq2, lk2 = diff_lambda[layer].astype(F32)
        lam = jnp.exp(jnp.sum(lq1 * lk1)) - jnp.exp(jnp.sum(lq2 * lk2)) + lam_init
        scal = jnp.concatenate([jnp.stack([lam, jnp.asarray(1.0 - lam_init, F32)]), cfar,
                                jnp.zeros((2,), F32)])
        y_b = _diff_attn(pb, pf, bias_b, scal, diff_subln_gain[layer], bsz, t)

        conv_w = jnp.concatenate([dn_conv[layer].astype(F32),
                                  jnp.zeros((SUBLANES - CONV_K, 3 * BRANCH_W), F32)], axis=0)
        decay_lanes = slice(C_HEADS, 2 * C_HEADS)
        par = jnp.zeros((SUBLANES, HEAD_W), F32)
        par = (par.at[0, decay_lanes].set(dn_a_log[layer].astype(F32))
               .at[1, decay_lanes].set(dn_dt_bias[layer].astype(F32)))
        y_c = _delta_net(pf, ps, conv_w, par, dn_norm_gain[layer], bsz, t, tc=DELTA_TILE)

        y_d = _hgrn(pf, hg_lb[layer], hg_norm_gain[layer], bsz, t, tc=HGRN_TILE)

        ys = [y.reshape(n, BRANCH_W) for y in (y_a, y_b, y_c, y_d)]
        last = layer == depth - 1
        res = _merge_out(gates, ys, w_branch[layer].astype(BF16), w_out[layer].astype(BF16), xf,
                         final_gain if last else norm_gain[layer + 1], last)
        if last:
            out = res
        else:
            xf, h = res
    return out.reshape(bsz, t, d).astype(x.dtype)
```

```python
import functools
import math

import jax
import jax.numpy as jnp
from jax import lax
from jax.experimental import pallas as pl
from jax.experimental.pallas import tpu as pltpu

F32 = jnp.float32
BF16 = jnp.bfloat16
HI = lax.Precision.HIGHEST

SUBLANES = 8
D_MODEL = 2048
BRANCH_W = 512
HEAD_W = 128
C_HEADS = BRANCH_W // HEAD_W
L2_EPS = 1e-6
A_HEAD_DIM = 64
DILATIONS = (1, 4, 16)
BAND = 128
A_TILE = BAND * DILATIONS[-1]
A_GROUP = 4
DIFF_QK = 64
B_BT = 256
B_TQ = 512
B_TK = 512
B_NEAR = 2048 // B_BT + 1
CHUNK = 64
DELTA_TILE = 256
HGRN_TILE = 256
SUB = 16
HALF = SUB // 2
CONV_K = 4
NUM_BUCKETS = 32
MAX_DISTANCE = 2048
RMS_EPS = 1e-6
NEG = -1e30
VMEM_LIMIT = 56 * 1024 * 1024
MM_TM, MM_TN = 2048, 1024
MERGE_TM = 256

PF_CQKV, PF_CZ, PF_AGATE, PF_BGATE, PF_DQ, PF_DF, PF_DI, PF_DGATE, PF_AQ, PF_AK, PF_AV = 0, 3, 4, 5, 6, 7, 8, 9, 10, 11, 12
PF_COLS = 13 * 512


def _cparams(sem):
    return pltpu.CompilerParams(dimension_semantics=sem, vmem_limit_bytes=VMEM_LIMIT)


def _nt(a, b, precision=None):
    return lax.dot_general(a, b, (((1,), (1,)), ((), ())), precision=precision,
                           preferred_element_type=F32)


def _tn(a, b, precision=None):
    return lax.dot_general(a, b, (((0,), (0,)), ((), ())), precision=precision,
                           preferred_element_type=F32)


def _nn(a, b, precision=None):
    return jnp.dot(a, b, precision=precision, preferred_element_type=F32)


def _sigmoid(x):
    return 0.5 * jnp.tanh(0.5 * x) + 0.5


def _silu(x):
    return x * _sigmoid(x)


def _log1p_exp_neg_abs(x):
    return jnp.log1p(jnp.exp(-jnp.abs(x)))


def _hi_lo(x):
    bits = lax.bitcast_convert_type(x, jnp.int32)
    hi = lax.bitcast_convert_type(bits & jnp.int32(-65536), F32)
    return hi, x - hi


def _split_cols(a):
    hl = jnp.concatenate(_hi_lo(a), axis=1)
    return jnp.concatenate([hl, hl], axis=1)


def _split_rows(b):
    hi, lo = _hi_lo(b)
    return jnp.concatenate([hi, hi, lo, lo], axis=0)


def _split_rows_lhs(a):
    hi, lo = _hi_lo(a)
    return jnp.concatenate([hi, lo, hi, lo], axis=0)


def _nt_acc(a, b):
    ah, al = _hi_lo(a)
    bh, bl = _hi_lo(b)
    return (_nt(jnp.concatenate([ah, al], axis=1), jnp.concatenate([bh, bh], axis=1))
            + _nt(ah, bl))


def _rmsnorm_kernel(x_ref, g_ref, o_ref):
    x = x_ref[...]
    ms = jnp.mean(x * x, axis=-1, keepdims=True)
    o_ref[...] = (x * lax.rsqrt(ms + RMS_EPS) * g_ref[...]).astype(o_ref.dtype)


def _rmsnorm(x, gain, out_dtype, tm=512):
    n, d = x.shape
    return pl.pallas_call(
        _rmsnorm_kernel,
        grid=(n // tm,),
        in_specs=[pl.BlockSpec((tm, d), lambda i: (i, 0)),
                  pl.BlockSpec((1, d), lambda i: (0, 0))],
        out_specs=pl.BlockSpec((tm, d), lambda i: (i, 0)),
        out_shape=jax.ShapeDtypeStruct((n, d), out_dtype),
        compiler_params=_cparams(("parallel",)),
        name="rmsnorm",
    )(x, gain.reshape(1, d).astype(F32))


def _mm_kernel(a_ref, w_ref, o_ref, *, act):
    acc = _nn(a_ref[...], w_ref[...])
    if act == "sigmoid":
        acc = _sigmoid(acc)
    o_ref[...] = acc.astype(o_ref.dtype)


def _mm(a, w, out_dtype, act=None, tm=MM_TM, tn=MM_TN, name="mm"):
    n, k = a.shape
    c = w.shape[1]
    while c % tn:
        tn //= 2
    return pl.pallas_call(
        functools.partial(_mm_kernel, act=act),
        grid=(n // tm, c // tn),
        in_specs=[pl.BlockSpec((tm, k), lambda i, j: (i, 0)),
                  pl.BlockSpec((k, tn), lambda i, j: (0, j))],
        out_specs=pl.BlockSpec((tm, tn), lambda i, j: (i, j)),
        out_shape=jax.ShapeDtypeStruct((n, c), out_dtype),
        compiler_params=_cparams(("parallel", "parallel")),
        name=name,
    )(a, w)


def _merge_out_kernel(g0, g1, g2, g3, y0, y1, y2, y3, wbr_ref, wout_ref, x_ref, gain_ref, *out_refs,
                      last):
    mixed = g0[...].astype(F32) * _nn(y0[...], wbr_ref[0])
    mixed += g1[...].astype(F32) * _nn(y1[...], wbr_ref[1])
    mixed += g2[...].astype(F32) * _nn(y2[...], wbr_ref[2])
    mixed += g3[...].astype(F32) * _nn(y3[...], wbr_ref[3])
    x = x_ref[...] + _nn(mixed.astype(BF16), wout_ref[...])
    ms = jnp.mean(x * x, axis=-1, keepdims=True)
    normed = x * lax.rsqrt(ms + RMS_EPS) * gain_ref[...]
    if last:
        out_refs[0][...] = normed
    else:
        out_refs[0][...] = x
        out_refs[1][...] = normed.astype(BF16)


def _merge_out(gates, ys, w_br, w_out, x, next_gain, last, tm=MERGE_TM):
    n = gates.shape[0]
    gate_specs = [pl.BlockSpec((tm, D_MODEL), functools.partial(lambda i, b: (i, b), b=b))
                  for b in range(4)]
    y_specs = [pl.BlockSpec((tm, BRANCH_W), lambda i: (i, 0)) for _ in range(4)]
    resident = dict(pipeline_mode=pl.Buffered(1))
    row_spec = pl.BlockSpec((tm, D_MODEL), lambda i: (i, 0))
    if last:
        out_specs, out_shape = row_spec, jax.ShapeDtypeStruct((n, D_MODEL), F32)
    else:
        out_specs = (row_spec, row_spec)
        out_shape = (jax.ShapeDtypeStruct((n, D_MODEL), F32), jax.ShapeDtypeStruct((n, D_MODEL), BF16))
    return pl.pallas_call(
        functools.partial(_merge_out_kernel, last=last),
        grid=(n // tm,),
        in_specs=gate_specs + y_specs + [
            pl.BlockSpec((4, BRANCH_W, D_MODEL), lambda i: (0, 0, 0), **resident),
            pl.BlockSpec((D_MODEL, D_MODEL), lambda i: (0, 0), **resident),
            row_spec,
            pl.BlockSpec((1, D_MODEL), lambda i: (0, 0))],
        out_specs=out_specs,
        out_shape=out_shape,
        compiler_params=_cparams(("parallel",)),
        name="merge_out",
    )(gates, gates, gates, gates, *ys, w_br, w_out, x, next_gain.reshape(1, D_MODEL).astype(F32))


def _t5_bucket(dist):
    n = jnp.maximum(dist, 0)
    max_exact = NUM_BUCKETS // 2
    nf = jnp.maximum(n, max_exact).astype(F32)
    large = max_exact + (jnp.log(nf / max_exact) / math.log(MAX_DISTANCE / max_exact)
                         * (NUM_BUCKETS - max_exact)).astype(jnp.int32)
    large = jnp.minimum(large, NUM_BUCKETS - 1)
    return jnp.where(n < max_exact, n, large)


def _bucket_span(d_min, d_max):
    def host_bucket(n):
        n = max(n, 0)
        half = NUM_BUCKETS // 2
        if n < half:
            return n
        return min(half + int(math.log(n / half) / math.log(MAX_DISTANCE / half) * (NUM_BUCKETS - half)),
                   NUM_BUCKETS - 1)
    return max(host_bucket(d_min) - 1, 0), min(host_bucket(d_max) + 1, NUM_BUCKETS - 1)


def _bucket_lookup(table, bucket, span=(0, NUM_BUCKETS - 1)):
    tab = table.T.reshape((table.shape[1],) + (1,) * bucket.ndim + (NUM_BUCKETS,))
    out = jnp.zeros((table.shape[1],) + bucket.shape, F32)
    for b in range(span[0], span[1] + 1):
        out = jnp.where(bucket[None] == b, tab[..., b], out)
    return out


def _dilated_bias(bias_a):
    qi = jnp.arange(BAND)[:, None]
    kj = jnp.arange(2 * BAND)[None, :]
    rel = qi + BAND - kj
    valid = (rel >= 0) & (rel <= BAND)
    tabs = []
    for dil in DILATIONS:
        b = _bucket_lookup(bias_a, _t5_bucket(rel * dil))
        tabs.append(jnp.where(valid[None], b, NEG))
    return jnp.stack(tabs)


def _diff_bias(bias_b):
    qi = jnp.arange(B_BT)[:, None]
    kj = jnp.arange(B_BT)[None, :]
    tabs = []
    for d in range(-1, B_NEAR + 1):
        rel = d * B_BT + qi - kj
        span = _bucket_span(d * B_BT - (B_BT - 1), d * B_BT + (B_BT - 1))
        b = _bucket_lookup(bias_b, _t5_bucket(rel), span)
        tabs.append(jnp.where((rel >= 0)[None], b, NEG))
    return jnp.stack(tabs, axis=1)


def _dil_attn_kernel(q_ref, kp_ref, kc_ref, vp_ref, vc_ref, gate_ref, bias_ref, o_ref,
                     acc_ref, m_ref, l_ref):
    n = pl.program_id(1)
    lane = lax.broadcasted_iota(jnp.int32, (BAND, HEAD_W), 1)
    lo = lane < A_HEAD_DIM
    col = lax.broadcasted_iota(jnp.int32, (BAND, 2 * BAND), 1)
    nblk_tile = A_TILE // BAND

    def rows(start, r):
        return pl.ds(start, BAND, stride=r) if r > 1 else pl.ds(start, BAND)

    def with_prev(prev_ref, cur_ref, start, r, j):
        if j == 0:
            before = prev_ref[0, rows(A_TILE + start - BAND * r, r), :]
        else:
            before = cur_ref[0, rows(start - BAND * r, r), :]
        return jnp.concatenate([before, cur_ref[0, rows(start, r), :]], axis=0).astype(BF16)

    trips = [(p, r, it) for p, r in reversed(list(enumerate(DILATIONS)))
             for it in range(nblk_tile // A_GROUP)]
    first_p = trips[0][0]

    def logits_stage(p, r, it):
        starts, first_blk, vvs, ss = [], [], [], {}
        for g in range(A_GROUP):
            idx = it * A_GROUP + g
            j = idx // r
            start = j * (BAND * r) + idx % r
            starts.append(start)
            first_blk.append(j == 0)
            q = (q_ref[0, rows(start, r), :] * (A_HEAD_DIM ** -0.5)).astype(BF16)
            kk = with_prev(kp_ref, kc_ref, start, r, j)
            vvs.append(with_prev(vp_ref, vc_ref, start, r, j))
            for hh in range(2):
                qz = jnp.where(lo if hh == 0 else jnp.logical_not(lo), q, jnp.zeros_like(q))
                ss[g, hh] = _nt(qz, kk)
        return starts, first_blk, vvs, ss

    def finish_stage(p, r, staged):
        starts, first_blk, vvs, ss = staged
        es, ms, ls, outs = {}, {}, {}, {}
        for g in range(A_GROUP):
            for hh in range(2):
                s = ss[g, hh] + bias_ref[p, hh]
                if first_blk[g]:
                    s = jnp.where(jnp.logical_and(n == 0, col < BAND), NEG, s)
                m = jnp.max(s, axis=-1, keepdims=True)
                ms[g, hh] = jnp.broadcast_to(m, (BAND, HEAD_W))
                es[g, hh] = jnp.exp(s - m).astype(BF16)
        ones_blk = jnp.ones((2 * BAND, HEAD_W), BF16)
        for g in range(A_GROUP):
            v_ones = jnp.concatenate([vvs[g], ones_blk], axis=1)
            for hh in range(2):
                pv = _nn(es[g, hh], v_ones)
                outs[g, hh], ls[g, hh] = pv[:, :HEAD_W], pv[:, HEAD_W:]
        merged = []
        for g in range(A_GROUP):
            o_new = jnp.where(lo, outs[g, 0], outs[g, 1])
            m_new = jnp.where(lo, ms[g, 0], ms[g, 1])
            l_new = jnp.where(lo, ls[g, 0], ls[g, 1])
            if p != first_p:
                rws = rows(starts[g], r)
                m_old = m_ref[rws, :]
                m_tot = jnp.maximum(m_old, m_new)
                a = jnp.exp(m_old - m_tot)
                b = jnp.exp(m_new - m_tot)
                o_new = a * acc_ref[rws, :] + b * o_new
                l_new = a * l_ref[rws, :] + b * l_new
                m_new = m_tot
            merged.append((o_new, m_new, l_new))
        for g in range(A_GROUP):
            rws = rows(starts[g], r)
            acc_ref[rws, :], m_ref[rws, :], l_ref[rws, :] = merged[g]

    staged = logits_stage(*trips[0])
    for i, (p, r, _) in enumerate(trips):
        nxt = logits_stage(*trips[i + 1]) if i + 1 < len(trips) else None
        finish_stage(p, r, staged)
        staged = nxt

    o_ref[0] = (acc_ref[...] / l_ref[...] * _silu(gate_ref[0])).astype(o_ref.dtype)


def _dil_attn(pf, bias_tab, bsz, t):
    nt = t // A_TILE
    hp = BRANCH_W // HEAD_W
    blk = (1, A_TILE, HEAD_W)

    def cur(cb):
        return pl.BlockSpec(blk, lambda b, n, h: (b, n, cb * hp + h))

    def prev(cb):
        return pl.BlockSpec(blk, lambda b, n, h: (b, jnp.maximum(n - 1, 0), cb * hp + h))

    return pl.pallas_call(
        _dil_attn_kernel,
        grid=(bsz, nt, hp),
        in_specs=[cur(PF_AQ), prev(PF_AK), cur(PF_AK), prev(PF_AV), cur(PF_AV), cur(PF_AGATE),
                  pl.BlockSpec((3, 2, BAND, 2 * BAND), lambda b, n, h: (0, h, 0, 0))],
        out_specs=pl.BlockSpec(blk, lambda b, n, h: (b, n, h)),
        out_shape=jax.ShapeDtypeStruct((bsz, t, BRANCH_W), BF16),
        scratch_shapes=[pltpu.VMEM((A_TILE, HEAD_W), F32), pltpu.VMEM((A_TILE, HEAD_W), F32),
                        pltpu.VMEM((A_TILE, HEAD_W), F32)],
        compiler_params=_cparams(("parallel", "parallel", "parallel")),
        name="dilated_attn",
    )(pf, pf, pf, pf, pf, pf, bias_tab)


def _diff_attn_kernel(scal_ref, q_ref, k_ref, v_ref, gate_ref, bias_ref, gain_ref, o_ref,
                      kt_ref, vx_ref, s0_ref, s1_ref, p0_ref, p1_ref, al0_ref, al1_ref,
                      acc_ref, m_ref):
    h = pl.program_id(1)
    qi = pl.program_id(2)
    nkb = kt_ref.shape[0]
    nrh = B_TQ // B_BT
    nsub = B_TK // B_BT
    ncol = B_TK // HEAD_W

    @pl.when(qi == 0)
    def _():
        def tr(kb, c):
            rows = pl.ds(pl.multiple_of(kb * B_TK, B_TK), B_TK)
            kt_ref[kb] = k_ref[0, rows, :].astype(F32).T.astype(BF16)
            vx_ref[rows, 0:HEAD_W] = v_ref[0, rows, :]
            vx_ref[rows, HEAD_W:2 * HEAD_W] = jnp.ones((B_TK, HEAD_W), BF16)
            return c
        lax.fori_loop(0, nkb, tr, 0)

    lane = lax.broadcasted_iota(jnp.int32, (B_TQ, HEAD_W), 1)
    lo = lane < DIFF_QK
    q = q_ref[0] * (DIFF_QK ** -0.5)
    zero = jnp.zeros_like(q)
    qz = (jnp.where(lo, q, zero), jnp.where(lo, zero, q))
    cfar = scal_ref[2 + h]
    nk = ((qi + 1) * nrh - 1) // nsub + 1
    n_far = jnp.maximum((nrh * qi - B_NEAR + 1) // nsub, 0)
    per_sub = B_BT // HEAD_W

    s_bufs, p_bufs, al_bufs = (s0_ref, s1_ref), (p0_ref, p1_ref), (al0_ref, al1_ref)
    acc_ref[...] = jnp.zeros_like(acc_ref)
    m_ref[...] = jnp.full_like(m_ref, NEG)
    al1_ref[...] = jnp.ones_like(al1_ref)
    p1_ref[...] = jnp.zeros_like(p1_ref)

    row_halves = [slice(rh * B_BT, (rh + 1) * B_BT) for rh in range(nrh)]

    def qk(kb, par):
        kt = kt_ref[jnp.minimum(kb, nk - 1)]
        for mi in range(2):
            for rows in row_halves:
                s_bufs[par][mi, rows, :] = _nn(qz[mi][rows], kt)

    def pv(kb, par):
        kb = jnp.clip(kb, 0, nk - 1)
        vx = vx_ref[pl.ds(pl.multiple_of(kb * B_TK, B_TK), B_TK), :]
        for mi in range(2):
            for rows in row_halves:
                a = al_bufs[par][mi, rows]
                upd = _nn(p_bufs[par][mi, rows, :], vx)
                acc_ref[mi, rows] = jnp.concatenate([a, a], axis=1) * acc_ref[mi, rows] + upd

    def col(par, mi, c, kb, near):
        x = s_bufs[par][mi, :, c * HEAD_W:(c + 1) * HEAD_W]
        if near:
            cc = (c % per_sub) * HEAD_W
            tiles = [jnp.clip(nrh * qi + rh - nsub * kb - c // per_sub, -1, B_NEAR) + 1
                     for rh in range(nrh)]
            x = x + jnp.concatenate([bias_ref[0, tl, :, cc:cc + HEAD_W] for tl in tiles], axis=0)
        return x

    def softmax(kb, par, near):
        shift = 0.0 if near else cfar
        m_sub = []
        for mi in range(2):
            mx = col(par, mi, 0, kb, near)
            for c in range(1, ncol):
                mx = jnp.maximum(mx, col(par, mi, c, kb, near))
            m_prev = m_ref[mi]
            m_next = jnp.maximum(m_prev, jnp.max(mx, axis=-1, keepdims=True) + shift)
            al_bufs[par][mi] = jnp.exp(m_prev - m_next)
            m_ref[mi] = m_next
            m_sub.append(m_next - shift)
        for mi in range(2):
            for c in range(ncol):
                e = jnp.exp(col(par, mi, c, kb, near) - m_sub[mi])
                p_bufs[par][mi, :, c * HEAD_W:(c + 1) * HEAD_W] = e.astype(BF16)

    def pair(u, near):
        for par in range(2):
            t = 2 * u + par
            qk(t + 1, 1 - par)
            pv(t - 1, 1 - par)
            softmax(t, par, near)

    def far_body(u, c):
        pair(u, False)
        return c

    def near_body(u, c):
        pair(u, True)
        return c

    far_pairs = n_far // 2
    full_pairs = nk // 2
    qk(0, 0)
    lax.fori_loop(0, far_pairs, far_body, 0)
    lax.fori_loop(far_pairs, full_pairs, near_body, 0)

    @pl.when(nk % 2 == 1)
    def _():
        pv(nk - 2, 1)
        softmax(nk - 1, 0, True)
        pv(nk - 1, 0)

    @pl.when(nk % 2 == 0)
    def _():
        pv(nk - 1, 1)

    lam = scal_ref[0]
    a1, a2 = acc_ref[0], acc_ref[1]
    o = a1[:, :HEAD_W] / a1[:, HEAD_W:] - lam * (a2[:, :HEAD_W] / a2[:, HEAD_W:])
    ms = jnp.mean(o * o, axis=-1, keepdims=True)
    o = o * lax.rsqrt(ms + RMS_EPS) * gain_ref[...] * scal_ref[1]
    o_ref[0] = (o * _silu(gate_ref[0])).astype(o_ref.dtype)


def _diff_attn(pb, pf, bias_tab, scal, gain, bsz, t):
    nh = BRANCH_W // HEAD_W
    nq = t // B_TQ
    return pl.pallas_call(
        _diff_attn_kernel,
        grid=(bsz, nh, nq),
        in_specs=[pl.BlockSpec(memory_space=pltpu.SMEM),
                  pl.BlockSpec((1, B_TQ, HEAD_W), lambda b, h, i: (b, i, h)),
                  pl.BlockSpec((1, t, HEAD_W), lambda b, h, i: (b, 0, nh + h)),
                  pl.BlockSpec((1, t, HEAD_W), lambda b, h, i: (b, 0, 2 * nh + h)),
                  pl.BlockSpec((1, B_TQ, HEAD_W), lambda b, h, i: (b, i, PF_BGATE * nh + h)),
                  pl.BlockSpec((1, B_NEAR + 2, B_BT, B_BT), lambda b, h, i: (h, 0, 0, 0)),
                  pl.BlockSpec((1, HEAD_W), lambda b, h, i: (0, 0))],
        out_specs=pl.BlockSpec((1, B_TQ, HEAD_W), lambda b, h, i: (b, i, h)),
        out_shape=jax.ShapeDtypeStruct((bsz, t, BRANCH_W), BF16),
        scratch_shapes=[pltpu.VMEM((t // B_TK, HEAD_W, B_TK), BF16),
                        pltpu.VMEM((t, 2 * HEAD_W), BF16),
                        pltpu.VMEM((2, B_TQ, B_TK), F32),
                        pltpu.VMEM((2, B_TQ, B_TK), F32),
                        pltpu.VMEM((2, B_TQ, B_TK), BF16),
                        pltpu.VMEM((2, B_TQ, B_TK), BF16),
                        pltpu.VMEM((2, B_TQ, HEAD_W), F32),
                        pltpu.VMEM((2, B_TQ, HEAD_W), F32),
                        pltpu.VMEM((2, B_TQ, 2 * HEAD_W), F32),
                        pltpu.VMEM((2, B_TQ, HEAD_W), F32)],
        compiler_params=_cparams(("arbitrary", "arbitrary", "arbitrary")),
        name="diff_attn",
    )(scal, pb, pb, pb, pf, bias_tab, gain.reshape(1, HEAD_W).astype(F32))


def _delta_kernel(qkv_ref, z_ref, small_ref, conv_ref, par_ref, gain_ref, o_ref,
                  xe_ref, s_ref):
    tc = qkv_ref.shape[1]
    nchunk = tc // CHUNK

    @pl.when(pl.program_id(1) == 0)
    def _():
        xe_ref[0:SUBLANES, :] = jnp.zeros((SUBLANES, xe_ref.shape[1]), F32)
        s_ref[...] = jnp.zeros_like(s_ref)

    xe_ref[SUBLANES:SUBLANES + tc, :] = qkv_ref[0]
    first_tap = SUBLANES - (CONV_K - 1)
    conv = conv_ref[0:1, :] * xe_ref[pl.ds(first_tap, tc), :]
    for kk in range(1, CONV_K):
        conv = conv + conv_ref[kk:kk + 1, :] * xe_ref[pl.ds(first_tap + kk, tc), :]
    xe_ref[0:SUBLANES, :] = xe_ref[tc:tc + SUBLANES, :]
    c = _silu(conv)

    small = small_ref[0]
    beta_all = _sigmoid(small)
    xa = small + par_ref[1:2, :]
    softplus = jnp.maximum(xa, 0.0) + _log1p_exp_neg_abs(xa)
    g_all = -jnp.exp(par_ref[0:1, :]) * softplus

    ri = lax.broadcasted_iota(jnp.int32, (CHUNK, CHUNK), 0)
    ci = lax.broadcasted_iota(jnp.int32, (CHUNK, CHUNK), 1)
    tri = ri >= ci
    strict = ri > ci
    eye = (ri == ci).astype(F32)
    blocks = [ri // size == ci // size for size in (8, 16, 32, CHUNK)]
    rt = lax.broadcasted_iota(jnp.int32, (tc, tc), 0)
    ct = lax.broadcasted_iota(jnp.int32, (tc, tc), 1)
    tri_blocks = jnp.logical_and(rt >= ct, rt // CHUNK == ct // CHUNK).astype(F32)
    sel = (lax.broadcasted_iota(jnp.int32, (SUBLANES, HEAD_W), 1)
           == lax.broadcasted_iota(jnp.int32, (SUBLANES, HEAD_W), 0) + C_HEADS).astype(F32)
    nh = BRANCH_W // HEAD_W
    units = [(ch, h) for ch in range(nchunk) for h in range(nh)]

    gcum_all = _nn(tri_blocks, g_all, HI)
    grow_all = _nt(sel, gcum_all, HI)

    qs, ks, vs, bcols, gcols, glasts, gammas, ms_ = {}, {}, {}, {}, {}, {}, {}, {}
    for h in range(nh):
        qf = c[:, h * HEAD_W:(h + 1) * HEAD_W]
        kf = c[:, BRANCH_W + h * HEAD_W:BRANCH_W + (h + 1) * HEAD_W]
        qf = qf * lax.rsqrt(jnp.sum(qf * qf, axis=-1, keepdims=True) + L2_EPS) * (HEAD_W ** -0.5)
        kf = kf * lax.rsqrt(jnp.sum(kf * kf, axis=-1, keepdims=True) + L2_EPS)
        for ch in range(nchunk):
            r0 = ch * CHUNK
            u_ = (ch, h)
            qs[u_] = qf[r0:r0 + CHUNK]
            ks[u_] = kf[r0:r0 + CHUNK]
            vs[u_] = c[r0:r0 + CHUNK, 2 * BRANCH_W + h * HEAD_W:2 * BRANCH_W + (h + 1) * HEAD_W]
            bcols[u_] = beta_all[r0:r0 + CHUNK, h:h + 1]
            lane = C_HEADS + h
            gcols[u_] = gcum_all[r0:r0 + CHUNK, lane:lane + 1]
            glasts[u_] = gcum_all[r0 + CHUNK - 1:r0 + CHUNK, lane:lane + 1]
            grow = grow_all[h:h + 1, r0:r0 + CHUNK]
            gammas[u_] = jnp.exp(jnp.where(tri, gcols[u_] - grow, NEG))
    for u_ in units:
        kkt = _nt_acc(ks[u_], ks[u_])
        ms_[u_] = jnp.where(strict, bcols[u_] * kkt * gammas[u_], 0.0)

    def prod(a, b):
        return {u_: _nn(a[u_], b[u_]) for u_ in units}

    pw = {u_: jnp.where(blocks[0], -ms_[u_], 0.0) for u_ in units}
    tinv = {u_: eye + pw[u_] for u_ in units}
    for _ in range(2):
        pw = prod(pw, pw)
        step = prod(tinv, pw)
        tinv = {u_: tinv[u_] + step[u_] for u_ in units}
    for inner, outer in zip(blocks[:-1], blocks[1:]):
        coupling = {u_: jnp.where(jnp.logical_and(outer, jnp.logical_not(inner)), ms_[u_], 0.0)
                    for u_ in units}
        step = prod(tinv, prod(coupling, tinv))
        tinv = {u_: tinv[u_] - step[u_] for u_ in units}
    lower = {u_: _split_cols(eye + ms_[u_]) for u_ in units}
    for _ in range(2):
        resid = {u_: eye - _nn(lower[u_], _split_rows(tinv[u_])) for u_ in units}
        step = prod(tinv, resid)
        tinv = {u_: tinv[u_] + step[u_] for u_ in units}

    us, ws, aqks, qds, kds = {}, {}, {}, {}, {}
    for u_ in units:
        egc = jnp.exp(gcols[u_])
        rhs = jnp.concatenate([vs[u_] * bcols[u_], ks[u_] * (bcols[u_] * egc)], axis=1)
        uw = _nn(_split_cols(tinv[u_]), _split_rows(rhs))
        us[u_], ws[u_] = uw[:, :HEAD_W], uw[:, HEAD_W:]
        aqks[u_] = _nt(qs[u_], ks[u_]) * gammas[u_]
        qds[u_] = qs[u_] * egc
        kds[u_] = ks[u_] * jnp.exp(glasts[u_] - gcols[u_])

    states = [s_ref[h] for h in range(nh)]
    for ch in range(nchunk):
        r0 = ch * CHUNK
        v_news, o_inters = {}, {}
        for h in range(nh):
            u_ = (ch, h)
            s = states[h]
            s_hi, s_lo = _hi_lo(s)
            w_hi, w_lo = _hi_lo(ws[u_])
            ws_prod = (_nn(jnp.concatenate([w_hi, w_lo], axis=1), jnp.concatenate([s_hi, s_hi], axis=0))
                       + _nn(w_hi, s_lo))
            v_news[h] = us[u_] - ws_prod
            o_inters[h] = _nn(qds[u_], s)
        for h in range(nh):
            u_ = (ch, h)
            v_new = v_news[h]
            o = o_inters[h] + _nn(aqks[u_], v_new)
            states[h] = (jnp.exp(glasts[u_]) * states[h]
                         + _tn(_split_rows_lhs(kds[u_]), _split_rows(v_new)))
            if ch == nchunk - 1:
                s_ref[h] = states[h]
            ms = jnp.mean(o * o, axis=-1, keepdims=True)
            o = o * lax.rsqrt(ms + RMS_EPS) * gain_ref[...]
            zg = z_ref[0, r0:r0 + CHUNK, h * HEAD_W:(h + 1) * HEAD_W]
            o_ref[0, r0:r0 + CHUNK, h * HEAD_W:(h + 1) * HEAD_W] = (o * _silu(zg)).astype(o_ref.dtype)


def _delta_net(pf, small, conv_w, par, gain, bsz, t, tc=CHUNK):
    cw = 3 * BRANCH_W
    return pl.pallas_call(
        _delta_kernel,
        grid=(bsz, t // tc),
        in_specs=[pl.BlockSpec((1, tc, cw), lambda b, i: (b, i, 0)),
                  pl.BlockSpec((1, tc, BRANCH_W), lambda b, i: (b, i, PF_CZ)),
                  pl.BlockSpec((1, tc, HEAD_W), lambda b, i: (b, i, 0)),
                  pl.BlockSpec((SUBLANES, cw), lambda b, i: (0, 0)),
                  pl.BlockSpec((SUBLANES, HEAD_W), lambda b, i: (0, 0)),
                  pl.BlockSpec((1, HEAD_W), lambda b, i: (0, 0))],
        out_specs=pl.BlockSpec((1, tc, BRANCH_W), lambda b, i: (b, i, 0)),
        out_shape=jax.ShapeDtypeStruct((bsz, t, BRANCH_W), BF16),
        scratch_shapes=[pltpu.VMEM((tc + SUBLANES, cw), F32),
                        pltpu.VMEM((BRANCH_W // HEAD_W, HEAD_W, HEAD_W), F32)],
        compiler_params=_cparams(("arbitrary", "arbitrary")),
        name="delta_net",
    )(pf, pf, small, conv_w, par, gain.reshape(1, HEAD_W).astype(F32))


def _hgrn_kernel(q_ref, f_ref, i_ref, gate_ref, lb_ref, gain_ref, o_ref, st_ref):
    tc = q_ref.shape[1]
    nchunk = tc // CHUNK
    nh = BRANCH_W // HEAD_W
    nsub = CHUNK // SUB

    @pl.when(pl.program_id(1) == 0)
    def _():
        st_ref[...] = jnp.zeros_like(st_ref)

    lb = lb_ref[...]
    df = f_ref[0]
    log_sig = jnp.minimum(df, 0.0) - _log1p_exp_neg_abs(df)
    a = jnp.log(lb)
    b = jnp.log1p(-lb) + log_sig
    logf_all = jnp.maximum(a, b) + _log1p_exp_neg_abs(a - b)
    k_all = (1.0 - lb) * _sigmoid(-df)

    ri = lax.broadcasted_iota(jnp.int32, (CHUNK, 3 * CHUNK), 0)
    ci = lax.broadcasted_iota(jnp.int32, (CHUNK, 3 * CHUNK), 1)
    tri3 = jnp.where(ri >= ci % CHUNK, 1.0, 0.0).astype(BF16)
    sub_row = lax.broadcasted_iota(jnp.int32, (HALF, HEAD_W), 0)
    units = [(ch, h) for ch in range(nchunk) for h in range(nh)]

    bcs = []
    for ch in range(nchunk):
        lf = logf_all[ch * CHUNK:(ch + 1) * CHUNK]
        hi = lf.astype(BF16)
        r1 = lf - hi.astype(F32)
        mid = r1.astype(BF16)
        lo = (r1 - mid.astype(F32)).astype(BF16)
        bcs.append(_nn(tri3, jnp.concatenate([hi, mid, lo], axis=0)))

    qs, ks, vs, bcu, bls, qes, dstate, intra = {}, {}, {}, {}, {}, {}, {}, {}
    for u_ in units:
        ch, h = u_
        r0 = ch * CHUNK
        cs = slice(h * HEAD_W, (h + 1) * HEAD_W)
        qs[u_] = q_ref[0, r0:r0 + CHUNK, cs]
        vs[u_] = i_ref[0, r0:r0 + CHUNK, cs]
        ks[u_] = k_all[r0:r0 + CHUNK, cs]
        bcu[u_] = bcs[ch][:, cs]
        bls[u_] = bcu[u_][CHUNK - 1:CHUNK, :]
        qes[u_] = (qs[u_] * jnp.exp(bcu[u_])).astype(BF16)
        k_end = ks[u_] * jnp.exp(bls[u_] - bcu[u_])
        dstate[u_] = _tn(_split_rows_lhs(vs[u_]), _split_rows(k_end))

    row_c = lax.broadcasted_iota(jnp.int32, (CHUNK, HEAD_W), 0)
    second_half = (row_c // HALF) % 2 == 1
    rr = lax.broadcasted_iota(jnp.int32, (CHUNK, CHUNK), 0)
    cc = lax.broadcasted_iota(jnp.int32, (CHUNK, CHUNK), 1)
    same_block = rr // SUB == cc // SUB
    logits, v_bfs = {}, {}
    for u_ in units:
        q, k, bc = qs[u_], ks[u_], bcu[u_]
        v_bfs[u_] = vs[u_].astype(BF16)
        for si in range(1, nsub):
            rs = slice(si * SUB, (si + 1) * SUB)
            ref_row = bc[si * SUB - 1:si * SUB, :]
            q_dec = (q[rs] * jnp.exp(bc[rs] - ref_row)).astype(BF16)
            k_dec = (k[:si * SUB] * jnp.exp(ref_row - bc[:si * SUB])).astype(BF16)
            logits[u_, si] = _nt(q_dec, k_dec)
        mid_rows = jnp.concatenate(
            [jnp.broadcast_to(bc[si * SUB + HALF - 1:si * SUB + HALF, :], (SUB, HEAD_W))
             for si in range(nsub)], axis=0)
        e_mid = jnp.exp(-jnp.abs(bc - mid_rows))
        q_mid = jnp.where(second_half, q * e_mid, 0.0).astype(BF16)
        k_mid = jnp.where(second_half, 0.0, k * e_mid).astype(BF16)
        logits[u_, "mid"] = jnp.where(same_block, _nt(q_mid, k_mid), 0.0)
    for u_ in units:
        pieces = [jnp.zeros((SUB, HEAD_W), F32)]
        for si in range(1, nsub):
            pieces.append(_nn(logits[u_, si].astype(BF16), v_bfs[u_][:si * SUB]))
        o_b = _nn(logits[u_, "mid"].astype(BF16), v_bfs[u_])
        intra[u_] = jnp.concatenate(pieces, axis=0) + o_b

    diag = {u_: [] for u_ in units}
    for sb in range(CHUNK // HALF):
        rs = slice(sb * HALF, (sb + 1) * HALF)
        o_s = {u_: jnp.zeros((HALF, HEAD_W), F32) for u_ in units}
        for j in range(HALF):
            jj = sb * HALF + j
            for u_ in units:
                q, k, v, bc = qs[u_], ks[u_], vs[u_], bcu[u_]
                e = jnp.exp(jnp.where(sub_row >= j, bc[rs] - bc[jj:jj + 1, :], NEG))
                a_col = jnp.sum(q[rs] * k[jj:jj + 1, :] * e, axis=-1, keepdims=True)
                o_s[u_] = o_s[u_] + a_col * v[jj:jj + 1, :]
        for u_ in units:
            diag[u_].append(o_s[u_])
    for u_ in units:
        intra[u_] = intra[u_] + jnp.concatenate(diag[u_], axis=0)

    states = [st_ref[h] for h in range(nh)]
    for u_ in units:
        ch, h = u_
        r0 = ch * CHUNK
        cs = slice(h * HEAD_W, (h + 1) * HEAD_W)
        o = _nt(qes[u_], states[h].astype(BF16)) + intra[u_]
        states[h] = states[h] * jnp.exp(bls[u_]) + dstate[u_]
        if ch == nchunk - 1:
            st_ref[h] = states[h]
        ms = jnp.mean(o * o, axis=-1, keepdims=True)
        o = o * lax.rsqrt(ms + RMS_EPS) * gain_ref[...]
        o_ref[0, r0:r0 + CHUNK, cs] = (o * _silu(gate_ref[0, r0:r0 + CHUNK, cs])).astype(o_ref.dtype)


def _hgrn(pf, lb, gain, bsz, t, tc=CHUNK):
    def spec(cb):
        return pl.BlockSpec((1, tc, BRANCH_W), lambda b, i: (b, i, cb))

    return pl.pallas_call(
        _hgrn_kernel,
        grid=(bsz, t // tc),
        in_specs=[spec(PF_DQ), spec(PF_DF), spec(PF_DI), spec(PF_DGATE),
                  pl.BlockSpec((1, BRANCH_W), lambda b, i: (0, 0)),
                  pl.BlockSpec((1, HEAD_W), lambda b, i: (0, 0))],
        out_specs=pl.BlockSpec((1, tc, BRANCH_W), lambda b, i: (b, i, 0)),
        out_shape=jax.ShapeDtypeStruct((bsz, t, BRANCH_W), BF16),
        scratch_shapes=[pltpu.VMEM((BRANCH_W // HEAD_W, HEAD_W, HEAD_W), F32)],
        compiler_params=_cparams(("arbitrary", "arbitrary")),
        name="hgrn2",
    )(pf, pf, pf, pf, lb.reshape(1, BRANCH_W), gain.reshape(1, HEAD_W).astype(F32))


def _split_w_in(w):
    o = 0
    parts = {}
    for name, width in (("a_q", 512), ("a_k", 512), ("a_v", 512), ("a_gate", 512),
                        ("b_q", 512), ("b_k", 512), ("b_v", 512), ("b_gate", 512),
                        ("c_qkv", 1536), ("c_z", 512), ("c_beta", 4), ("c_a", 4),
                        ("d_q", 512), ("d_f", 512), ("d_i", 512), ("d_gate", 512),
                        ("merge", 4 * D_MODEL)):
        parts[name] = w[:, o:o + width]
        o += width
    w_f = jnp.concatenate([parts[k] for k in ("c_qkv", "c_z", "a_gate", "b_gate", "d_q", "d_f",
                                              "d_i", "d_gate", "a_q", "a_k", "a_v")], axis=1)
    w_b = jnp.concatenate([parts[k] for k in ("b_q", "b_k", "b_v")], axis=1)
    w_s = jnp.concatenate([parts["c_beta"], parts["c_a"],
                           jnp.zeros((w.shape[0], HEAD_W - 2 * C_HEADS), w.dtype)], axis=1)
    return w_f.astype(BF16), w_b.astype(BF16), w_s.astype(BF16), parts["merge"].astype(BF16)


def kernel(x, norm_gain, w_in, rel_bias, diff_lambda, diff_subln_gain, dn_conv, dn_a_log, dn_dt_bias,
           dn_norm_gain, hg_lb_logits, hg_norm_gain, w_branch, w_out, final_gain):
    bsz, t, d = x.shape
    n = bsz * t
    depth = w_in.shape[0]
    lb_p = jax.nn.softmax(hg_lb_logits.astype(F32), axis=0)
    hg_lb = jnp.clip(jnp.cumsum(lb_p, axis=0) - lb_p[0], 0.0, 1.0)
    bias_a = _dilated_bias(rel_bias[:, :8].astype(F32))
    bias_b = _diff_bias(rel_bias[:, 8:].astype(F32))
    cfar = rel_bias[NUM_BUCKETS - 1, 8:].astype(F32)

    xf = x.reshape(n, d).astype(F32)
    h = _rmsnorm(xf, norm_gain[0], BF16)
    for layer in range(depth):
        w_f, w_b, w_s, w_g = _split_w_in(w_in[layer])
        pf = _mm(h, w_f, F32, name="proj_f32").reshape(bsz, t, PF_COLS)
        pb = _mm(h, w_b, BF16, name="proj_bf16").reshape(bsz, t, 3 * BRANCH_W)
        ps = _mm(h, w_s, F32, name="proj_small").reshape(bsz, t, HEAD_W)
        gates = _mm(h, w_g, BF16, act="sigmoid", name="proj_gates")

        y_a = _dil_attn(pf, bias_a, bsz, t)

        lam_init = 0.8 - 0.6 * math.exp(-0.3 * layer)
        lq1, lk1, lq2, lk2 = diff_lambda[layer].astype(F32)
        lam = jnp.exp(jnp.sum(lq1 * lk1)) - jnp.exp(jnp.sum(lq2 * lk2)) + lam_init
        scal = jnp.concatenate([jnp.stack([lam, jnp.asarray(1.0 - lam_init, F32)]), cfar,
                                jnp.zeros((2,), F32)])
        y_b = _diff_attn(pb, pf, bias_b, scal, diff_subln_gain[layer], bsz, t)

        conv_w = jnp.concatenate([dn_conv[layer].astype(F32),
                                  jnp.zeros((SUBLANES - CONV_K, 3 * BRANCH_W), F32)], axis=0)
        decay_lanes = slice(C_HEADS, 2 * C_HEADS)
        par = jnp.zeros((SUBLANES, HEAD_W), F32)
        par = (par.at[0, decay_lanes].set(dn_a_log[layer].astype(F32))
               .at[1, decay_lanes].set(dn_dt_bias[layer].astype(F32)))
        y_c = _delta_net(pf, ps, conv_w, par, dn_norm_gain[layer], bsz, t, tc=DELTA_TILE)

        y_d = _hgrn(pf, hg_lb[layer], hg_norm_gain[layer], bsz, t, tc=HGRN_TILE)

        ys = [y.reshape(n, BRANCH_W) for y in (y_a, y_b, y_c, y_d)]
        last = layer == depth - 1
        res = _merge_out(gates, ys, w_branch[layer].astype(BF16), w_out[layer].astype(BF16), xf,
                         final_gain if last else norm_gain[layer + 1], last)
        if last:
            out = res
        else:
            xf, h = res
    return out.reshape(bsz, t, d).astype(x.dtype)
```
